```python
import math
import jax, jax.numpy as jnp
from jax import lax
import numpy as np

D_MODEL = 2048
BATCH = 2
SEQ = 8192
DEPTH = 4

HEAD_DIM = 64
H_SB = 8
H_DIL_PER_GROUP = 4
DIL_GROUPS = ((128, 1), (512, 4), (2048, 16))
H_DIL = H_DIL_PER_GROUP * len(DIL_GROUPS)
H_MOBA = 8
H_FOX = 8
H_ALL = H_SB + H_DIL + H_MOBA + H_FOX
SB_OFF = 0
DIL_OFF = SB_OFF + H_SB
MOBA_OFF = DIL_OFF + H_DIL
FOX_OFF = MOBA_OFF + H_MOBA
W_MIX = H_ALL * HEAD_DIM
N_BRANCH = 4
C_QKV = 3 * W_MIX
C_IN = C_QKV + H_FOX + N_BRANCH * D_MODEL
Q_BLOCK = 128
MOBA_BLOCK = 256
MOBA_TOPK = 3
MOBA_QBLOCK = 64
N_BUCKETS = 32
MAX_DIST = 2048
H_BIAS = H_DIL + H_MOBA
N_EXPERTS = 32
TOP_K = 4
D_FF = 768
SWIGLU_ALPHA = 1.702
SWIGLU_LIMIT = 7.0
MOE_BLOCK = 128
FORGET_BIAS_INIT = 3.0
LN_EPS = 1e-5
DN_ALPHA = (2 * DEPTH) ** 0.25
DN_BETA = (8 * DEPTH) ** -0.25

kernel_name = 'hybrid_gated_mixers_moe_deepnorm'


def layer_norm(x, g, b):
    xf = x.astype(jnp.float32)
    mu = xf.mean(-1, keepdims=True)
    var = jnp.square(xf - mu).mean(-1, keepdims=True)
    return ((xf - mu) * lax.rsqrt(var + LN_EPS) * g + b).astype(x.dtype)


def t5_bucket(n):
    exact = N_BUCKETS // 2
    nf = jnp.maximum(n, 1).astype(jnp.float32)
    large = exact + (jnp.log(nf / exact) / math.log(MAX_DIST / exact) * (N_BUCKETS - exact)).astype(jnp.int32)
    large = jnp.minimum(large, N_BUCKETS - 1)
    return jnp.where(n < exact, n, large)


def _blocks(t, qb):
    B, H, S, dh = t.shape
    return t.reshape(B, H, S // qb, qb, dh).transpose(2, 0, 1, 3, 4)


def _unblocks(t):
    nb, B, H, qb, dh = t.shape
    return t.transpose(1, 0, 3, 2, 4).reshape(B, nb * qb, H * dh)


def stick_breaking_attention(q, k, v):
    B, H, S, dh = q.shape
    scale = dh ** -0.5
    pos = jnp.arange(S)

    def block(args):
        qb, i = args
        z = jnp.einsum('bhqd,bhkd->bhqk', qb, k).astype(jnp.float32) * scale
        t = i * Q_BLOCK + jnp.arange(Q_BLOCK)
        past = pos[None, :] < t[:, None]
        log_1m = jnp.where(past, jax.nn.log_sigmoid(-z), 0.0)
        after = lax.cumsum(log_1m, axis=3, reverse=True) - log_1m
        w = jnp.where(past, jnp.exp(jax.nn.log_sigmoid(z) + after), 0.0)
        return jnp.einsum('bhqk,bhkd->bhqd', w.astype(v.dtype), v)

    out = lax.map(block, (_blocks(q, Q_BLOCK), jnp.arange(S // Q_BLOCK)))
    return _unblocks(out)


def dilated_group(q, k, v, bias_tab, window, dil):
    B, S, Hg, dh = q.shape
    W = window // dil
    L = S // dil
    nb = -(-L // W)
    Lp = nb * W
    BB = B * dil

    def to_sub(t):
        return t.reshape(B, L, dil, Hg, dh).transpose(0, 2, 3, 1, 4).reshape(BB, Hg, L, dh)

    qs = jnp.pad(to_sub(q), ((0, 0), (0, 0), (0, Lp - L), (0, 0))).reshape(BB, Hg, nb, W, dh)

    def key_blocks(t):
        tp = jnp.pad(to_sub(t), ((0, 0), (0, 0), (W, Lp - L), (0, 0)))
        prev = tp[:, :, :Lp].reshape(BB, Hg, nb, W, dh)
        cur = tp[:, :, W:].reshape(BB, Hg, nb, W, dh)
        return jnp.concatenate([prev, cur], axis=3)

    kb, vb = key_blocks(k), key_blocks(v)
    z = jnp.einsum('bhnqd,bhnkd->bhnqk', qs, kb).astype(jnp.float32) * (dh ** -0.5)
    qi = jnp.arange(W)
    kj = jnp.arange(2 * W)
    steps = qi[:, None] + W - kj[None, :]
    key_idx = jnp.arange(nb)[:, None, None] * W - W + kj[None, None, :]
    valid = (steps >= 0) & (steps <= W) & (key_idx >= 0)
    bias = bias_tab[t5_bucket(jnp.maximum(steps, 0) * dil)].transpose(2, 0, 1).astype(jnp.float32)
    z = jnp.where(valid, z + bias[:, None], -jnp.inf)
    lse = jax.nn.logsumexp(z, axis=-1)
    p = jnp.exp(z - lse[..., None])
    o = jnp.einsum('bhnqk,bhnkd->bhnqd', p.astype(v.dtype), vb)
    o = o.reshape(BB, Hg, Lp, dh)[:, :, :L]
    lse = lse.reshape(BB, Hg, Lp)[:, :, :L]
    o = o.reshape(B, dil, Hg, L, dh).transpose(0, 3, 1, 2, 4).reshape(B, S, Hg, dh)
    lse = lse.reshape(B, dil, Hg, L).transpose(0, 3, 1, 2).reshape(B, S, Hg)
    return o, lse


def dilated_attention(q, k, v, bias_tab):
    B, S = q.shape[:2]
    outs, lses = [], []
    for g, (window, dil) in enumerate(DIL_GROUPS):
        sl = slice(g * H_DIL_PER_GROUP, (g + 1) * H_DIL_PER_GROUP)
        o, lse = dilated_group(q[:, :, sl], k[:, :, sl], v[:, :, sl], bias_tab[:, sl], window, dil)
        outs.append(o)
        lses.append(lse)
    wts = jax.nn.softmax(jnp.stack(lses, 0), axis=0)
    o = jnp.einsum('gbsh,gbshd->bshd', wts.astype(q.dtype), jnp.stack(outs, 0))
    return o.reshape(B, S, H_DIL_PER_GROUP * HEAD_DIM)


def moba_attention(q, k, v, bias_tab):
    B, H, S, dh = q.shape
    MB = MOBA_BLOCK
    nblk = -(-S // MB)
    Sp = nblk * MB
    pad = ((0, 0), (0, 0), (0, Sp - S), (0, 0))
    kp, vp = jnp.pad(k, pad), jnp.pad(v, pad)
    k_blocks = kp.reshape(B, H, nblk, MB, dh)
    v_blocks = vp.reshape(B, H, nblk, MB, dh)
    k_mean = k_blocks.astype(jnp.float32).mean(axis=3).astype(q.dtype)
    n_sel = min(MOBA_TOPK, nblk)
    bias_t = bias_tab.T.astype(jnp.float32)
    h_idx = jnp.arange(H)[None, :, None, None, None]
    take = jax.vmap(jax.vmap(lambda blocks, idx: blocks[idx]))
    scale = dh ** -0.5

    def block(args):
        qb, i = args
        q0 = i * MOBA_QBLOCK
        t = q0 + jnp.arange(MOBA_QBLOCK)
        own = q0 // MB
        gate = jnp.einsum('bhqd,bhnd->bhqn', qb, k_mean).astype(jnp.float32)
        gate = jnp.where(jnp.arange(nblk) < own, gate, -jnp.inf)
        _, sel = lax.top_k(gate, n_sel)
        sel_ok = sel < own
        kg = take(k_blocks, sel)
        vg = take(v_blocks, sel)
        z_sel = jnp.einsum('bhqd,bhqnkd->bhqnk', qb, kg).astype(jnp.float32) * scale
        dist_sel = t[:, None, None] - (sel[..., None] * MB + jnp.arange(MB))
        z_sel = z_sel + bias_t[h_idx, t5_bucket(jnp.maximum(dist_sel, 0))]
        z_sel = jnp.where(sel_ok[..., None], z_sel, -jnp.inf).reshape(B, H, MOBA_QBLOCK, n_sel * MB)
        k_own = lax.dynamic_slice_in_dim(kp, own * MB, MB, axis=2)
        v_own = lax.dynamic_slice_in_dim(vp, own * MB, MB, axis=2)
        dist_own = t[:, None] - (own * MB + jnp.arange(MB))[None, :]
        z_own = (jnp.einsum('bhqd,bhkd->bhqk', qb, k_own).astype(jnp.float32) * scale
                 + bias_t[:, t5_bucket(jnp.maximum(dist_own, 0))])
        z_own = jnp.where(dist_own >= 0, z_own, -jnp.inf)
        p = jax.nn.softmax(jnp.concatenate([z_sel, z_own], axis=-1), axis=-1).astype(v.dtype)
        p_sel = p[..., :n_sel * MB].reshape(B, H, MOBA_QBLOCK, n_sel, MB)
        return (jnp.einsum('bhqnk,bhqnkd->bhqd', p_sel, vg)
                + jnp.einsum('bhqk,bhkd->bhqd', p[..., n_sel * MB:], v_own))

    out = lax.map(block, (_blocks(q, MOBA_QBLOCK), jnp.arange(S // MOBA_QBLOCK)))
    return _unblocks(out)


def forgetting_attention(q, k, v, f_logit):
    B, H, S, dh = q.shape
    scale = dh ** -0.5
    log_f = jax.nn.log_sigmoid(f_logit.astype(jnp.float32)).transpose(0, 2, 1)
    c = jnp.cumsum(log_f, axis=2)
    c_blocks = c.reshape(B, H, S // Q_BLOCK, Q_BLOCK).transpose(2, 0, 1, 3)
    pos = jnp.arange(S)

    def block(args):
        qb, cb, i = args
        t = i * Q_BLOCK + jnp.arange(Q_BLOCK)
        z = (jnp.einsum('bhqd,bhkd->bhqk', qb, k).astype(jnp.float32) * scale
             + (cb[..., :, None] - c[:, :, None, :]))
        z = jnp.where(pos[None, :] <= t[:, None], z, -jnp.inf)
        p = jax.nn.softmax(z, axis=-1).astype(v.dtype)
        return jnp.einsum('bhqk,bhkd->bhqd', p, v)

    out = lax.map(block, (_blocks(q, Q_BLOCK), c_blocks, jnp.arange(S // Q_BLOCK)))
    return _unblocks(out)


def token_mixers(x, rel_bias, w_in, b_forget, b_gate, w_br_sb, w_br_dil, w_br_moba, w_br_fox, w_out):
    B, S, D = x.shape
    h = x @ w_in
    q = h[..., :W_MIX].reshape(B, S, H_ALL, HEAD_DIM)
    k = h[..., W_MIX:2 * W_MIX].reshape(B, S, H_ALL, HEAD_DIM)
    v = h[..., 2 * W_MIX:C_QKV].reshape(B, S, H_ALL, HEAD_DIM)
    f_logit = h[..., C_QKV:C_QKV + H_FOX] + b_forget
    gates = jax.nn.sigmoid((h[..., C_QKV + H_FOX:] + b_gate).astype(jnp.float32)).astype(x.dtype)
    gates = gates.reshape(B, S, N_BRANCH, D)

    def heads(t, lo, n):
        return t[:, :, lo:lo + n].transpose(0, 2, 1, 3)

    o_sb = stick_breaking_attention(heads(q, SB_OFF, H_SB), heads(k, SB_OFF, H_SB), heads(v, SB_OFF, H_SB))
    dsl = slice(DIL_OFF, DIL_OFF + H_DIL)
    o_dil = dilated_attention(q[:, :, dsl], k[:, :, dsl], v[:, :, dsl], rel_bias[:, :H_DIL])
    o_moba = moba_attention(heads(q, MOBA_OFF, H_MOBA), heads(k, MOBA_OFF, H_MOBA),
                            heads(v, MOBA_OFF, H_MOBA), rel_bias[:, H_DIL:])
    o_fox = forgetting_attention(heads(q, FOX_OFF, H_FOX), heads(k, FOX_OFF, H_FOX),
                                 heads(v, FOX_OFF, H_FOX), f_logit)
    merged = (gates[:, :, 0] * (o_sb @ w_br_sb) + gates[:, :, 1] * (o_dil @ w_br_dil)
              + gates[:, :, 2] * (o_moba @ w_br_moba) + gates[:, :, 3] * (o_fox @ w_br_fox))
    return merged @ w_out


def clamped_swiglu(hcat):
    glu, lin = hcat[..., :D_FF], hcat[..., D_FF:]
    glu = jnp.minimum(glu, SWIGLU_LIMIT)
    lin = jnp.clip(lin, -SWIGLU_LIMIT, SWIGLU_LIMIT)
    return glu * jax.nn.sigmoid(SWIGLU_ALPHA * glu) * (lin + 1)


def moe_ffn(x, w_router, b_router, w_up, b_up, w_down, b_down):
    B, S, D = x.shape
    N = B * S
    NK = N * TOP_K
    xf = x.reshape(N, D)
    logits = (xf @ w_router).astype(jnp.float32) + b_router.astype(jnp.float32)
    top_val, top_idx = lax.top_k(logits, TOP_K)
    gate = jax.nn.softmax(top_val, axis=-1)
    e_flat = top_idx.reshape(NK)
    tok_flat = jnp.arange(NK, dtype=jnp.int32) // TOP_K
    g_flat = gate.reshape(NK)
    order = jnp.argsort(e_flat)
    e_s, tok_s, g_s = e_flat[order], tok_flat[order], g_flat[order]
    counts = jax.ops.segment_sum(jnp.ones((NK,), jnp.int32), e_flat, num_segments=N_EXPERTS)
    start = jnp.cumsum(counts) - counts
    padded = (counts + MOE_BLOCK - 1) // MOE_BLOCK * MOE_BLOCK
    pend = jnp.cumsum(padded)
    pstart = pend - padded
    slot = pstart[e_s] + (jnp.arange(NK, dtype=jnp.int32) - start[e_s])
    n_blocks = -(-(NK + N_EXPERTS * (MOE_BLOCK - 1)) // MOE_BLOCK)
    P = n_blocks * MOE_BLOCK
    slot_tok = jnp.zeros((P,), jnp.int32).at[slot].set(tok_s)
    slot_gate = jnp.zeros((P,), jnp.float32).at[slot].set(g_s)
    blk_exp = jnp.minimum(jnp.searchsorted(pend, jnp.arange(n_blocks, dtype=jnp.int32) * MOE_BLOCK, side='right'),
                          N_EXPERTS - 1)

    def expert_block(args):
        toks, gts, e = args
        xb = xf[toks]
        hh = clamped_swiglu(xb @ w_up[e] + b_up[e])
        y = hh @ w_down[e] + b_down[e]
        return y * gts[:, None].astype(y.dtype)

    y = lax.map(expert_block, (slot_tok.reshape(n_blocks, MOE_BLOCK), slot_gate.reshape(n_blocks, MOE_BLOCK), blk_exp))
    out = jax.ops.segment_sum(y.reshape(P, D), slot_tok, num_segments=N)
    return out.reshape(B, S, D)


def setup_inputs(seed: int = 0) -> dict:
    key = jax.random.key(seed)
    ks = jax.random.split(key, 24)
    f32 = jnp.float32

    def nrm(k, shape, scale):
        return jax.random.normal(k, shape, f32) * scale

    w_dil_out = H_DIL_PER_GROUP * HEAD_DIM
    col_scale = jnp.concatenate([jnp.ones((2 * W_MIX,), f32), jnp.full((W_MIX,), DN_BETA, f32),
                                 jnp.full((H_FOX,), 0.5, f32), jnp.ones((N_BRANCH * D_MODEL,), f32)])
    return {
        'x': nrm(ks[0], (BATCH, SEQ, D_MODEL), 1.0),
        'rel_bias': nrm(ks[1], (N_BUCKETS, H_BIAS), 0.5),
        'w_in': nrm(ks[2], (DEPTH, D_MODEL, C_IN), D_MODEL ** -0.5) * col_scale,
        'b_forget': FORGET_BIAS_INIT + nrm(ks[3], (DEPTH, H_FOX), 0.1),
        'b_gate': nrm(ks[4], (DEPTH, N_BRANCH * D_MODEL), 0.02),
        'w_br_sb': nrm(ks[5], (DEPTH, H_SB * HEAD_DIM, D_MODEL), (H_SB * HEAD_DIM) ** -0.5 * DN_BETA),
        'w_br_dil': nrm(ks[6], (DEPTH, w_dil_out, D_MODEL), w_dil_out ** -0.5 * DN_BETA),
        'w_br_moba': nrm(ks[7], (DEPTH, H_MOBA * HEAD_DIM, D_MODEL), (H_MOBA * HEAD_DIM) ** -0.5 * DN_BETA),
        'w_br_fox': nrm(ks[8], (DEPTH, H_FOX * HEAD_DIM, D_MODEL), (H_FOX * HEAD_DIM) ** -0.5 * DN_BETA),
        'w_out': nrm(ks[9], (DEPTH, D_MODEL, D_MODEL), D_MODEL ** -0.5 * DN_BETA),
        'ln1_g': 1.0 + nrm(ks[10], (DEPTH, D_MODEL), 0.02),
        'ln1_b': nrm(ks[11], (DEPTH, D_MODEL), 0.02),
        'w_router': nrm(ks[12], (DEPTH, D_MODEL, N_EXPERTS), D_MODEL ** -0.5),
        'b_router': nrm(ks[13], (DEPTH, N_EXPERTS), 0.01),
        'w_up': nrm(ks[14], (DEPTH, N_EXPERTS, D_MODEL, 2 * D_FF), D_MODEL ** -0.5),
        'b_up': nrm(ks[15], (DEPTH, N_EXPERTS, 2 * D_FF), 0.02),
        'w_down': nrm(ks[16], (DEPTH, N_EXPERTS, D_FF, D_MODEL), D_FF ** -0.5 * DN_BETA),
        'b_down': nrm(ks[17], (DEPTH, N_EXPERTS, D_MODEL), 0.02),
        'ln2_g': 1.0 + nrm(ks[18], (DEPTH, D_MODEL), 0.02),
        'ln2_b': nrm(ks[19], (DEPTH, D_MODEL), 0.02),
    }


def reference(x, rel_bias, w_in, b_forget, b_gate, w_br_sb, w_br_dil, w_br_moba, w_br_fox, w_out,
              ln1_g, ln1_b, w_router, b_router, w_up, b_up, w_down, b_down, ln2_g, ln2_b):
    for l in range(DEPTH):
        y = token_mixers(x, rel_bias, w_in[l], b_forget[l], b_gate[l], w_br_sb[l], w_br_dil[l],
                         w_br_moba[l], w_br_fox[l], w_out[l])
        x = layer_norm(DN_ALPHA * x + y, ln1_g[l], ln1_b[l])
        y = moe_ffn(x, w_router[l], b_router[l], w_up[l], b_up[l], w_down[l], b_down[l])
        x = layer_norm(DN_ALPHA * x + y, ln2_g[l], ln2_b[l])
    return x
```

```python
import functools
import math

import numpy as np
import jax
import jax.numpy as jnp
from jax import lax
from jax.experimental import pallas as pl
from jax.experimental.pallas import tpu as pltpu

D_MODEL = 2048
DEPTH = 4
HEAD_DIM = 64
H_SB = 8
H_DIL_PER_GROUP = 4
DIL_GROUPS = ((128, 1), (512, 4), (2048, 16))
H_DIL = H_DIL_PER_GROUP * len(DIL_GROUPS)
H_MOBA = 8
H_FOX = 8
H_ALL = H_SB + H_DIL + H_MOBA + H_FOX
SB_OFF = 0
DIL_OFF = SB_OFF + H_SB
MOBA_OFF = DIL_OFF + H_DIL
FOX_OFF = MOBA_OFF + H_MOBA
W_MIX = H_ALL * HEAD_DIM
N_BRANCH = 4
C_QKV = 3 * W_MIX
MOBA_BLOCK = 256
MOBA_TOPK = 3
N_BUCKETS = 32
MAX_DIST = 2048
N_EXPERTS = 32
TOP_K = 4
D_FF = 768
SWIGLU_ALPHA = 1.702
SWIGLU_LIMIT = 7.0
LN_EPS = 1e-5
DN_ALPHA = (2 * DEPTH) ** 0.25

F32 = jnp.float32
BF16 = jnp.bfloat16
LANES = 128
NEG = -1e30
HIGHEST = lax.Precision.HIGHEST
MIB = 1024 * 1024
NT = (((1,), (1,)), ((), ()))

ATT_BLOCK = 256
DIL_W = 128
MOE_ROWS = 256
MOBA_NT = 8


def _params(sem, vmem_mib):
    return pltpu.CompilerParams(dimension_semantics=sem, vmem_limit_bytes=vmem_mib * MIB)


def _t5_bucket(n):
    exact = N_BUCKETS // 2
    nf = jnp.maximum(n, 1).astype(F32)
    large = exact + (jnp.log(nf / exact) / math.log(MAX_DIST / exact) * (N_BUCKETS - exact)).astype(jnp.int32)
    large = jnp.minimum(large, N_BUCKETS - 1)
    return jnp.where(n < exact, n, large)


def _head_pair(q):
    lane = lax.broadcasted_iota(jnp.int32, q.shape, 1)
    zero = jnp.zeros_like(q)
    return [jnp.where(lane < HEAD_DIM, q, zero), jnp.where(lane >= HEAD_DIM, q, zero)], lane


def _softmax_step(carry, s, v):
    m, l, acc = carry
    m_new = jnp.maximum(m, jnp.max(s, axis=1, keepdims=True))
    alpha = jnp.exp(m - m_new)
    p = jnp.exp(s - m_new)
    l = alpha * l + jnp.sum(p, axis=1, keepdims=True)
    acc = alpha * acc + jnp.dot(p.astype(BF16), v, preferred_element_type=F32)
    return m_new, l, acc


def _softmax_init(rows):
    return (jnp.full((rows, 1), NEG, F32), jnp.zeros((rows, 1), F32), jnp.zeros((rows, LANES), F32))


def _matmul_kernel(a_ref, w_ref, o_ref):
    o_ref[...] = jnp.dot(a_ref[...], w_ref[...], preferred_element_type=F32).astype(o_ref.dtype)


def _matmul(a, w, tm, tn, out_dtype):
    m, k = a.shape
    n = w.shape[1]
    return pl.pallas_call(
        _matmul_kernel,
        grid=(n // tn, m // tm),
        in_specs=[pl.BlockSpec((tm, k), lambda j, i: (i, 0)),
                  pl.BlockSpec((k, tn), lambda j, i: (0, j))],
        out_specs=pl.BlockSpec((tm, tn), lambda j, i: (i, j)),
        out_shape=jax.ShapeDtypeStruct((m, n), out_dtype),
        compiler_params=_params(("parallel", "parallel"), 40),
        name="qkv_proj",
    )(a, w)


def _decay_kernel(x_ref, wf_ref, bf_ref, c_ref, carry_ref):
    @pl.when(pl.program_id(1) == 0)
    def _():
        carry_ref[...] = jnp.zeros_like(carry_ref)

    tm = x_ref.shape[0]
    f = jnp.dot(x_ref[...], wf_ref[...], preferred_element_type=F32, precision=HIGHEST) + bf_ref[...]
    logf = jnp.minimum(f, 0.0) - jnp.log1p(jnp.exp(-jnp.abs(f)))
    row = lax.broadcasted_iota(jnp.int32, (tm, tm), 0)
    col = lax.broadcasted_iota(jnp.int32, (tm, tm), 1)
    tri = jnp.where(col <= row, 1.0, 0.0).astype(F32)
    c = jnp.dot(tri, logf, preferred_element_type=F32, precision=HIGHEST) + carry_ref[...]
    c_ref[...] = c
    carry_ref[...] = c[tm - 1:tm, :]


def _decay(x, w_f, b_f, batch, tm=256):
    n, d = x.shape
    nb = n // batch // tm
    return pl.pallas_call(
        _decay_kernel,
        grid=(batch, nb),
        in_specs=[pl.BlockSpec((tm, d), lambda b, i: (b * nb + i, 0)),
                  pl.BlockSpec((d, LANES), lambda b, i: (0, 0)),
                  pl.BlockSpec((1, LANES), lambda b, i: (0, 0))],
        out_specs=pl.BlockSpec((tm, LANES), lambda b, i: (b * nb + i, 0)),
        out_shape=jax.ShapeDtypeStruct((n, LANES), F32),
        scratch_shapes=[pltpu.VMEM((1, LANES), F32)],
        compiler_params=_params(("arbitrary", "arbitrary"), 24),
        name="fox_decay",
    )(x, w_f, b_f)


def _sb_kernel(q_ref, k_ref, v_ref, o_ref):
    t = ATT_BLOCK
    i = pl.program_id(2)
    qh, lane = _head_pair(q_ref[...])
    row = lax.broadcasted_iota(jnp.int32, (t, t), 0)
    col = lax.broadcasted_iota(jnp.int32, (t, t), 1)
    later = jnp.where(row > col, 1.0, 0.0).astype(BF16)
    past = col < row

    def step(j, carry, diagonal):
        start = pl.multiple_of(j * t, t)
        k = k_ref[pl.ds(start, t), :]
        v = v_ref[pl.ds(start, t), :]
        out = []
        for h in range(2):
            tail, acc = carry[h]
            z = lax.dot_general(qh[h], k, NT, preferred_element_type=F32)
            log_1m = -(jnp.maximum(z, 0.0) + jnp.log(1.0 + jnp.exp(-jnp.abs(z))))
            log_1m_in = jnp.where(past, log_1m, 0.0) if diagonal else log_1m
            hi = log_1m_in.astype(BF16)
            lo = (log_1m_in - hi.astype(F32)).astype(BF16)
            after = (jnp.dot(hi, later, preferred_element_type=F32)
                     + jnp.dot(lo, later, preferred_element_type=F32) + tail)
            w = jnp.exp(z + log_1m + after)
            if diagonal:
                w = jnp.where(past, w, 0.0)
            acc = acc + jnp.dot(w.astype(BF16), v, preferred_element_type=F32)
            tail = after[:, 0:1] + log_1m_in[:, 0:1]
            out.append((tail, acc))
        return tuple(out)

    init = tuple((jnp.zeros((t, 1), F32), jnp.zeros((t, LANES), F32)) for _ in range(2))
    carry = step(i, init, True)
    carry = lax.fori_loop(0, i, lambda jj, c: step(i - 1 - jj, c, False), carry)
    o_ref[...] = jnp.where(lane < HEAD_DIM, carry[0][1], carry[1][1]).astype(o_ref.dtype)


def _attention_specs(seq, nq, tq, q_col, k_col, v_col):
    return [pl.BlockSpec((tq, LANES), lambda b, p, i: (b * nq + i, q_col + p)),
            pl.BlockSpec((seq, LANES), lambda b, p, i: (b, k_col + p)),
            pl.BlockSpec((seq, LANES), lambda b, p, i: (b, v_col + p))]


def _head_cols(off):
    q_col = off * HEAD_DIM // LANES
    return q_col, q_col + W_MIX // LANES, q_col + 2 * W_MIX // LANES


def _sb_attention(qkv, batch):
    n = qkv.shape[0]
    seq = n // batch
    nq = seq // ATT_BLOCK
    pairs = H_SB // 2
    return pl.pallas_call(
        _sb_kernel,
        grid=(batch, pairs, nq),
        in_specs=_attention_specs(seq, nq, ATT_BLOCK, *_head_cols(SB_OFF)),
        out_specs=pl.BlockSpec((ATT_BLOCK, LANES), lambda b, p, i: (b * nq + i, p)),
        out_shape=jax.ShapeDtypeStruct((n, pairs * LANES), BF16),
        compiler_params=_params(("parallel", "parallel", "arbitrary"), 40),
        name="sb_attention",
    )(qkv, qkv, qkv)


def _fox_kernel(q_ref, k_ref, v_ref, cq_ref, ck_ref, o_ref):
    t = ATT_BLOCK
    p_id = pl.program_id(1)
    i = pl.program_id(2)
    qh, lane = _head_pair(q_ref[...])
    cq_all = cq_ref[...]
    cq = [jnp.sum(jnp.where(lane == 2 * p_id + h, cq_all, 0.0), axis=1, keepdims=True) for h in range(2)]
    row = lax.broadcasted_iota(jnp.int32, (t, t), 0)
    col = lax.broadcasted_iota(jnp.int32, (t, t), 1)
    causal = col <= row

    def step(j, carry, diagonal):
        start = pl.multiple_of(j * t, t)
        k = k_ref[pl.ds(start, t), :]
        v = v_ref[pl.ds(start, t), :]
        out = []
        for h in range(2):
            z = lax.dot_general(qh[h], k, NT, preferred_element_type=F32)
            s = z + (cq[h] - ck_ref[h:h + 1, pl.ds(start, t)])
            if diagonal:
                s = jnp.where(causal, s, NEG)
            out.append(_softmax_step(carry[h], s, v))
        return tuple(out)

    carry = tuple(_softmax_init(t) for _ in range(2))
    carry = lax.fori_loop(0, i, lambda j, c: step(j, c, False), carry)
    carry = step(i, carry, True)
    o = [carry[h][2] / carry[h][1] for h in range(2)]
    o_ref[...] = jnp.where(lane < HEAD_DIM, o[0], o[1]).astype(o_ref.dtype)


def _fox_attention(qkv, c, c_t, batch):
    n = qkv.shape[0]
    seq = n // batch
    nq = seq // ATT_BLOCK
    pairs = H_FOX // 2
    specs = _attention_specs(seq, nq, ATT_BLOCK, *_head_cols(FOX_OFF))
    specs += [pl.BlockSpec((ATT_BLOCK, LANES), lambda b, p, i: (b * nq + i, 0)),
              pl.BlockSpec((None, None, 2, seq), lambda b, p, i: (b, p, 0, 0))]
    return pl.pallas_call(
        _fox_kernel,
        grid=(batch, pairs, nq),
        in_specs=specs,
        out_specs=pl.BlockSpec((ATT_BLOCK, LANES), lambda b, p, i: (b * nq + i, p)),
        out_shape=jax.ShapeDtypeStruct((n, pairs * LANES), BF16),
        compiler_params=_params(("parallel", "parallel", "arbitrary"), 40),
        name="fox_attention",
    )(qkv, qkv, qkv, c, c_t)


def _moba_kernel(q_ref, k_ref, v_ref, bias_ref, o_ref, kmean_ref):
    t = MOBA_BLOCK
    seq = k_ref.shape[0]
    i = pl.program_id(2)

    @pl.when(i == 0)
    def _():
        blk = lax.broadcasted_iota(jnp.int32, (LANES, seq), 0)
        pos = lax.broadcasted_iota(jnp.int32, (LANES, seq), 1)
        member = jnp.where((pos >= blk * t) & (pos < (blk + 1) * t), 1.0, 0.0).astype(BF16)
        total = jnp.dot(member, k_ref[...], preferred_element_type=F32)
        mean = total * (1.0 / t)
        hi = mean.astype(BF16)
        kmean_ref[0] = hi
        kmean_ref[1] = (mean - hi.astype(F32)).astype(BF16)

    qh, lane = _head_pair(q_ref[...])
    kmean_hi = kmean_ref[0]
    kmean_lo = kmean_ref[1]
    blk_id = lax.broadcasted_iota(jnp.int32, (t, LANES), 1).astype(F32)
    fully_past = blk_id < i.astype(F32)
    row = lax.broadcasted_iota(jnp.int32, (t, t), 0)
    col = lax.broadcasted_iota(jnp.int32, (t, t), 1)
    causal = col <= row

    chosen = []
    for h in range(2):
        gate = (lax.dot_general(qh[h], kmean_hi, NT, preferred_element_type=F32)
                + lax.dot_general(qh[h], kmean_lo, NT, preferred_element_type=F32))
        gate = jnp.where(fully_past, gate, -jnp.inf)
        picked = jnp.zeros((t, LANES), jnp.bool_)
        for _ in range(MOBA_TOPK):
            best = jnp.max(gate, axis=1, keepdims=True)
            first = jnp.min(jnp.where(gate == best, blk_id, float(LANES)), axis=1, keepdims=True)
            pick = blk_id == first
            picked = picked | pick
            gate = jnp.where(pick, -jnp.inf, gate)
        chosen.append(jnp.where(picked & fully_past, 1.0, 0.0))

    def scores(j, h, k, tile):
        z = lax.dot_general(qh[h], k, NT, preferred_element_type=F32)
        return z + bias_ref[tile, h]

    own = pl.multiple_of(i * t, t)
    k_own = k_ref[pl.ds(own, t), :]
    v_own = v_ref[pl.ds(own, t), :]
    carry = []
    for h in range(2):
        s = jnp.where(causal, scores(i, h, k_own, 0), NEG)
        carry.append(_softmax_step(_softmax_init(t), s, v_own))

    def step(j, carry):
        start = pl.multiple_of(j * t, t)
        k = k_ref[pl.ds(start, t), :]
        v = v_ref[pl.ds(start, t), :]
        tile = jnp.minimum(i - j, MOBA_NT)
        out = []
        for h in range(2):
            hit = jnp.sum(jnp.where(blk_id == j.astype(F32), chosen[h], 0.0), axis=1, keepdims=True) > 0.5
            s = jnp.where(hit, scores(j, h, k, tile), NEG)
            out.append(_softmax_step(carry[h], s, v))
        return tuple(out)

    carry = lax.fori_loop(0, i, step, tuple(carry))
    o = [carry[h][2] / carry[h][1] for h in range(2)]
    o_ref[...] = jnp.where(lane < HEAD_DIM, o[0], o[1]).astype(o_ref.dtype)


def _moba_attention(qkv, bias_tiles, batch):
    n = qkv.shape[0]
    seq = n // batch
    assert seq % MOBA_BLOCK == 0 and seq // MOBA_BLOCK <= LANES
    nq = seq // MOBA_BLOCK
    pairs = H_MOBA // 2
    specs = _attention_specs(seq, nq, MOBA_BLOCK, *_head_cols(MOBA_OFF))
    specs.append(pl.BlockSpec((None, MOBA_NT + 1, 2, MOBA_BLOCK, MOBA_BLOCK), lambda b, p, i: (p, 0, 0, 0, 0)))
    return pl.pallas_call(
        _moba_kernel,
        grid=(batch, pairs, nq),
        in_specs=specs,
        out_specs=pl.BlockSpec((MOBA_BLOCK, LANES), lambda b, p, i: (b * nq + i, p)),
        out_shape=jax.ShapeDtypeStruct((n, pairs * LANES), BF16),
        scratch_shapes=[pltpu.VMEM((2, LANES, LANES), BF16)],
        compiler_params=_params(("arbitrary", "arbitrary", "arbitrary"), 48),
        name="moba_attention",
    )(qkv, qkv, qkv, bias_tiles)


def _moba_bias_tiles(rel_bias):
    r = jnp.arange(MOBA_BLOCK)
    dist = (jnp.arange(MOBA_NT + 1)[:, None, None] * MOBA_BLOCK + r[None, :, None] - r[None, None, :])
    tiles = rel_bias[:, H_DIL:][_t5_bucket(jnp.maximum(dist, 0))]
    tiles = tiles.transpose(3, 0, 1, 2).reshape(H_MOBA // 2, 2, MOBA_NT + 1, MOBA_BLOCK, MOBA_BLOCK)
    return tiles.transpose(0, 2, 1, 3, 4).astype(F32)


def _check_moba_saturation():
    n = np.float32(MOBA_NT * MOBA_BLOCK - (MOBA_BLOCK - 1))
    exact = N_BUCKETS // 2
    large = exact + int(np.log(n / np.float32(exact)) / math.log(MAX_DIST / exact) * (N_BUCKETS - exact))
    assert large >= N_BUCKETS - 1, "MOBA_NT too small for the bias bucket table"


_check_moba_saturation()


def _dilated_kernel(q_ref, k_ref, v_ref, bias_ref, o_ref, lse_ref):
    w = DIL_W
    nb = q_ref.shape[0] // w
    row = lax.broadcasted_iota(jnp.int32, (w, w), 0)
    col = lax.broadcasted_iota(jnp.int32, (w, w), 1)
    in_cur = col <= row
    in_prev = col >= row

    def body(n, _):
        cur = pl.multiple_of(n * w, w)
        prev = pl.multiple_of(jnp.maximum(n - 1, 0) * w, w)
        qh, lane = _head_pair(q_ref[pl.ds(cur, w), :])
        k_cur, v_cur = k_ref[pl.ds(cur, w), :], v_ref[pl.ds(cur, w), :]
        k_prev, v_prev = k_ref[pl.ds(prev, w), :], v_ref[pl.ds(prev, w), :]
        has_prev = n > 0
        outs, lses = [], []
        for h in range(2):
            z_prev = lax.dot_general(qh[h], k_prev, NT, preferred_element_type=F32) + bias_ref[h, :, 0:w]
            z_cur = lax.dot_general(qh[h], k_cur, NT, preferred_element_type=F32) + bias_ref[h, :, w:2 * w]
            z_prev = jnp.where(in_prev & has_prev, z_prev, NEG)
            z_cur = jnp.where(in_cur, z_cur, NEG)
            m = jnp.maximum(jnp.max(z_prev, axis=1, keepdims=True), jnp.max(z_cur, axis=1, keepdims=True))
            p_prev = jnp.exp(z_prev - m)
            p_cur = jnp.exp(z_cur - m)
            l = jnp.sum(p_prev, axis=1, keepdims=True) + jnp.sum(p_cur, axis=1, keepdims=True)
            o = (jnp.dot(p_prev.astype(BF16), v_prev, preferred_element_type=F32)
                 + jnp.dot(p_cur.astype(BF16), v_cur, preferred_element_type=F32))
            outs.append(o / l)
            lses.append(m + jnp.log(l))
        o_ref[pl.ds(cur, w), :] = jnp.where(lane < HEAD_DIM, outs[0], outs[1])
        lse_ref[pl.ds(cur, w), :] = jnp.where(lane < HEAD_DIM, lses[0], lses[1])
        return 0

    lax.fori_loop(0, nb, body, 0)


def _dilated_group(qkv, bias, batch, group):
    window, dil = DIL_GROUPS[group]
    assert window // dil == DIL_W
    n, c = qkv.shape
    seq = n // batch
    sub = seq // dil
    assert seq % dil == 0 and sub % DIL_W == 0
    cb = c // LANES
    pairs = H_DIL_PER_GROUP // 2
    q_col, k_col, v_col = _head_cols(DIL_OFF + group * H_DIL_PER_GROUP)
    view = qkv.reshape(batch * sub, dil * c)

    def spec(col0):
        return pl.BlockSpec((sub, LANES), lambda b, p, r: (b, r * cb + col0 + p))

    out_spec = pl.BlockSpec((sub, LANES), lambda b, p, r: (b, r * pairs + p))
    out_shape = jax.ShapeDtypeStruct((batch * sub, dil * pairs * LANES), F32)
    o, lse = pl.pallas_call(
        _dilated_kernel,
        grid=(batch, pairs, dil),
        in_specs=[spec(q_col), spec(k_col), spec(v_col),
                  pl.BlockSpec((None, 2, DIL_W, 2 * DIL_W), lambda b, p, r: (group * pairs + p, 0, 0, 0))],
        out_specs=[out_spec, out_spec],
        out_shape=[out_shape, out_shape],
        compiler_params=_params(("parallel", "parallel", "parallel"), 48),
        name=f"dilated_attention_g{group}",
    )(view, view, view, bias)
    return o.reshape(n, pairs * LANES), lse.reshape(n, pairs * LANES)


def _dilated_bias_tiles(rel_bias):
    steps = jnp.arange(DIL_W)[:, None] + DIL_W - jnp.arange(2 * DIL_W)[None, :]
    tiles = []
    for g, (_, dil) in enumerate(DIL_GROUPS):
        tab = rel_bias[:, g * H_DIL_PER_GROUP:(g + 1) * H_DIL_PER_GROUP]
        tiles.append(tab[_t5_bucket(jnp.maximum(steps, 0) * dil)].transpose(2, 0, 1))
    tiles = jnp.stack(tiles, 0).astype(F32)
    return tiles.reshape(len(DIL_GROUPS) * H_DIL_PER_GROUP // 2, 2, DIL_W, 2 * DIL_W)


def _dilated_combine_kernel(o0, o1, o2, l0, l1, l2, out_ref):
    lse = [l0[...], l1[...], l2[...]]
    m = jnp.maximum(jnp.maximum(lse[0], lse[1]), lse[2])
    e = [jnp.exp(x - m) for x in lse]
    den = e[0] + e[1] + e[2]
    out = (e[0] / den) * o0[...] + (e[1] / den) * o1[...] + (e[2] / den) * o2[...]
    out_ref[...] = out.astype(out_ref.dtype)


def _dilated_combine(os, lses, tm=1024):
    n, c = os[0].shape
    spec = pl.BlockSpec((tm, c), lambda i: (i, 0))
    return pl.pallas_call(
        _dilated_combine_kernel,
        grid=(n // tm,),
        in_specs=[spec] * 6,
        out_specs=spec,
        out_shape=jax.ShapeDtypeStruct((n, c), BF16),
        compiler_params=_params(("parallel",), 32),
        name="dilated_combine",
    )(*os, *lses)


def _merge_kernel(x_ref, o_sb, o_dil, o_moba, o_fox, wg0, wg1, wg2, wg3, bg_ref,
                  w_sb, w_dil, w_moba, w_fox, out_ref):
    x = x_ref[...]
    acc = None
    branches = ((o_sb, w_sb, wg0), (o_dil, w_dil, wg1), (o_moba, w_moba, wg2), (o_fox, w_fox, wg3))
    for b, (o_ref, w_ref, wg_ref) in enumerate(branches):
        gate = jax.nn.sigmoid(jnp.dot(x, wg_ref[...], preferred_element_type=F32) + bg_ref[b])
        term = gate * jnp.dot(o_ref[...], w_ref[...], preferred_element_type=F32)
        acc = term if acc is None else acc + term
    out_ref[...] = acc.astype(out_ref.dtype)


def _gated_merge(x_bf, outs, w_gate, b_gate, w_branch, tm=512, tn=512):
    n, d = x_bf.shape
    nj = d // tn
    row = lambda width: pl.BlockSpec((tm, width), lambda j, i: (i, 0))
    in_specs = [row(d)] + [row(o.shape[1]) for o in outs]
    in_specs += [pl.BlockSpec((d, tn), functools.partial(lambda j, i, b: (0, b * nj + j), b=b)) for b in range(N_BRANCH)]
    in_specs.append(pl.BlockSpec((N_BRANCH, 1, tn), lambda j, i: (0, 0, j)))
    in_specs += [pl.BlockSpec((w.shape[0], tn), lambda j, i: (0, j)) for w in w_branch]
    return pl.pallas_call(
        _merge_kernel,
        grid=(nj, n // tm),
        in_specs=in_specs,
        out_specs=pl.BlockSpec((tm, tn), lambda j, i: (i, j)),
        out_shape=jax.ShapeDtypeStruct((n, d), BF16),
        compiler_params=_params(("parallel", "parallel"), 48),
        name="gated_merge",
    )(x_bf, *outs, w_gate, w_gate, w_gate, w_gate, b_gate, *w_branch)


def _layer_norm_store(z, g_ref, b_ref, xo_ref, xb_ref):
    mu = jnp.mean(z, axis=1, keepdims=True)
    zc = z - mu
    var = jnp.mean(zc * zc, axis=1, keepdims=True)
    out = zc * lax.rsqrt(var + LN_EPS) * g_ref[...] + b_ref[...]
    xo_ref[...] = out
    xb_ref[...] = out.astype(BF16)


def _out_ln_kernel(m_ref, w_ref, x_ref, g_ref, b_ref, xo_ref, xb_ref):
    y = jnp.dot(m_ref[...], w_ref[...], preferred_element_type=F32)
    _layer_norm_store(DN_ALPHA * x_ref[...] + y, g_ref, b_ref, xo_ref, xb_ref)


def _add_ln_kernel(y_ref, x_ref, g_ref, b_ref, xo_ref, xb_ref):
    _layer_norm_store(DN_ALPHA * x_ref[...] + y_ref[...], g_ref, b_ref, xo_ref, xb_ref)


def _ln_call(kernel, lead_specs, lead_args, x, g, b, tm, name):
    n, d = x.shape
    row = pl.BlockSpec((tm, d), lambda i: (i, 0))
    vec = pl.BlockSpec((1, d), lambda i: (0, 0))
    return pl.pallas_call(
        kernel,
        grid=(n // tm,),
        in_specs=lead_specs + [row, vec, vec],
        out_specs=[row, row],
        out_shape=[jax.ShapeDtypeStruct((n, d), F32), jax.ShapeDtypeStruct((n, d), BF16)],
        compiler_params=_params(("parallel",), 48),
        name=name,
    )(*lead_args, x, g, b)


def _out_proj_ln(merged, w_out, x, g, b, tm=256):
    d = x.shape[1]
    specs = [pl.BlockSpec((tm, d), lambda i: (i, 0)), pl.BlockSpec((d, d), lambda i: (0, 0))]
    return _ln_call(_out_ln_kernel, specs, (merged, w_out), x, g, b, tm, "out_proj_ln")


def _add_ln(y, x, g, b, tm=256):
    d = x.shape[1]
    return _ln_call(_add_ln_kernel, [pl.BlockSpec((tm, d), lambda i: (i, 0))], (y,), x, g, b, tm, "moe_add_ln")


def _router_kernel(x_ref, w_ref, b_ref, idx_ref, gate_ref):
    logits = jnp.dot(x_ref[...], w_ref[...], preferred_element_type=F32, precision=HIGHEST) + b_ref[...]
    col = lax.broadcasted_iota(jnp.int32, logits.shape, 1)
    colf = col.astype(F32)
    logits = jnp.where(col < N_EXPERTS, logits, -jnp.inf)
    vals, idxs = [], []
    for _ in range(TOP_K):
        best = jnp.max(logits, axis=1, keepdims=True)
        first = jnp.min(jnp.where(logits == best, colf, float(LANES)), axis=1, keepdims=True)
        logits = jnp.where(colf == first, -jnp.inf, logits)
        vals.append(best)
        idxs.append(first)
    es = [jnp.exp(v - vals[0]) for v in vals]
    den = es[0] + es[1] + es[2] + es[3]
    idx_out = jnp.zeros(logits.shape, F32)
    gate_out = jnp.zeros(logits.shape, F32)
    for r in range(TOP_K):
        idx_out = jnp.where(col == r, idxs[r], idx_out)
        gate_out = jnp.where(col == r, es[r] / den, gate_out)
    idx_ref[...] = idx_out.astype(jnp.int32)
    gate_ref[...] = gate_out


def _router(x, w_r, b_r, tm=512):
    n, d = x.shape
    out = pl.BlockSpec((tm, LANES), lambda i: (i, 0))
    return pl.pallas_call(
        _router_kernel,
        grid=(n // tm,),
        in_specs=[pl.BlockSpec((tm, d), lambda i: (i, 0)),
                  pl.BlockSpec((d, LANES), lambda i: (0, 0)),
                  pl.BlockSpec((1, LANES), lambda i: (0, 0))],
        out_specs=[out, out],
        out_shape=[jax.ShapeDtypeStruct((n, LANES), jnp.int32), jax.ShapeDtypeStruct((n, LANES), F32)],
        compiler_params=_params(("parallel",), 32),
        name="moe_router",
    )(x, w_r, b_r)


def _expert_kernel(blk_exp_ref, xs_ref, wu_ref, bu_ref, wd_ref, bd_ref, y_ref):
    del blk_exp_ref
    h = jnp.dot(xs_ref[...], wu_ref[...], preferred_element_type=F32) + bu_ref[...]
    glu = jnp.minimum(h[:, :D_FF], SWIGLU_LIMIT)
    lin = jnp.clip(h[:, D_FF:], -SWIGLU_LIMIT, SWIGLU_LIMIT)
    act = glu * jax.nn.sigmoid(SWIGLU_ALPHA * glu) * (lin + 1.0)
    y_ref[...] = jnp.dot(act.astype(BF16), wd_ref[...], preferred_element_type=F32) + bd_ref[...]


def _expert_ffn(xs, blk_exp, w_up, b_up, w_down, b_down):
    p, d = xs.shape
    grid_spec = pltpu.PrefetchScalarGridSpec(
        num_scalar_prefetch=1,
        grid=(p // MOE_ROWS,),
        in_specs=[pl.BlockSpec((MOE_ROWS, d), lambda i, e: (i, 0)),
                  pl.BlockSpec((None, d, 2 * D_FF), lambda i, e: (e[i], 0, 0)),
                  pl.BlockSpec((None, 1, 2 * D_FF), lambda i, e: (e[i], 0, 0)),
                  pl.BlockSpec((None, D_FF, d), lambda i, e: (e[i], 0, 0)),
                  pl.BlockSpec((None, 1, d), lambda i, e: (e[i], 0, 0))],
        out_specs=pl.BlockSpec((MOE_ROWS, d), lambda i, e: (i, 0)),
    )
    return pl.pallas_call(
        _expert_kernel,
        grid_spec=grid_spec,
        out_shape=jax.ShapeDtypeStruct((p, d), F32),
        compiler_params=_params(("arbitrary",), 48),
        name="expert_ffn",
    )(blk_exp, xs, w_up, b_up, w_down, b_down)


def _moe(x, x_bf, w_r, b_r, w_up, b_up, w_down, b_down):
    n, d = x.shape
    nk = n * TOP_K
    idx, gate = _router(x, w_r, b_r)
    e_flat = idx[:, :TOP_K].reshape(nk)
    gate = gate[:, :TOP_K]
    order = jnp.argsort(e_flat)
    counts = jnp.sum(jax.nn.one_hot(e_flat, N_EXPERTS, dtype=jnp.int32), axis=0)
    start = jnp.cumsum(counts) - counts
    padded = (counts + MOE_ROWS - 1) // MOE_ROWS * MOE_ROWS
    pend = jnp.cumsum(padded)
    pstart = pend - padded
    e_sorted = e_flat[order]
    slot_sorted = pstart[e_sorted] + (jnp.arange(nk, dtype=jnp.int32) - start[e_sorted])
    n_blocks = -(-(nk + N_EXPERTS * (MOE_ROWS - 1)) // MOE_ROWS)
    p = n_blocks * MOE_ROWS
    slot_tok = jnp.zeros((p,), jnp.int32).at[slot_sorted].set((order // TOP_K).astype(jnp.int32))
    slot_of = jnp.zeros((nk,), jnp.int32).at[order].set(slot_sorted.astype(jnp.int32))
    blk_exp = jnp.minimum(jnp.searchsorted(pend, jnp.arange(n_blocks, dtype=jnp.int32) * MOE_ROWS, side='right'),
                          N_EXPERTS - 1).astype(jnp.int32)
    xs = x_bf[slot_tok]
    y = _expert_ffn(xs, blk_exp, w_up, b_up, w_down, b_down)
    yk = y[slot_of].reshape(n, TOP_K, d)
    return jnp.sum(yk * gate[:, :, None], axis=1)


def kernel(x, rel_bias, w_in, b_forget, b_gate, w_br_sb, w_br_dil, w_br_moba, w_br_fox, w_out,
           ln1_g, ln1_b, w_router, b_router, w_up, b_up, w_down, b_down, ln2_g, ln2_b):
    batch, seq, d = x.shape
    n = batch * seq
    depth = w_in.shape[0]

    q_scale = jnp.concatenate([jnp.full((W_MIX,), HEAD_DIM ** -0.5, F32), jnp.ones((2 * W_MIX,), F32)])
    w_qkv = (w_in[:, :, :C_QKV] * q_scale).astype(BF16)
    w_f = jnp.pad(w_in[:, :, C_QKV:C_QKV + H_FOX], ((0, 0), (0, 0), (0, LANES - H_FOX)))
    b_f = jnp.pad(b_forget, ((0, 0), (0, LANES - H_FOX))).reshape(depth, 1, LANES)
    w_gate = w_in[:, :, C_QKV + H_FOX:].astype(BF16)
    b_gate3 = b_gate.reshape(depth, N_BRANCH, 1, d)
    w_branch = [w.astype(BF16) for w in (w_br_sb, w_br_dil, w_br_moba, w_br_fox)]
    w_out_bf = w_out.astype(BF16)
    w_r = jnp.pad(w_router, ((0, 0), (0, 0), (0, LANES - N_EXPERTS)))
    b_r = jnp.pad(b_router, ((0, 0), (0, LANES - N_EXPERTS))).reshape(depth, 1, LANES)
    w_up_bf = w_up.astype(BF16)
    w_down_bf = w_down.astype(BF16)
    b_up3 = b_up.reshape(depth, N_EXPERTS, 1, 2 * D_FF)
    b_down3 = b_down.reshape(depth, N_EXPERTS, 1, d)
    moba_bias = _moba_bias_tiles(rel_bias)
    dil_bias = _dilated_bias_tiles(rel_bias)

    xf = x.reshape(n, d)
    xb = xf.astype(BF16)
    for l in range(depth):
        qkv = _matmul(xb, w_qkv[l], 1024, 768, BF16)
        c = _decay(xf, w_f[l], b_f[l], batch)
        c_t = c[:, :H_FOX].reshape(batch, seq, H_FOX // 2, 2).transpose(0, 2, 3, 1)
        o_sb = _sb_attention(qkv, batch)
        groups = [_dilated_group(qkv, dil_bias, batch, g) for g in range(len(DIL_GROUPS))]
        o_dil = _dilated_combine([o for o, _ in groups], [s for _, s in groups])
        o_moba = _moba_attention(qkv, moba_bias, batch)
        o_fox = _fox_attention(qkv, c, c_t, batch)
        merged = _gated_merge(xb, (o_sb, o_dil, o_moba, o_fox), w_gate[l], b_gate3[l],
                              [w[l] for w in w_branch])
        xf, xb = _out_proj_ln(merged, w_out_bf[l], xf, ln1_g[l].reshape(1, d), ln1_b[l].reshape(1, d))
        y = _moe(xf, xb, w_r[l], b_r[l], w_up_bf[l], b_up3[l], w_down_bf[l], b_down3[l])
        xf, xb = _add_ln(y, xf, ln2_g[l].reshape(1, d), ln2_b[l].reshape(1, d))
    return xf.reshape(batch, seq, d)
```

```python
import functools
import math

import numpy as np
import jax
import jax.numpy as jnp
from jax import lax
from jax.experimental import pallas as pl
from jax.experimental.pallas import tpu as pltpu

D_MODEL = 2048
DEPTH = 4
HEAD_DIM = 64
H_SB = 8
H_DIL_PER_GROUP = 4
DIL_GROUPS = ((128, 1), (512, 4), (2048, 16))
H_DIL = H_DIL_PER_GROUP * len(DIL_GROUPS)
H_MOBA = 8
H_FOX = 8
H_ALL = H_SB + H_DIL + H_MOBA + H_FOX
SB_OFF = 0
DIL_OFF = SB_OFF + H_SB
MOBA_OFF = DIL_OFF + H_DIL
FOX_OFF = MOBA_OFF + H_MOBA
W_MIX = H_ALL * HEAD_DIM
N_BRANCH = 4
C_QKV = 3 * W_MIX
MOBA_BLOCK = 256
MOBA_TOPK = 3
N_BUCKETS = 32
MAX_DIST = 2048
N_EXPERTS = 32
TOP_K = 4
D_FF = 768
SWIGLU_ALPHA = 1.702
SWIGLU_LIMIT = 7.0
LN_EPS = 1e-5
DN_ALPHA = (2 * DEPTH) ** 0.25

F32 = jnp.float32
BF16 = jnp.bfloat16
LANES = 128
NEG = -1e30
HIGHEST = lax.Precision.HIGHEST
MIB = 1024 * 1024
NT = (((1,), (1,)), ((), ()))
SIGN_BIT = np.uint32(0x80000000)

MAIN_SB = 0
MAIN_DIL = MAIN_SB + H_SB
MAIN_MOBA = MAIN_DIL + H_DIL_PER_GROUP
MAIN_FOX = MAIN_MOBA + H_MOBA
MAIN_W = (MAIN_FOX + H_FOX) * HEAD_DIM
DIL_GROUP_W = H_DIL_PER_GROUP * HEAD_DIM
DIL_W = 128
MOE_ROWS = 256
SB_T = 1024
SB_SEG = 256
FOX_TQ = 1024
FOX_TK = 1024
MOBA_T = 1024
MOBA_NT = 8
AUG_PARTS = 3


def _params(sem, vmem_mib):
    return pltpu.CompilerParams(dimension_semantics=sem, vmem_limit_bytes=vmem_mib * MIB)


def _t5_bucket(n):
    exact = N_BUCKETS // 2
    nf = jnp.maximum(n, 1).astype(F32)
    large = exact + (jnp.log(nf / exact) / math.log(MAX_DIST / exact) * (N_BUCKETS - exact)).astype(jnp.int32)
    large = jnp.minimum(large, N_BUCKETS - 1)
    return jnp.where(n < exact, n, large)


def _bias_lookup(table, bucket):
    onehot = (bucket[..., None] == jnp.arange(N_BUCKETS)).astype(F32)
    return jnp.einsum('...k,kh->...h', onehot, table.astype(F32), precision=HIGHEST)


def _head_pair(q):
    lane = lax.broadcasted_iota(jnp.int32, q.shape, 1)
    zero = jnp.zeros_like(q)
    return [jnp.where(lane < HEAD_DIM, q, zero), jnp.where(lane >= HEAD_DIM, q, zero)], lane


def _attention_specs(seq, nq, tq, q_col, k_col, v_col):
    return [pl.BlockSpec((tq, LANES), lambda b, p, i: (b * nq + i, q_col + p)),
            pl.BlockSpec((seq, LANES), lambda b, p, i: (b, k_col + p)),
            pl.BlockSpec((seq, LANES), lambda b, p, i: (b, v_col + p))]


def _head_cols(off, width):
    q_col = off * HEAD_DIM // LANES
    return q_col, q_col + width // LANES, q_col + 2 * width // LANES


def _matmul_kernel(a_ref, w_ref, o_ref):
    o_ref[...] = jnp.dot(a_ref[...], w_ref[...], preferred_element_type=F32).astype(o_ref.dtype)


def _matmul(a, w, tm, tn, out_dtype, name):
    m, k = a.shape
    n = w.shape[1]
    return pl.pallas_call(
        _matmul_kernel,
        grid=(n // tn, m // tm),
        in_specs=[pl.BlockSpec((tm, k), lambda j, i: (i, 0)),
                  pl.BlockSpec((k, tn), lambda j, i: (0, j))],
        out_specs=pl.BlockSpec((tm, tn), lambda j, i: (i, j)),
        out_shape=jax.ShapeDtypeStruct((m, n), out_dtype),
        compiler_params=_params(("parallel", "parallel"), 40),
        name=name,
    )(a, w)


def _aug_constants():
    pairs = H_FOX // 2
    place_k = np.zeros((AUG_PARTS, LANES, pairs * LANES), np.float32)
    place_q = np.zeros((AUG_PARTS, LANES, pairs * LANES), np.float32)
    ones_k = np.zeros((1, pairs * LANES), np.float32)
    ones_q = np.zeros((1, pairs * LANES), np.float32)
    for g in range(H_FOX):
        p, h = divmod(g, 2)
        base = p * LANES + HEAD_DIM * (1 - h)
        for j in range(AUG_PARTS):
            place_k[j, g, base + j] = 1.0
            place_q[j, g, base + AUG_PARTS + j] = 1.0
            ones_k[0, base + AUG_PARTS + j] = 1.0
            ones_q[0, base + j] = 1.0
    return place_k, place_q, ones_k, ones_q


def _split_parts(x):
    parts = []
    for _ in range(AUG_PARTS):
        piece = x.astype(BF16)
        parts.append(piece)
        x = x - piece.astype(F32)
    return parts


def _decay_kernel(x_ref, wf_ref, bf_ref, pk_ref, pq_ref, ok_ref, oq_ref, augq_ref, augk_ref, carry_ref):
    @pl.when(pl.program_id(1) == 0)
    def _():
        carry_ref[...] = jnp.zeros_like(carry_ref)

    tm = x_ref.shape[0]
    f = jnp.dot(x_ref[...], wf_ref[...], preferred_element_type=F32, precision=HIGHEST) + bf_ref[...]
    logf = jnp.minimum(f, 0.0) - jnp.log1p(jnp.exp(-jnp.abs(f)))
    row = lax.broadcasted_iota(jnp.int32, (tm, tm), 0)
    col = lax.broadcasted_iota(jnp.int32, (tm, tm), 1)
    tri = jnp.where(col <= row, 1.0, 0.0).astype(F32)
    c = jnp.dot(tri, logf, preferred_element_type=F32, precision=HIGHEST) + carry_ref[...]
    carry_ref[...] = c[tm - 1:tm, :]
    aug_q = oq_ref[...]
    aug_k = ok_ref[...]
    for j, piece in enumerate(_split_parts(c)):
        aug_q = aug_q + jnp.dot(piece, pq_ref[j], preferred_element_type=F32)
        aug_k = aug_k - jnp.dot(piece, pk_ref[j], preferred_element_type=F32)
    augq_ref[...] = aug_q.astype(BF16)
    augk_ref[...] = aug_k.astype(BF16)


def _decay(x, w_f, b_f, batch, tm=256):
    n, d = x.shape
    nb = n // batch // tm
    width = H_FOX // 2 * LANES
    place_k, place_q, ones_k, ones_q = _aug_constants()
    const = lambda a: pl.BlockSpec(a.shape, lambda b, i: (0,) * a.ndim)
    out = pl.BlockSpec((tm, width), lambda b, i: (b * nb + i, 0))
    return pl.pallas_call(
        _decay_kernel,
        grid=(batch, nb),
        in_specs=[pl.BlockSpec((tm, d), lambda b, i: (b * nb + i, 0)),
                  pl.BlockSpec((d, LANES), lambda b, i: (0, 0)),
                  pl.BlockSpec((1, LANES), lambda b, i: (0, 0)),
                  const(place_k), const(place_q), const(ones_k), const(ones_q)],
        out_specs=[out, out],
        out_shape=[jax.ShapeDtypeStruct((n, width), BF16)] * 2,
        scratch_shapes=[pltpu.VMEM((1, LANES), F32)],
        compiler_params=_params(("arbitrary", "arbitrary"), 24),
        name="fox_decay",
    )(x, w_f, b_f, jnp.asarray(place_k, BF16), jnp.asarray(place_q, BF16), jnp.asarray(ones_k), jnp.asarray(ones_q))


def _sb_kernel(q_ref, k_ref, v_ref, o_ref):
    t, seg = SB_T, SB_SEG
    nseg = t // seg
    i = pl.program_id(2)
    qh, lane = _head_pair(q_ref[...])
    qneg = [-x for x in qh]
    r = lax.broadcasted_iota(jnp.int32, (seg, seg), 0)
    c = lax.broadcasted_iota(jnp.int32, (seg, seg), 1)
    later = jnp.where(r > c, 1.0, 0.0).astype(BF16)
    row = lax.broadcasted_iota(jnp.int32, (t, t), 0)
    col = lax.broadcasted_iota(jnp.int32, (t, t), 1)
    past = col < row

    def step(j, carry, diagonal):
        start = pl.multiple_of(j * t, t)
        k = k_ref[pl.ds(start, t), :]
        v = v_ref[pl.ds(start, t), :]
        out = []
        for h in range(2):
            tail, acc = carry[h]
            nz = lax.dot_general(qneg[h], k, NT, preferred_element_type=F32)
            minus_abs = lax.bitcast_convert_type(lax.bitcast_convert_type(nz, jnp.uint32) | SIGN_BIT, F32)
            log_1m = jnp.minimum(nz, 0.0) - jnp.log(1.0 + jnp.exp(minus_abs))
            base = log_1m - nz
            if diagonal:
                log_1m = jnp.where(past, log_1m, 0.0)
            sums = [jnp.dot(log_1m[:, b * seg:(b + 1) * seg].astype(BF16), later, preferred_element_type=F32)
                    for b in range(nseg)]
            for b in reversed(range(nseg)):
                sl = slice(b * seg, (b + 1) * seg)
                w = jnp.exp(base[:, sl] + (sums[b] + tail))
                if diagonal:
                    w = jnp.where(past[:, sl], w, 0.0)
                acc = acc + jnp.dot(w.astype(BF16), v[sl], preferred_element_type=F32)
                tail = tail + (sums[b][:, 0:1] + log_1m[:, b * seg:b * seg + 1])
            out.append((tail, acc))
        return tuple(out)

    init = tuple((jnp.zeros((t, 1), F32), jnp.zeros((t, LANES), F32)) for _ in range(2))
    carry = step(i, init, True)
    carry = lax.fori_loop(0, i, lambda jj, c: step(i - 1 - jj, c, False), carry)
    o_ref[...] = jnp.where(lane < HEAD_DIM, carry[0][1], carry[1][1]).astype(o_ref.dtype)


def _sb_attention(qkv, batch):
    n = qkv.shape[0]
    seq = n // batch
    nq = seq // SB_T
    pairs = H_SB // 2
    return pl.pallas_call(
        _sb_kernel,
        grid=(batch, pairs, nq),
        in_specs=_attention_specs(seq, nq, SB_T, *_head_cols(MAIN_SB, MAIN_W)),
        out_specs=pl.BlockSpec((SB_T, LANES), lambda b, p, i: (b * nq + i, p)),
        out_shape=jax.ShapeDtypeStruct((n, pairs * LANES), BF16),
        compiler_params=_params(("parallel", "parallel", "arbitrary"), 56),
        name="sb_attention",
    )(qkv, qkv, qkv)


def _fox_kernel(q_ref, k_ref, v_ref, aq_ref, ak_ref, o_ref):
    tq, tk = FOX_TQ, FOX_TK
    i = pl.program_id(2)
    q = q_ref[...]
    lane = lax.broadcasted_iota(jnp.int32, q.shape, 1)
    own = [lane < HEAD_DIM, lane >= HEAD_DIM]
    qh = [jnp.where(own[h], q, aq_ref[...]) for h in range(2)]
    klane = lax.broadcasted_iota(jnp.int32, (tk, LANES), 1)
    kown = [klane < HEAD_DIM, klane >= HEAD_DIM]
    row = lax.broadcasted_iota(jnp.int32, (tq, tk), 0)
    col = lax.broadcasted_iota(jnp.int32, (tq, tk), 1)

    def step(j, carry, diagonal):
        start = pl.multiple_of(j * tk, tk)
        k = k_ref[pl.ds(start, tk), :]
        v = v_ref[pl.ds(start, tk), :]
        ak = ak_ref[pl.ds(start, tk), :]
        out = []
        for h in range(2):
            s = lax.dot_general(qh[h], jnp.where(kown[h], k, ak), NT, preferred_element_type=F32)
            if diagonal:
                s = jnp.where(col + j * tk <= row + i * tq, s, NEG)
            m, acc = carry[h]
            m_new = jnp.maximum(m, jnp.max(s, axis=1, keepdims=True))
            p = jnp.exp(s - m_new).astype(BF16)
            vh = jnp.where(kown[h], v, jnp.ones_like(v))
            acc = jnp.exp(m - m_new) * acc + jnp.dot(p, vh, preferred_element_type=F32)
            out.append((m_new, acc))
        return tuple(out)

    carry = tuple((jnp.full((tq, 1), NEG, F32), jnp.zeros((tq, LANES), F32)) for _ in range(2))
    n_full = (i * tq) // tk
    carry = lax.fori_loop(0, n_full, lambda j, c: step(j, c, False), carry)
    for d in range(max(tq // tk, 1)):
        carry = step(n_full + d, carry, True)
    o = []
    for h in range(2):
        acc = carry[h][1]
        o.append(acc / acc[:, HEAD_DIM * (1 - h):HEAD_DIM * (1 - h) + 1])
    o_ref[...] = jnp.where(own[0], o[0], o[1]).astype(o_ref.dtype)


def _fox_attention(qkv, aug_q, aug_k, batch):
    n = qkv.shape[0]
    seq = n // batch
    nq = seq // FOX_TQ
    pairs = H_FOX // 2
    specs = _attention_specs(seq, nq, FOX_TQ, *_head_cols(MAIN_FOX, MAIN_W))
    specs += [pl.BlockSpec((FOX_TQ, LANES), lambda b, p, i: (b * nq + i, p)),
              pl.BlockSpec((seq, LANES), lambda b, p, i: (b, p))]
    return pl.pallas_call(
        _fox_kernel,
        grid=(batch, pairs, nq),
        in_specs=specs,
        out_specs=pl.BlockSpec((FOX_TQ, LANES), lambda b, p, i: (b * nq + i, p)),
        out_shape=jax.ShapeDtypeStruct((n, pairs * LANES), BF16),
        compiler_params=_params(("parallel", "parallel", "arbitrary"), 48),
        name="fox_attention",
    )(qkv, qkv, qkv, aug_q, aug_k)


def _moba_kernel(q_ref, k_ref, v_ref, onehot_ref, bias_ref, o_ref, kmean_ref):
    t, mb = MOBA_T, MOBA_BLOCK
    nb = t // mb
    seq = k_ref.shape[0]
    i = pl.program_id(2)

    @pl.when(i == 0)
    def _():
        blk = lax.broadcasted_iota(jnp.int32, (LANES, seq), 0)
        pos = lax.broadcasted_iota(jnp.int32, (LANES, seq), 1)
        member = jnp.where((pos >= blk * mb) & (pos < (blk + 1) * mb), 1.0, 0.0).astype(BF16)
        mean = jnp.dot(member, k_ref[...], preferred_element_type=F32) * (1.0 / mb)
        hi = mean.astype(BF16)
        kmean_ref[0] = hi
        kmean_ref[1] = (mean - hi.astype(F32)).astype(BF16)

    q = q_ref[...]
    lane = lax.broadcasted_iota(jnp.int32, (t, LANES), 1)
    own_half = [lane < HEAD_DIM, lane >= HEAD_DIM]
    zero = jnp.zeros_like(q)
    blk_id = lane.astype(F32)
    own_blk = ((i * t + lax.broadcasted_iota(jnp.int32, (t, LANES), 0)) // mb).astype(F32)
    fully_past = blk_id < own_blk

    qh = []
    for h in range(2):
        q_only = jnp.where(own_half[h], q, zero)
        gate = (lax.dot_general(q_only, kmean_ref[0], NT, preferred_element_type=F32)
                + lax.dot_general(q_only, kmean_ref[1], NT, preferred_element_type=F32))
        gate = jnp.where(fully_past, gate, -jnp.inf)
        picked = jnp.zeros((t, LANES), jnp.bool_)
        for _ in range(MOBA_TOPK):
            best = jnp.max(gate, axis=1, keepdims=True)
            first = jnp.min(jnp.where(gate == best, blk_id, float(LANES)), axis=1, keepdims=True)
            pick = blk_id == first
            picked = picked | pick
            gate = jnp.where(pick, -jnp.inf, gate)
        allowed = (picked & fully_past) | (blk_id == own_blk)
        penalty = jnp.where(allowed, 0.0, NEG)
        if h == 0:
            penalty = pltpu.roll(penalty, HEAD_DIM, axis=1)
        qh.append(jnp.where(own_half[h], q, penalty.astype(BF16)))

    def step(j, carry):
        start = pl.multiple_of(j * t, t)
        k = k_ref[pl.ds(start, t), :]
        v = v_ref[pl.ds(start, t), :]
        onehot = onehot_ref[pl.ds(start, t), :]
        out = []
        for h in range(2):
            s = lax.dot_general(qh[h], jnp.where(own_half[h], k, onehot), NT, preferred_element_type=F32)
            rows = []
            for a in range(nb):
                tiles = [bias_ref[jnp.clip((i - j) * nb + a - b, 0, MOBA_NT), h] for b in range(nb)]
                rows.append(jnp.concatenate(tiles, axis=1))
            s = s + jnp.concatenate(rows, axis=0)
            m, acc = carry[h]
            m_new = jnp.maximum(m, jnp.max(s, axis=1, keepdims=True))
            p = jnp.exp(s - m_new).astype(BF16)
            vh = jnp.where(own_half[h], v, jnp.ones_like(v))
            acc = jnp.exp(m - m_new) * acc + jnp.dot(p, vh, preferred_element_type=F32)
            out.append((m_new, acc))
        return tuple(out)

    carry = tuple((jnp.full((t, 1), NEG, F32), jnp.zeros((t, LANES), F32)) for _ in range(2))
    carry = lax.fori_loop(0, i + 1, step, carry)
    o = []
    for h in range(2):
        acc = carry[h][1]
        o.append(acc / acc[:, HEAD_DIM * (1 - h):HEAD_DIM * (1 - h) + 1])
    o_ref[...] = jnp.where(own_half[0], o[0], o[1]).astype(o_ref.dtype)


def _moba_attention(qkv, bias_tiles, batch):
    n = qkv.shape[0]
    seq = n // batch
    assert seq % MOBA_T == 0 and seq // MOBA_BLOCK <= HEAD_DIM
    nq = seq // MOBA_T
    pairs = H_MOBA // 2
    blk = np.arange(seq)[:, None] // MOBA_BLOCK
    onehot = jnp.asarray(blk == (np.arange(LANES)[None, :] % HEAD_DIM), BF16)
    specs = _attention_specs(seq, nq, MOBA_T, *_head_cols(MAIN_MOBA, MAIN_W))
    specs.append(pl.BlockSpec((seq, LANES), lambda b, p, i: (0, 0)))
    specs.append(pl.BlockSpec((None, MOBA_NT + 1, 2, MOBA_BLOCK, MOBA_BLOCK), lambda b, p, i: (p, 0, 0, 0, 0)))
    return pl.pallas_call(
        _moba_kernel,
        grid=(batch, pairs, nq),
        in_specs=specs,
        out_specs=pl.BlockSpec((MOBA_T, LANES), lambda b, p, i: (b * nq + i, p)),
        out_shape=jax.ShapeDtypeStruct((n, pairs * LANES), BF16),
        scratch_shapes=[pltpu.VMEM((2, LANES, LANES), BF16)],
        compiler_params=_params(("arbitrary", "arbitrary", "arbitrary"), 56),
        name="moba_attention",
    )(qkv, qkv, qkv, onehot, bias_tiles)


def _moba_bias_tiles(rel_bias):
    r = jnp.arange(MOBA_BLOCK)
    dist = (jnp.arange(MOBA_NT + 1)[:, None, None] * MOBA_BLOCK + r[None, :, None] - r[None, None, :])
    tiles = _bias_lookup(rel_bias[:, H_DIL:], _t5_bucket(jnp.maximum(dist, 0)))
    tiles = jnp.where((dist >= 0)[..., None], tiles, NEG)
    tiles = tiles.transpose(3, 0, 1, 2).reshape(H_MOBA // 2, 2, MOBA_NT + 1, MOBA_BLOCK, MOBA_BLOCK)
    return tiles.transpose(0, 2, 1, 3, 4).astype(F32)


def _check_moba_saturation():
    n = np.float32(MOBA_NT * MOBA_BLOCK - (MOBA_BLOCK - 1))
    exact = N_BUCKETS // 2
    large = exact + int(np.log(n / np.float32(exact)) / math.log(MAX_DIST / exact) * (N_BUCKETS - exact))
    assert large >= N_BUCKETS - 1, "MOBA_NT too small for the bias bucket table"


_check_moba_saturation()


def _dilated_kernel(q_ref, k_ref, v_ref, bias_ref, o_ref, lse_ref):
    w = DIL_W
    nb = q_ref.shape[0] // w
    row = lax.broadcasted_iota(jnp.int32, (w, w), 0)
    col = lax.broadcasted_iota(jnp.int32, (w, w), 1)
    in_cur = col <= row
    in_prev = col >= row

    def body(n, _):
        cur = pl.multiple_of(n * w, w)
        prev = pl.multiple_of(jnp.maximum(n - 1, 0) * w, w)
        qh, lane = _head_pair(q_ref[pl.ds(cur, w), :])
        k_cur, v_cur = k_ref[pl.ds(cur, w), :], v_ref[pl.ds(cur, w), :]
        k_prev, v_prev = k_ref[pl.ds(prev, w), :], v_ref[pl.ds(prev, w), :]
        has_prev = n > 0
        outs, lses = [], []
        for h in range(2):
            z_prev = lax.dot_general(qh[h], k_prev, NT, preferred_element_type=F32) + bias_ref[h, :, 0:w]
            z_cur = lax.dot_general(qh[h], k_cur, NT, preferred_element_type=F32) + bias_ref[h, :, w:2 * w]
            z_prev = jnp.where(in_prev & has_prev, z_prev, NEG)
            z_cur = jnp.where(in_cur, z_cur, NEG)
            m = jnp.maximum(jnp.max(z_prev, axis=1, keepdims=True), jnp.max(z_cur, axis=1, keepdims=True))
            p_prev = jnp.exp(z_prev - m)
            p_cur = jnp.exp(z_cur - m)
            l = jnp.sum(p_prev, axis=1, keepdims=True) + jnp.sum(p_cur, axis=1, keepdims=True)
            o = (jnp.dot(p_prev.astype(BF16), v_prev, preferred_element_type=F32)
                 + jnp.dot(p_cur.astype(BF16), v_cur, preferred_element_type=F32))
            outs.append(o / l)
            lses.append(m + jnp.log(l))
        o_ref[pl.ds(cur, w), :] = jnp.where(lane < HEAD_DIM, outs[0], outs[1])
        lse_ref[pl.ds(cur, w), :] = jnp.where(lane < HEAD_DIM, lses[0], lses[1])
        return 0

    lax.fori_loop(0, nb, body, 0)


def _dilated_group(qkv, head_off, width, bias, batch, group):
    window, dil = DIL_GROUPS[group]
    assert window // dil == DIL_W
    n = qkv.shape[0]
    sub = n // batch // dil
    assert sub % DIL_W == 0
    pairs = H_DIL_PER_GROUP // 2
    q_col, k_col, v_col = _head_cols(head_off, width)

    def spec(col0):
        return pl.BlockSpec((sub, LANES), lambda b, p, r: (b * dil + r, col0 + p))

    out_spec = pl.BlockSpec((sub, LANES), lambda b, p, r: (b * dil + r, p))
    out_shape = jax.ShapeDtypeStruct((n, pairs * LANES), F32)
    return pl.pallas_call(
        _dilated_kernel,
        grid=(batch, pairs, dil),
        in_specs=[spec(q_col), spec(k_col), spec(v_col),
                  pl.BlockSpec((None, 2, DIL_W, 2 * DIL_W), lambda b, p, r: (group * pairs + p, 0, 0, 0))],
        out_specs=[out_spec, out_spec],
        out_shape=[out_shape, out_shape],
        compiler_params=_params(("parallel", "parallel", "parallel"), 48),
        name=f"dilated_attention_g{group}",
    )(qkv, qkv, qkv, bias)


def _by_residue(a, batch, dil):
    n, c = a.shape
    return a.reshape(batch, n // batch // dil, dil, c).transpose(0, 2, 1, 3).reshape(n, c)


def _by_position(a, batch, dil):
    n, c = a.shape
    return a.reshape(batch, dil, n // batch // dil, c).transpose(0, 2, 1, 3).reshape(n, c)


def _dilated_bias_tiles(rel_bias):
    steps = jnp.arange(DIL_W)[:, None] + DIL_W - jnp.arange(2 * DIL_W)[None, :]
    tiles = []
    for g, (_, dil) in enumerate(DIL_GROUPS):
        tab = rel_bias[:, g * H_DIL_PER_GROUP:(g + 1) * H_DIL_PER_GROUP]
        tiles.append(_bias_lookup(tab, _t5_bucket(jnp.maximum(steps, 0) * dil)).transpose(2, 0, 1))
    tiles = jnp.stack(tiles, 0).astype(F32)
    return tiles.reshape(len(DIL_GROUPS) * H_DIL_PER_GROUP // 2, 2, DIL_W, 2 * DIL_W)


def _dilated_combine_kernel(o0, o1, o2, l0, l1, l2, out_ref):
    lse = [l0[...], l1[...], l2[...]]
    m = jnp.maximum(jnp.maximum(lse[0], lse[1]), lse[2])
    e = [jnp.exp(x - m) for x in lse]
    den = e[0] + e[1] + e[2]
    out = (e[0] / den) * o0[...] + (e[1] / den) * o1[...] + (e[2] / den) * o2[...]
    out_ref[...] = out.astype(out_ref.dtype)


def _dilated_combine(os, lses, tm=1024):
    n, c = os[0].shape
    spec = pl.BlockSpec((tm, c), lambda i: (i, 0))
    return pl.pallas_call(
        _dilated_combine_kernel,
        grid=(n // tm,),
        in_specs=[spec] * 6,
        out_specs=spec,
        out_shape=jax.ShapeDtypeStruct((n, c), BF16),
        compiler_params=_params(("parallel",), 32),
        name="dilated_combine",
    )(*os, *lses)


def _merge_kernel(x_ref, o_sb, o_dil, o_moba, o_fox, wg0, wg1, wg2, wg3, bg_ref,
                  w_sb, w_dil, w_moba, w_fox, out_ref):
    x = x_ref[...]
    acc = None
    branches = ((o_sb, w_sb, wg0), (o_dil, w_dil, wg1), (o_moba, w_moba, wg2), (o_fox, w_fox, wg3))
    for b, (o_ref, w_ref, wg_ref) in enumerate(branches):
        gate = jax.nn.sigmoid(jnp.dot(x, wg_ref[...], preferred_element_type=F32) + bg_ref[b])
        term = gate * jnp.dot(o_ref[...], w_ref[...], preferred_element_type=F32)
        acc = term if acc is None else acc + term
    out_ref[...] = acc.astype(out_ref.dtype)


def _gated_merge(x_bf, outs, w_gate, b_gate, w_branch, tm=512, tn=512):
    n, d = x_bf.shape
    nj = d // tn
    row = lambda width: pl.BlockSpec((tm, width), lambda j, i: (i, 0))
    in_specs = [row(d)] + [row(o.shape[1]) for o in outs]
    in_specs += [pl.BlockSpec((d, tn), functools.partial(lambda j, i, b: (0, b * nj + j), b=b)) for b in range(N_BRANCH)]
    in_specs.append(pl.BlockSpec((N_BRANCH, 1, tn), lambda j, i: (0, 0, j)))
    in_specs += [pl.BlockSpec((w.shape[0], tn), lambda j, i: (0, j)) for w in w_branch]
    return pl.pallas_call(
        _merge_kernel,
        grid=(nj, n // tm),
        in_specs=in_specs,
        out_specs=pl.BlockSpec((tm, tn), lambda j, i: (i, j)),
        out_shape=jax.ShapeDtypeStruct((n, d), BF16),
        compiler_params=_params(("parallel", "parallel"), 48),
        name="gated_merge",
    )(x_bf, *outs, w_gate, w_gate, w_gate, w_gate, b_gate, *w_branch)


def _layer_norm_store(z, g_ref, b_ref, xo_ref, xb_ref):
    mu = jnp.mean(z, axis=1, keepdims=True)
    zc = z - mu
    var = jnp.mean(zc * zc, axis=1, keepdims=True)
    out = zc * lax.rsqrt(var + LN_EPS) * g_ref[...] + b_ref[...]
    xo_ref[...] = out
    xb_ref[...] = out.astype(BF16)


def _out_ln_kernel(m_ref, w_ref, x_ref, g_ref, b_ref, xo_ref, xb_ref):
    y = jnp.dot(m_ref[...], w_ref[...], preferred_element_type=F32)
    _layer_norm_store(DN_ALPHA * x_ref[...] + y, g_ref, b_ref, xo_ref, xb_ref)


def _add_ln_kernel(y_ref, x_ref, g_ref, b_ref, xo_ref, xb_ref):
    _layer_norm_store(DN_ALPHA * x_ref[...] + y_ref[...], g_ref, b_ref, xo_ref, xb_ref)


def _ln_call(kernel, lead_specs, lead_args, x, g, b, tm, name):
    n, d = x.shape
    row = pl.BlockSpec((tm, d), lambda i: (i, 0))
    vec = pl.BlockSpec((1, d), lambda i: (0, 0))
    return pl.pallas_call(
        kernel,
        grid=(n // tm,),
        in_specs=lead_specs + [row, vec, vec],
        out_specs=[row, row],
        out_shape=[jax.ShapeDtypeStruct((n, d), F32), jax.ShapeDtypeStruct((n, d), BF16)],
        compiler_params=_params(("parallel",), 48),
        name=name,
    )(*lead_args, x, g, b)


def _out_proj_ln(merged, w_out, x, g, b, tm=256):
    d = x.shape[1]
    specs = [pl.BlockSpec((tm, d), lambda i: (i, 0)), pl.BlockSpec((d, d), lambda i: (0, 0))]
    return _ln_call(_out_ln_kernel, specs, (merged, w_out), x, g, b, tm, "out_proj_ln")


def _add_ln(y, x, g, b, tm=256):
    d = x.shape[1]
    return _ln_call(_add_ln_kernel, [pl.BlockSpec((tm, d), lambda i: (i, 0))], (y,), x, g, b, tm, "moe_add_ln")


def _router_kernel(x_ref, w_ref, b_ref, idx_ref, gate_ref):
    logits = jnp.dot(x_ref[...], w_ref[...], preferred_element_type=F32, precision=HIGHEST) + b_ref[...]
    col = lax.broadcasted_iota(jnp.int32, logits.shape, 1)
    colf = col.astype(F32)
    logits = jnp.where(col < N_EXPERTS, logits, -jnp.inf)
    vals, idxs = [], []
    for _ in range(TOP_K):
        best = jnp.max(logits, axis=1, keepdims=True)
        first = jnp.min(jnp.where(logits == best, colf, float(LANES)), axis=1, keepdims=True)
        logits = jnp.where(colf == first, -jnp.inf, logits)
        vals.append(best)
        idxs.append(first)
    es = [jnp.exp(v - vals[0]) for v in vals]
    den = es[0] + es[1] + es[2] + es[3]
    idx_out = jnp.zeros(logits.shape, F32)
    gate_out = jnp.zeros(logits.shape, F32)
    for r in range(TOP_K):
        idx_out = jnp.where(col == r, idxs[r], idx_out)
        gate_out = jnp.where(col == r, es[r] / den, gate_out)
    idx_ref[...] = idx_out.astype(jnp.int32)
    gate_ref[...] = gate_out


def _router(x, w_r, b_r, tm=512):
    n, d = x.shape
    out = pl.BlockSpec((tm, LANES), lambda i: (i, 0))
    return pl.pallas_call(
        _router_kernel,
        grid=(n // tm,),
        in_specs=[pl.BlockSpec((tm, d), lambda i: (i, 0)),
                  pl.BlockSpec((d, LANES), lambda i: (0, 0)),
                  pl.BlockSpec((1, LANES), lambda i: (0, 0))],
        out_specs=[out, out],
        out_shape=[jax.ShapeDtypeStruct((n, LANES), jnp.int32), jax.ShapeDtypeStruct((n, LANES), F32)],
        compiler_params=_params(("parallel",), 32),
        name="moe_router",
    )(x, w_r, b_r)


def _expert_kernel(blk_exp_ref, xs_ref, wu_ref, bu_ref, wd_ref, bd_ref, y_ref):
    del blk_exp_ref
    h = jnp.dot(xs_ref[...], wu_ref[...], preferred_element_type=F32) + bu_ref[...]
    glu = jnp.minimum(h[:, :D_FF], SWIGLU_LIMIT)
    lin = jnp.clip(h[:, D_FF:], -SWIGLU_LIMIT, SWIGLU_LIMIT)
    act = glu * jax.nn.sigmoid(SWIGLU_ALPHA * glu) * (lin + 1.0)
    y_ref[...] = jnp.dot(act.astype(BF16), wd_ref[...], preferred_element_type=F32) + bd_ref[...]


def _expert_ffn(xs, blk_exp, w_up, b_up, w_down, b_down):
    p, d = xs.shape
    grid_spec = pltpu.PrefetchScalarGridSpec(
        num_scalar_prefetch=1,
        grid=(p // MOE_ROWS,),
        in_specs=[pl.BlockSpec((MOE_ROWS, d), lambda i, e: (i, 0)),
                  pl.BlockSpec((None, d, 2 * D_FF), lambda i, e: (e[i], 0, 0)),
                  pl.BlockSpec((None, 1, 2 * D_FF), lambda i, e: (e[i], 0, 0)),
                  pl.BlockSpec((None, D_FF, d), lambda i, e: (e[i], 0, 0)),
                  pl.BlockSpec((None, 1, d), lambda i, e: (e[i], 0, 0))],
        out_specs=pl.BlockSpec((MOE_ROWS, d), lambda i, e: (i, 0)),
    )
    return pl.pallas_call(
        _expert_kernel,
        grid_spec=grid_spec,
        out_shape=jax.ShapeDtypeStruct((p, d), F32),
        compiler_params=_params(("arbitrary",), 48),
        name="expert_ffn",
    )(blk_exp, xs, w_up, b_up, w_down, b_down)


def _moe(x, x_bf, w_r, b_r, w_up, b_up, w_down, b_down):
    n, d = x.shape
    nk = n * TOP_K
    idx, gate = _router(x, w_r, b_r)
    e_flat = idx[:, :TOP_K].reshape(nk)
    gate = gate[:, :TOP_K]
    order = jnp.argsort(e_flat)
    counts = jnp.sum(jax.nn.one_hot(e_flat, N_EXPERTS, dtype=jnp.int32), axis=0)
    start = jnp.cumsum(counts) - counts
    padded = (counts + MOE_ROWS - 1) // MOE_ROWS * MOE_ROWS
    pend = jnp.cumsum(padded)
    pstart = pend - padded
    e_sorted = e_flat[order]
    slot_sorted = pstart[e_sorted] + (jnp.arange(nk, dtype=jnp.int32) - start[e_sorted])
    n_blocks = -(-(nk + N_EXPERTS * (MOE_ROWS - 1)) // MOE_ROWS)
    p = n_blocks * MOE_ROWS
    slot_tok = jnp.zeros((p,), jnp.int32).at[slot_sorted].set((order // TOP_K).astype(jnp.int32))
    slot_of = jnp.zeros((nk,), jnp.int32).at[order].set(slot_sorted.astype(jnp.int32))
    blk_exp = jnp.minimum(jnp.searchsorted(pend, jnp.arange(n_blocks, dtype=jnp.int32) * MOE_ROWS, side='right'),
                          N_EXPERTS - 1).astype(jnp.int32)
    xs = x_bf[slot_tok]
    y = _expert_ffn(xs, blk_exp, w_up, b_up, w_down, b_down)
    yk = y[slot_of].reshape(n, TOP_K, d)
    return jnp.sum(yk * gate[:, :, None], axis=1)


def kernel(x, rel_bias, w_in, b_forget, b_gate, w_br_sb, w_br_dil, w_br_moba, w_br_fox, w_out,
           ln1_g, ln1_b, w_router, b_router, w_up, b_up, w_down, b_down, ln2_g, ln2_b):
    batch, seq, d = x.shape
    n = batch * seq
    depth = w_in.shape[0]

    def qkv_weight(heads):
        cols = []
        for sec, scale in ((0, HEAD_DIM ** -0.5), (1, 1.0), (2, 1.0)):
            for lo, cnt in heads:
                a = sec * W_MIX + lo * HEAD_DIM
                cols.append(w_in[:, :, a:a + cnt * HEAD_DIM] * scale)
        return jnp.concatenate(cols, axis=2).astype(BF16)

    w_main = qkv_weight(((SB_OFF, H_SB), (DIL_OFF, H_DIL_PER_GROUP), (MOBA_OFF, H_MOBA), (FOX_OFF, H_FOX)))
    w_dil = [qkv_weight(((DIL_OFF + g * H_DIL_PER_GROUP, H_DIL_PER_GROUP),)) for g in range(1, len(DIL_GROUPS))]
    w_f = jnp.pad(w_in[:, :, C_QKV:C_QKV + H_FOX], ((0, 0), (0, 0), (0, LANES - H_FOX)))
    b_f = jnp.pad(b_forget, ((0, 0), (0, LANES - H_FOX))).reshape(depth, 1, LANES)
    w_gate = w_in[:, :, C_QKV + H_FOX:].astype(BF16)
    b_gate3 = b_gate.reshape(depth, N_BRANCH, 1, d)
    w_branch = [w.astype(BF16) for w in (w_br_sb, w_br_dil, w_br_moba, w_br_fox)]
    w_out_bf = w_out.astype(BF16)
    w_r = jnp.pad(w_router, ((0, 0), (0, 0), (0, LANES - N_EXPERTS)))
    b_r = jnp.pad(b_router, ((0, 0), (0, LANES - N_EXPERTS))).reshape(depth, 1, LANES)
    w_up_bf = w_up.astype(BF16)
    w_down_bf = w_down.astype(BF16)
    b_up3 = b_up.reshape(depth, N_EXPERTS, 1, 2 * D_FF)
    b_down3 = b_down.reshape(depth, N_EXPERTS, 1, d)
    moba_bias = _moba_bias_tiles(rel_bias)
    dil_bias = _dilated_bias_tiles(rel_bias)

    xf = x.reshape(n, d)
    xb = xf.astype(BF16)
    for l in range(depth):
        qkv = _matmul(xb, w_main[l], 1024, 768, BF16, "qkv_proj")
        aug_q, aug_k = _decay(xf, w_f[l], b_f[l], batch)
        o_sb = _sb_attention(qkv, batch)
        os, lses = [], []
        o, lse = _dilated_group(qkv, MAIN_DIL, MAIN_W, dil_bias, batch, 0)
        os.append(o)
        lses.append(lse)
        for g in range(1, len(DIL_GROUPS)):
            dil = DIL_GROUPS[g][1]
            qkv_g = _matmul(_by_residue(xb, batch, dil), w_dil[g - 1][l], 1024, 3 * DIL_GROUP_W, BF16,
                            f"qkv_proj_g{g}")
            o, lse = _dilated_group(qkv_g, 0, DIL_GROUP_W, dil_bias, batch, g)
            os.append(_by_position(o, batch, dil))
            lses.append(_by_position(lse, batch, dil))
        o_dil = _dilated_combine(os, lses)
        o_moba = _moba_attention(qkv, moba_bias, batch)
        o_fox = _fox_attention(qkv, aug_q, aug_k, batch)
        merged = _gated_merge(xb, (o_sb, o_dil, o_moba, o_fox), w_gate[l], b_gate3[l],
                              [w[l] for w in w_branch])
        xf, xb = _out_proj_ln(merged, w_out_bf[l], xf, ln1_g[l].reshape(1, d), ln1_b[l].reshape(1, d))
        y = _moe(xf, xb, w_r[l], b_r[l], w_up_bf[l], b_up3[l], w_down_bf[l], b_down3[l])
        xf, xb = _add_ln(y, xf, ln2_g[l].reshape(1, d), ln2_b[l].reshape(1, d))
    return xf.reshape(batch, seq, d)
```

```python
import functools
import math

import numpy as np
import jax
import jax.numpy as jnp
from jax import lax
from jax.experimental import pallas as pl
from jax.experimental.pallas import tpu as pltpu

D_MODEL = 2048
DEPTH = 4
HEAD_DIM = 64
H_SB = 8
H_DIL_PER_GROUP = 4
DIL_GROUPS = ((128, 1), (512, 4), (2048, 16))
H_DIL = H_DIL_PER_GROUP * len(DIL_GROUPS)
H_MOBA = 8
H_FOX = 8
H_ALL = H_SB + H_DIL + H_MOBA + H_FOX
SB_OFF = 0
DIL_OFF = SB_OFF + H_SB
MOBA_OFF = DIL_OFF + H_DIL
FOX_OFF = MOBA_OFF + H_MOBA
W_MIX = H_ALL * HEAD_DIM
N_BRANCH = 4
C_QKV = 3 * W_MIX
MOBA_BLOCK = 256
MOBA_TOPK = 3
N_BUCKETS = 32
MAX_DIST = 2048
N_EXPERTS = 32
TOP_K = 4
D_FF = 768
SWIGLU_ALPHA = 1.702
SWIGLU_LIMIT = 7.0
LN_EPS = 1e-5
DN_ALPHA = (2 * DEPTH) ** 0.25

F32 = jnp.float32
BF16 = jnp.bfloat16
LANES = 128
NEG = -1e30
HIGHEST = lax.Precision.HIGHEST
MIB = 1024 * 1024
NT = (((1,), (1,)), ((), ()))
SIGN_BIT = np.uint32(0x80000000)

MAIN_SB = 0
MAIN_DIL = MAIN_SB + H_SB
MAIN_MOBA = MAIN_DIL + H_DIL_PER_GROUP
MAIN_FOX = MAIN_MOBA + H_MOBA
MAIN_W = (MAIN_FOX + H_FOX) * HEAD_DIM
DIL_GROUP_W = H_DIL_PER_GROUP * HEAD_DIM
DIL_W = 128
MOE_ROWS = 256
SB_T = 1024
SB_SEG = 256
FOX_TQ = 1024
FOX_TK = 1024
MOBA_T = 1024
MOBA_NT = 8
AUG_PARTS = 3


def _params(sem, vmem_mib):
    return pltpu.CompilerParams(dimension_semantics=sem, vmem_limit_bytes=vmem_mib * MIB)


def _t5_bucket(n):
    exact = N_BUCKETS // 2
    nf = jnp.maximum(n, 1).astype(F32)
    large = exact + (jnp.log(nf / exact) / math.log(MAX_DIST / exact) * (N_BUCKETS - exact)).astype(jnp.int32)
    large = jnp.minimum(large, N_BUCKETS - 1)
    return jnp.where(n < exact, n, large)


def _bias_lookup(table, bucket):
    onehot = (bucket[..., None] == jnp.arange(N_BUCKETS)).astype(F32)
    return jnp.einsum('...k,kh->...h', onehot, table.astype(F32), precision=HIGHEST)


def _head_pair(q):
    lane = lax.broadcasted_iota(jnp.int32, q.shape, 1)
    zero = jnp.zeros_like(q)
    return [jnp.where(lane < HEAD_DIM, q, zero), jnp.where(lane >= HEAD_DIM, q, zero)], lane


def _attention_specs(seq, nq, tq, q_col, k_col, v_col):
    return [pl.BlockSpec((tq, LANES), lambda b, p, i: (b * nq + i, q_col + p)),
            pl.BlockSpec((seq, LANES), lambda b, p, i: (b, k_col + p)),
            pl.BlockSpec((seq, LANES), lambda b, p, i: (b, v_col + p))]


def _head_cols(off, width):
    q_col = off * HEAD_DIM // LANES
    return q_col, q_col + width // LANES, q_col + 2 * width // LANES


def _matmul_kernel(a_ref, w_ref, o_ref):
    o_ref[...] = jnp.dot(a_ref[...], w_ref[...], preferred_element_type=F32).astype(o_ref.dtype)


def _matmul(a, w, tm, tn, out_dtype, name):
    m, k = a.shape
    n = w.shape[1]
    return pl.pallas_call(
        _matmul_kernel,
        grid=(n // tn, m // tm),
        in_specs=[pl.BlockSpec((tm, k), lambda j, i: (i, 0)),
                  pl.BlockSpec((k, tn), lambda j, i: (0, j))],
        out_specs=pl.BlockSpec((tm, tn), lambda j, i: (i, j)),
        out_shape=jax.ShapeDtypeStruct((m, n), out_dtype),
        compiler_params=_params(("parallel", "parallel"), 40),
        name=name,
    )(a, w)


def _aug_constants():
    pairs = H_FOX // 2
    place_k = np.zeros((AUG_PARTS, LANES, pairs * LANES), np.float32)
    place_q = np.zeros((AUG_PARTS, LANES, pairs * LANES), np.float32)
    ones_k = np.zeros((1, pairs * LANES), np.float32)
    ones_q = np.zeros((1, pairs * LANES), np.float32)
    for g in range(H_FOX):
        p, h = divmod(g, 2)
        base = p * LANES + HEAD_DIM * (1 - h)
        for j in range(AUG_PARTS):
            place_k[j, g, base + j] = 1.0
            place_q[j, g, base + AUG_PARTS + j] = 1.0
            ones_k[0, base + AUG_PARTS + j] = 1.0
            ones_q[0, base + j] = 1.0
    return place_k, place_q, ones_k, ones_q


def _split_parts(x):
    parts = []
    for _ in range(AUG_PARTS):
        piece = x.astype(BF16)
        parts.append(piece)
        x = x - piece.astype(F32)
    return parts


def _decay_kernel(x_ref, wf_ref, bf_ref, pk_ref, pq_ref, ok_ref, oq_ref, augq_ref, augk_ref, carry_ref):
    @pl.when(pl.program_id(1) == 0)
    def _():
        carry_ref[...] = jnp.zeros_like(carry_ref)

    tm = x_ref.shape[0]
    f = jnp.dot(x_ref[...], wf_ref[...], preferred_element_type=F32, precision=HIGHEST) + bf_ref[...]
    logf = jnp.minimum(f, 0.0) - jnp.log1p(jnp.exp(-jnp.abs(f)))
    row = lax.broadcasted_iota(jnp.int32, (tm, tm), 0)
    col = lax.broadcasted_iota(jnp.int32, (tm, tm), 1)
    tri = jnp.where(col <= row, 1.0, 0.0).astype(F32)
    c = jnp.dot(tri, logf, preferred_element_type=F32, precision=HIGHEST) + carry_ref[...]
    carry_ref[...] = c[tm - 1:tm, :]
    aug_q = oq_ref[...]
    aug_k = ok_ref[...]
    for j, piece in enumerate(_split_parts(c)):
        aug_q = aug_q + jnp.dot(piece, pq_ref[j], preferred_element_type=F32)
        aug_k = aug_k - jnp.dot(piece, pk_ref[j], preferred_element_type=F32)
    augq_ref[...] = aug_q.astype(BF16)
    augk_ref[...] = aug_k.astype(BF16)


def _decay(x, w_f, b_f, batch, tm=256):
    n, d = x.shape
    nb = n // batch // tm
    width = H_FOX // 2 * LANES
    place_k, place_q, ones_k, ones_q = _aug_constants()
    const = lambda a: pl.BlockSpec(a.shape, lambda b, i: (0,) * a.ndim)
    out = pl.BlockSpec((tm, width), lambda b, i: (b * nb + i, 0))
    return pl.pallas_call(
        _decay_kernel,
        grid=(batch, nb),
        in_specs=[pl.BlockSpec((tm, d), lambda b, i: (b * nb + i, 0)),
                  pl.BlockSpec((d, LANES), lambda b, i: (0, 0)),
                  pl.BlockSpec((1, LANES), lambda b, i: (0, 0)),
                  const(place_k), const(place_q), const(ones_k), const(ones_q)],
        out_specs=[out, out],
        out_shape=[jax.ShapeDtypeStruct((n, width), BF16)] * 2,
        scratch_shapes=[pltpu.VMEM((1, LANES), F32)],
        compiler_params=_params(("arbitrary", "arbitrary"), 24),
        name="fox_decay",
    )(x, w_f, b_f, jnp.asarray(place_k, BF16), jnp.asarray(place_q, BF16), jnp.asarray(ones_k), jnp.asarray(ones_q))


def _sb_kernel(q_ref, k_ref, v_ref, o_ref):
    t, seg = SB_T, SB_SEG
    nseg = t // seg
    i = pl.program_id(2)
    qh, lane = _head_pair(q_ref[...])
    qneg = [-x for x in qh]
    r = lax.broadcasted_iota(jnp.int32, (seg, seg), 0)
    c = lax.broadcasted_iota(jnp.int32, (seg, seg), 1)
    later = jnp.where(r > c, 1.0, 0.0).astype(BF16)
    row = lax.broadcasted_iota(jnp.int32, (t, t), 0)
    col = lax.broadcasted_iota(jnp.int32, (t, t), 1)
    past = col < row

    def step(j, carry, diagonal):
        start = pl.multiple_of(j * t, t)
        k = k_ref[pl.ds(start, t), :]
        v = v_ref[pl.ds(start, t), :]
        out = []
        for h in range(2):
            tail, acc = carry[h]
            nz = lax.dot_general(qneg[h], k, NT, preferred_element_type=F32)
            minus_abs = lax.bitcast_convert_type(lax.bitcast_convert_type(nz, jnp.uint32) | SIGN_BIT, F32)
            log_1m = jnp.minimum(nz, 0.0) - jnp.log(1.0 + jnp.exp(minus_abs))
            base = log_1m - nz
            if diagonal:
                log_1m = jnp.where(past, log_1m, 0.0)
            sums = [jnp.dot(log_1m[:, b * seg:(b + 1) * seg].astype(BF16), later, preferred_element_type=F32)
                    for b in range(nseg)]
            for b in reversed(range(nseg)):
                sl = slice(b * seg, (b + 1) * seg)
                w = jnp.exp(base[:, sl] + (sums[b] + tail))
                if diagonal:
                    w = jnp.where(past[:, sl], w, 0.0)
                acc = acc + jnp.dot(w.astype(BF16), v[sl], preferred_element_type=F32)
                tail = tail + (sums[b][:, 0:1] + log_1m[:, b * seg:b * seg + 1])
            out.append((tail, acc))
        return tuple(out)

    init = tuple((jnp.zeros((t, 1), F32), jnp.zeros((t, LANES), F32)) for _ in range(2))
    carry = step(i, init, True)
    carry = lax.fori_loop(0, i, lambda jj, c: step(i - 1 - jj, c, False), carry)
    o_ref[...] = jnp.where(lane < HEAD_DIM, carry[0][1], carry[1][1]).astype(o_ref.dtype)


def _sb_attention(qkv, batch):
    n = qkv.shape[0]
    seq = n // batch
    nq = seq // SB_T
    pairs = H_SB // 2
    return pl.pallas_call(
        _sb_kernel,
        grid=(batch, pairs, nq),
        in_specs=_attention_specs(seq, nq, SB_T, *_head_cols(MAIN_SB, MAIN_W)),
        out_specs=pl.BlockSpec((SB_T, LANES), lambda b, p, i: (b * nq + i, p)),
        out_shape=jax.ShapeDtypeStruct((n, pairs * LANES), BF16),
        compiler_params=_params(("parallel", "parallel", "arbitrary"), 56),
        name="sb_attention",
    )(qkv, qkv, qkv)


def _fox_kernel(q_ref, k_ref, v_ref, aq_ref, ak_ref, o_ref):
    tq, tk = FOX_TQ, FOX_TK
    i = pl.program_id(2)
    q = q_ref[...]
    lane = lax.broadcasted_iota(jnp.int32, q.shape, 1)
    own = [lane < HEAD_DIM, lane >= HEAD_DIM]
    qh = [jnp.where(own[h], q, aq_ref[...]) for h in range(2)]
    klane = lax.broadcasted_iota(jnp.int32, (tk, LANES), 1)
    kown = [klane < HEAD_DIM, klane >= HEAD_DIM]
    row = lax.broadcasted_iota(jnp.int32, (tq, tk), 0)
    col = lax.broadcasted_iota(jnp.int32, (tq, tk), 1)

    def step(j, carry, diagonal):
        start = pl.multiple_of(j * tk, tk)
        k = k_ref[pl.ds(start, tk), :]
        v = v_ref[pl.ds(start, tk), :]
        ak = ak_ref[pl.ds(start, tk), :]
        out = []
        for h in range(2):
            s = lax.dot_general(qh[h], jnp.where(kown[h], k, ak), NT, preferred_element_type=F32)
            if diagonal:
                s = jnp.where(col + j * tk <= row + i * tq, s, NEG)
            m, acc = carry[h]
            m_new = jnp.maximum(m, jnp.max(s, axis=1, keepdims=True))
            p = jnp.exp(s - m_new).astype(BF16)
            vh = jnp.where(kown[h], v, jnp.ones_like(v))
            acc = jnp.exp(m - m_new) * acc + jnp.dot(p, vh, preferred_element_type=F32)
            out.append((m_new, acc))
        return tuple(out)

    carry = tuple((jnp.full((tq, 1), NEG, F32), jnp.zeros((tq, LANES), F32)) for _ in range(2))
    n_full = (i * tq) // tk
    carry = lax.fori_loop(0, n_full, lambda j, c: step(j, c, False), carry)
    for d in range(max(tq // tk, 1)):
        carry = step(n_full + d, carry, True)
    o = []
    for h in range(2):
        acc = carry[h][1]
        o.append(acc / acc[:, HEAD_DIM * (1 - h):HEAD_DIM * (1 - h) + 1])
    o_ref[...] = jnp.where(own[0], o[0], o[1]).astype(o_ref.dtype)


def _fox_attention(qkv, aug_q, aug_k, batch):
    n = qkv.shape[0]
    seq = n // batch
    nq = seq // FOX_TQ
    pairs = H_FOX // 2
    specs = _attention_specs(seq, nq, FOX_TQ, *_head_cols(MAIN_FOX, MAIN_W))
    specs += [pl.BlockSpec((FOX_TQ, LANES), lambda b, p, i: (b * nq + i, p)),
              pl.BlockSpec((seq, LANES), lambda b, p, i: (b, p))]
    return pl.pallas_call(
        _fox_kernel,
        grid=(batch, pairs, nq),
        in_specs=specs,
        out_specs=pl.BlockSpec((FOX_TQ, LANES), lambda b, p, i: (b * nq + i, p)),
        out_shape=jax.ShapeDtypeStruct((n, pairs * LANES), BF16),
        compiler_params=_params(("parallel", "parallel", "arbitrary"), 48),
        name="fox_attention",
    )(qkv, qkv, qkv, aug_q, aug_k)


def _moba_kernel(q_ref, k_ref, v_ref, onehot_ref, bias_ref, o_ref, kmean_ref):
    t, mb = MOBA_T, MOBA_BLOCK
    nb = t // mb
    seq = k_ref.shape[0]
    i = pl.program_id(2)

    @pl.when(i == 0)
    def _():
        blk = lax.broadcasted_iota(jnp.int32, (LANES, seq), 0)
        pos = lax.broadcasted_iota(jnp.int32, (LANES, seq), 1)
        member = jnp.where((pos >= blk * mb) & (pos < (blk + 1) * mb), 1.0, 0.0).astype(BF16)
        mean = jnp.dot(member, k_ref[...], preferred_element_type=F32) * (1.0 / mb)
        hi = mean.astype(BF16)
        kmean_ref[0] = hi
        kmean_ref[1] = (mean - hi.astype(F32)).astype(BF16)

    q = q_ref[...]
    lane = lax.broadcasted_iota(jnp.int32, (t, LANES), 1)
    own_half = [lane < HEAD_DIM, lane >= HEAD_DIM]
    zero = jnp.zeros_like(q)
    blk_id = lane.astype(F32)
    own_blk = ((i * t + lax.broadcasted_iota(jnp.int32, (t, LANES), 0)) // mb).astype(F32)
    fully_past = blk_id < own_blk

    qh = []
    for h in range(2):
        q_only = jnp.where(own_half[h], q, zero)
        gate = (lax.dot_general(q_only, kmean_ref[0], NT, preferred_element_type=F32)
                + lax.dot_general(q_only, kmean_ref[1], NT, preferred_element_type=F32))
        gate = jnp.where(fully_past, gate, -jnp.inf)
        picked = jnp.zeros((t, LANES), jnp.bool_)
        for _ in range(MOBA_TOPK):
            best = jnp.max(gate, axis=1, keepdims=True)
            first = jnp.min(jnp.where(gate == best, blk_id, float(LANES)), axis=1, keepdims=True)
            pick = blk_id == first
            picked = picked | pick
            gate = jnp.where(pick, -jnp.inf, gate)
        allowed = (picked & fully_past) | (blk_id == own_blk)
        penalty = jnp.where(allowed, 0.0, NEG)
        if h == 0:
            penalty = pltpu.roll(penalty, HEAD_DIM, axis=1)
        qh.append(jnp.where(own_half[h], q, penalty.astype(BF16)))

    def step(j, carry):
        start = pl.multiple_of(j * t, t)
        k = k_ref[pl.ds(start, t), :]
        v = v_ref[pl.ds(start, t), :]
        onehot = onehot_ref[pl.ds(start, t), :]
        out = []
        for h in range(2):
            s = lax.dot_general(qh[h], jnp.where(own_half[h], k, onehot), NT, preferred_element_type=F32)
            rows = []
            for a in range(nb):
                tiles = [bias_ref[jnp.clip((i - j) * nb + a - b, 0, MOBA_NT), h] for b in range(nb)]
                rows.append(jnp.concatenate(tiles, axis=1))
            s = s + jnp.concatenate(rows, axis=0)
            m, acc = carry[h]
            m_new = jnp.maximum(m, jnp.max(s, axis=1, keepdims=True))
            p = jnp.exp(s - m_new).astype(BF16)
            vh = jnp.where(own_half[h], v, jnp.ones_like(v))
            acc = jnp.exp(m - m_new) * acc + jnp.dot(p, vh, preferred_element_type=F32)
            out.append((m_new, acc))
        return tuple(out)

    carry = tuple((jnp.full((t, 1), NEG, F32), jnp.zeros((t, LANES), F32)) for _ in range(2))
    carry = lax.fori_loop(0, i + 1, step, carry)
    o = []
    for h in range(2):
        acc = carry[h][1]
        o.append(acc / acc[:, HEAD_DIM * (1 - h):HEAD_DIM * (1 - h) + 1])
    o_ref[...] = jnp.where(own_half[0], o[0], o[1]).astype(o_ref.dtype)


def _moba_attention(qkv, bias_tiles, batch):
    n = qkv.shape[0]
    seq = n // batch
    assert seq % MOBA_T == 0 and seq // MOBA_BLOCK <= HEAD_DIM
    nq = seq // MOBA_T
    pairs = H_MOBA // 2
    blk = np.arange(seq)[:, None] // MOBA_BLOCK
    onehot = jnp.asarray(blk == (np.arange(LANES)[None, :] % HEAD_DIM), BF16)
    specs = _attention_specs(seq, nq, MOBA_T, *_head_cols(MAIN_MOBA, MAIN_W))
    specs.append(pl.BlockSpec((seq, LANES), lambda b, p, i: (0, 0)))
    specs.append(pl.BlockSpec((None, MOBA_NT + 1, 2, MOBA_BLOCK, MOBA_BLOCK), lambda b, p, i: (p, 0, 0, 0, 0)))
    return pl.pallas_call(
        _moba_kernel,
        grid=(batch, pairs, nq),
        in_specs=specs,
        out_specs=pl.BlockSpec((MOBA_T, LANES), lambda b, p, i: (b * nq + i, p)),
        out_shape=jax.ShapeDtypeStruct((n, pairs * LANES), BF16),
        scratch_shapes=[pltpu.VMEM((2, LANES, LANES), BF16)],
        compiler_params=_params(("arbitrary", "arbitrary", "arbitrary"), 56),
        name="moba_attention",
    )(qkv, qkv, qkv, onehot, bias_tiles)


def _moba_bias_tiles(rel_bias):
    r = jnp.arange(MOBA_BLOCK)
    dist = (jnp.arange(MOBA_NT + 1)[:, None, None] * MOBA_BLOCK + r[None, :, None] - r[None, None, :])
    tiles = _bias_lookup(rel_bias[:, H_DIL:], _t5_bucket(jnp.maximum(dist, 0)))
    tiles = jnp.where((dist >= 0)[..., None], tiles, NEG)
    tiles = tiles.transpose(3, 0, 1, 2).reshape(H_MOBA // 2, 2, MOBA_NT + 1, MOBA_BLOCK, MOBA_BLOCK)
    return tiles.transpose(0, 2, 1, 3, 4).astype(F32)


def _check_moba_saturation():
    n = np.float32(MOBA_NT * MOBA_BLOCK - (MOBA_BLOCK - 1))
    exact = N_BUCKETS // 2
    large = exact + int(np.log(n / np.float32(exact)) / math.log(MAX_DIST / exact) * (N_BUCKETS - exact))
    assert large >= N_BUCKETS - 1, "MOBA_NT too small for the bias bucket table"


_check_moba_saturation()


def _dilated_kernel(q_ref, k_ref, v_ref, bias_ref, o_ref, lse_ref):
    w = DIL_W
    nb = q_ref.shape[0] // w
    row = lax.broadcasted_iota(jnp.int32, (w, w), 0)
    col = lax.broadcasted_iota(jnp.int32, (w, w), 1)
    in_cur = col <= row
    in_prev = col >= row

    def body(n, _):
        cur = pl.multiple_of(n * w, w)
        prev = pl.multiple_of(jnp.maximum(n - 1, 0) * w, w)
        qh, lane = _head_pair(q_ref[pl.ds(cur, w), :])
        k_cur, v_cur = k_ref[pl.ds(cur, w), :], v_ref[pl.ds(cur, w), :]
        k_prev, v_prev = k_ref[pl.ds(prev, w), :], v_ref[pl.ds(prev, w), :]
        has_prev = n > 0
        outs, lses = [], []
        for h in range(2):
            z_prev = lax.dot_general(qh[h], k_prev, NT, preferred_element_type=F32) + bias_ref[h, :, 0:w]
            z_cur = lax.dot_general(qh[h], k_cur, NT, preferred_element_type=F32) + bias_ref[h, :, w:2 * w]
            z_prev = jnp.where(in_prev & has_prev, z_prev, NEG)
            z_cur = jnp.where(in_cur, z_cur, NEG)
            m = jnp.maximum(jnp.max(z_prev, axis=1, keepdims=True), jnp.max(z_cur, axis=1, keepdims=True))
            p_prev = jnp.exp(z_prev - m)
            p_cur = jnp.exp(z_cur - m)
            l = jnp.sum(p_prev, axis=1, keepdims=True) + jnp.sum(p_cur, axis=1, keepdims=True)
            o = (jnp.dot(p_prev.astype(BF16), v_prev, preferred_element_type=F32)
                 + jnp.dot(p_cur.astype(BF16), v_cur, preferred_element_type=F32))
            outs.append(o / l)
            lses.append(m + jnp.log(l))
        o_ref[pl.ds(cur, w), :] = jnp.where(lane < HEAD_DIM, outs[0], outs[1])
        lse_ref[pl.ds(cur, w), :] = jnp.where(lane < HEAD_DIM, lses[0], lses[1])
        return 0

    lax.fori_loop(0, nb, body, 0)


def _dilated_group(qkv, head_off, width, bias, batch, group):
    window, dil = DIL_GROUPS[group]
    assert window // dil == DIL_W
    n = qkv.shape[0]
    sub = n // batch // dil
    assert sub % DIL_W == 0
    pairs = H_DIL_PER_GROUP // 2
    q_col, k_col, v_col = _head_cols(head_off, width)

    def spec(col0):
        return pl.BlockSpec((sub, LANES), lambda b, p, r: (b * dil + r, col0 + p))

    out_spec = pl.BlockSpec((sub, LANES), lambda b, p, r: (b * dil + r, p))
    out_shape = jax.ShapeDtypeStruct((n, pairs * LANES), F32)
    return pl.pallas_call(
        _dilated_kernel,
        grid=(batch, pairs, dil),
        in_specs=[spec(q_col), spec(k_col), spec(v_col),
                  pl.BlockSpec((None, 2, DIL_W, 2 * DIL_W), lambda b, p, r: (group * pairs + p, 0, 0, 0))],
        out_specs=[out_spec, out_spec],
        out_shape=[out_shape, out_shape],
        compiler_params=_params(("parallel", "parallel", "parallel"), 48),
        name=f"dilated_attention_g{group}",
    )(qkv, qkv, qkv, bias)


def _by_residue(a, batch, dil):
    n, c = a.shape
    return a.reshape(batch, n // batch // dil, dil, c).transpose(0, 2, 1, 3).reshape(n, c)


def _by_position(a, batch, dil):
    n, c = a.shape
    return a.reshape(batch, dil, n // batch // dil, c).transpose(0, 2, 1, 3).reshape(n, c)


def _dilated_bias_tiles(rel_bias):
    steps = jnp.arange(DIL_W)[:, None] + DIL_W - jnp.arange(2 * DIL_W)[None, :]
    tiles = []
    for g, (_, dil) in enumerate(DIL_GROUPS):
        tab = rel_bias[:, g * H_DIL_PER_GROUP:(g + 1) * H_DIL_PER_GROUP]
        tiles.append(_bias_lookup(tab, _t5_bucket(jnp.maximum(steps, 0) * dil)).transpose(2, 0, 1))
    tiles = jnp.stack(tiles, 0).astype(F32)
    return tiles.reshape(len(DIL_GROUPS) * H_DIL_PER_GROUP // 2, 2, DIL_W, 2 * DIL_W)


def _dilated_combine_kernel(o0, o1, o2, l0, l1, l2, out_ref):
    lse = [l0[...], l1[...], l2[...]]
    m = jnp.maximum(jnp.maximum(lse[0], lse[1]), lse[2])
    e = [jnp.exp(x - m) for x in lse]
    den = e[0] + e[1] + e[2]
    out = (e[0] / den) * o0[...] + (e[1] / den) * o1[...] + (e[2] / den) * o2[...]
    out_ref[...] = out.astype(out_ref.dtype)


def _dilated_combine(os, lses, tm=1024):
    n, c = os[0].shape
    spec = pl.BlockSpec((tm, c), lambda i: (i, 0))
    return pl.pallas_call(
        _dilated_combine_kernel,
        grid=(n // tm,),
        in_specs=[spec] * 6,
        out_specs=spec,
        out_shape=jax.ShapeDtypeStruct((n, c), BF16),
        compiler_params=_params(("parallel",), 32),
        name="dilated_combine",
    )(*os, *lses)


def _merge_kernel(x_ref, o_sb, o_dil, o_moba, o_fox, wg0, wg1, wg2, wg3, bg_ref,
                  w_sb, w_dil, w_moba, w_fox, out_ref):
    x = x_ref[...]
    acc = None
    branches = ((o_sb, w_sb, wg0), (o_dil, w_dil, wg1), (o_moba, w_moba, wg2), (o_fox, w_fox, wg3))
    for b, (o_ref, w_ref, wg_ref) in enumerate(branches):
        gate = jax.nn.sigmoid(jnp.dot(x, wg_ref[...], preferred_element_type=F32) + bg_ref[b])
        term = gate * jnp.dot(o_ref[...], w_ref[...], preferred_element_type=F32)
        acc = term if acc is None else acc + term
    out_ref[...] = acc.astype(out_ref.dtype)


def _gated_merge(x_bf, outs, w_gate, b_gate, w_branch, tm=512, tn=512):
    n, d = x_bf.shape
    nj = d // tn
    row = lambda width: pl.BlockSpec((tm, width), lambda j, i: (i, 0))
    in_specs = [row(d)] + [row(o.shape[1]) for o in outs]
    in_specs += [pl.BlockSpec((d, tn), functools.partial(lambda j, i, b: (0, b * nj + j), b=b)) for b in range(N_BRANCH)]
    in_specs.append(pl.BlockSpec((N_BRANCH, 1, tn), lambda j, i: (0, 0, j)))
    in_specs += [pl.BlockSpec((w.shape[0], tn), lambda j, i: (0, j)) for w in w_branch]
    return pl.pallas_call(
        _merge_kernel,
        grid=(nj, n // tm),
        in_specs=in_specs,
        out_specs=pl.BlockSpec((tm, tn), lambda j, i: (i, j)),
        out_shape=jax.ShapeDtypeStruct((n, d), BF16),
        compiler_params=_params(("parallel", "parallel"), 48),
        name="gated_merge",
    )(x_bf, *outs, w_gate, w_gate, w_gate, w_gate, b_gate, *w_branch)


def _layer_norm_store(z, g_ref, b_ref, xo_ref, xb_ref):
    mu = jnp.mean(z, axis=1, keepdims=True)
    zc = z - mu
    var = jnp.mean(zc * zc, axis=1, keepdims=True)
    out = zc * lax.rsqrt(var + LN_EPS) * g_ref[...] + b_ref[...]
    xo_ref[...] = out
    xb_ref[...] = out.astype(BF16)


def _out_ln_kernel(m_ref, w_ref, x_ref, g_ref, b_ref, xo_ref, xb_ref):
    y = jnp.dot(m_ref[...], w_ref[...], preferred_element_type=F32)
    _layer_norm_store(DN_ALPHA * x_ref[...] + y, g_ref, b_ref, xo_ref, xb_ref)


def _combine_ln_kernel(y_ref, gate_ref, x_ref, g_ref, b_ref, xo_ref, xb_ref):
    gate = gate_ref[...]
    z = DN_ALPHA * x_ref[...]
    for k in range(TOP_K):
        z = z + gate[:, k:k + 1] * y_ref[k]
    _layer_norm_store(z, g_ref, b_ref, xo_ref, xb_ref)


def _ln_call(kernel, lead_specs, lead_args, x, g, b, tm, name):
    n, d = x.shape
    row = pl.BlockSpec((tm, d), lambda i: (i, 0))
    vec = pl.BlockSpec((1, d), lambda i: (0, 0))
    return pl.pallas_call(
        kernel,
        grid=(n // tm,),
        in_specs=lead_specs + [row, vec, vec],
        out_specs=[row, row],
        out_shape=[jax.ShapeDtypeStruct((n, d), F32), jax.ShapeDtypeStruct((n, d), BF16)],
        compiler_params=_params(("parallel",), 48),
        name=name,
    )(*lead_args, x, g, b)


def _out_proj_ln(merged, w_out, x, g, b, tm=256):
    d = x.shape[1]
    specs = [pl.BlockSpec((tm, d), lambda i: (i, 0)), pl.BlockSpec((d, d), lambda i: (0, 0))]
    return _ln_call(_out_ln_kernel, specs, (merged, w_out), x, g, b, tm, "out_proj_ln")


def _combine_ln(yk, gate, x, g, b, tm=256):
    d = x.shape[1]
    specs = [pl.BlockSpec((TOP_K, tm, d), lambda i: (0, i, 0)), pl.BlockSpec((tm, LANES), lambda i: (i, 0))]
    return _ln_call(_combine_ln_kernel, specs, (yk, gate), x, g, b, tm, "moe_combine_ln")


def _router_kernel(x_ref, w_ref, b_ref, idx_ref, gate_ref, rank_ref, count_ref, seen_ref):
    @pl.when(pl.program_id(0) == 0)
    def _():
        seen_ref[...] = jnp.zeros_like(seen_ref)

    tm = x_ref.shape[0]
    logits = jnp.dot(x_ref[...], w_ref[...], preferred_element_type=F32, precision=HIGHEST) + b_ref[...]
    col = lax.broadcasted_iota(jnp.int32, logits.shape, 1)
    colf = col.astype(F32)
    logits = jnp.where(col < N_EXPERTS, logits, -jnp.inf)
    vals, idxs, hits = [], [], []
    for _ in range(TOP_K):
        best = jnp.max(logits, axis=1, keepdims=True)
        first = jnp.min(jnp.where(logits == best, colf, float(LANES)), axis=1, keepdims=True)
        hit = colf == first
        logits = jnp.where(hit, -jnp.inf, logits)
        vals.append(best)
        idxs.append(first)
        hits.append(jnp.where(hit, 1.0, 0.0))
    es = [jnp.exp(v - vals[0]) for v in vals]
    den = es[0] + es[1] + es[2] + es[3]

    chosen = hits[0] + hits[1] + hits[2] + hits[3]
    row = lax.broadcasted_iota(jnp.int32, (tm, tm), 0)
    before = jnp.where(lax.broadcasted_iota(jnp.int32, (tm, tm), 1) < row, 1.0, 0.0).astype(BF16)
    earlier = jnp.dot(before, chosen.astype(BF16), preferred_element_type=F32) + seen_ref[...]
    seen = seen_ref[...] + jnp.sum(chosen, axis=0, keepdims=True)
    seen_ref[...] = seen
    count_ref[...] = seen

    idx_out = jnp.zeros(logits.shape, F32)
    gate_out = jnp.zeros(logits.shape, F32)
    rank_out = jnp.zeros(logits.shape, F32)
    for r in range(TOP_K):
        idx_out = jnp.where(col == r, idxs[r], idx_out)
        gate_out = jnp.where(col == r, es[r] / den, gate_out)
        rank_out = jnp.where(col == r, jnp.sum(hits[r] * earlier, axis=1, keepdims=True), rank_out)
    idx_ref[...] = idx_out.astype(jnp.int32)
    gate_ref[...] = gate_out
    rank_ref[...] = rank_out.astype(jnp.int32)


def _router(x, w_r, b_r, tm=512):
    n, d = x.shape
    out = pl.BlockSpec((tm, LANES), lambda i: (i, 0))
    return pl.pallas_call(
        _router_kernel,
        grid=(n // tm,),
        in_specs=[pl.BlockSpec((tm, d), lambda i: (i, 0)),
                  pl.BlockSpec((d, LANES), lambda i: (0, 0)),
                  pl.BlockSpec((1, LANES), lambda i: (0, 0))],
        out_specs=[out, out, out, pl.BlockSpec((1, LANES), lambda i: (0, 0))],
        out_shape=[jax.ShapeDtypeStruct((n, LANES), jnp.int32), jax.ShapeDtypeStruct((n, LANES), F32),
                   jax.ShapeDtypeStruct((n, LANES), jnp.int32), jax.ShapeDtypeStruct((1, LANES), F32)],
        scratch_shapes=[pltpu.VMEM((1, LANES), F32)],
        compiler_params=_params(("arbitrary",), 32),
        name="moe_router",
    )(x, w_r, b_r)


def _expert_kernel(blk_exp_ref, xs_ref, wu_ref, bu_ref, wd_ref, bd_ref, y_ref):
    del blk_exp_ref
    h = jnp.dot(xs_ref[...], wu_ref[...], preferred_element_type=F32) + bu_ref[...]
    glu = jnp.minimum(h[:, :D_FF], SWIGLU_LIMIT)
    lin = jnp.clip(h[:, D_FF:], -SWIGLU_LIMIT, SWIGLU_LIMIT)
    act = glu * jax.nn.sigmoid(SWIGLU_ALPHA * glu) * (lin + 1.0)
    y_ref[...] = jnp.dot(act.astype(BF16), wd_ref[...], preferred_element_type=F32) + bd_ref[...]


def _expert_ffn(xs, blk_exp, w_up, b_up, w_down, b_down):
    p, d = xs.shape
    grid_spec = pltpu.PrefetchScalarGridSpec(
        num_scalar_prefetch=1,
        grid=(p // MOE_ROWS,),
        in_specs=[pl.BlockSpec((MOE_ROWS, d), lambda i, e: (i, 0)),
                  pl.BlockSpec((None, d, 2 * D_FF), lambda i, e: (e[i], 0, 0)),
                  pl.BlockSpec((None, 1, 2 * D_FF), lambda i, e: (e[i], 0, 0)),
                  pl.BlockSpec((None, D_FF, d), lambda i, e: (e[i], 0, 0)),
                  pl.BlockSpec((None, 1, d), lambda i, e: (e[i], 0, 0))],
        out_specs=pl.BlockSpec((MOE_ROWS, d), lambda i, e: (i, 0)),
    )
    return pl.pallas_call(
        _expert_kernel,
        grid_spec=grid_spec,
        out_shape=jax.ShapeDtypeStruct((p, d), F32),
        compiler_params=_params(("arbitrary",), 48),
        name="expert_ffn",
    )(blk_exp, xs, w_up, b_up, w_down, b_down)


def _moe(x, x_bf, w_r, b_r, w_up, b_up, w_down, b_down):
    n, d = x.shape
    nk = n * TOP_K
    idx, gate, rank, count = _router(x, w_r, b_r)
    counts = count[0, :N_EXPERTS].astype(jnp.int32)
    start = jnp.cumsum(counts) - counts
    padded = (counts + MOE_ROWS - 1) // MOE_ROWS * MOE_ROWS
    pend = jnp.cumsum(padded)
    pstart = pend - padded
    experts = jnp.arange(N_EXPERTS, dtype=jnp.int32)
    chosen = idx[:, :TOP_K, None] == experts
    slot = jnp.sum(jnp.where(chosen, pstart, 0), axis=2) + rank[:, :TOP_K]
    n_blocks = -(-(nk + N_EXPERTS * (MOE_ROWS - 1)) // MOE_ROWS)
    first_row = jnp.arange(n_blocks, dtype=jnp.int32) * MOE_ROWS
    blk_exp = jnp.minimum(jnp.sum(pend[None, :] <= first_row[:, None], axis=1), N_EXPERTS - 1).astype(jnp.int32)
    order = jnp.argsort(slot.reshape(nk)).astype(jnp.int32)
    blk_is = blk_exp[:, None] == experts
    blk_start = jnp.sum(jnp.where(blk_is, start, 0), axis=1)
    blk_pstart = jnp.sum(jnp.where(blk_is, pstart, 0), axis=1)
    blk_count = jnp.sum(jnp.where(blk_is, counts, 0), axis=1)
    within = first_row[:, None] + jnp.arange(MOE_ROWS, dtype=jnp.int32)[None, :] - blk_pstart[:, None]
    pair = jnp.clip(blk_start[:, None] + within, 0, nk - 1)
    slot_tok = jnp.where(within < blk_count[:, None], order[pair] // TOP_K, 0).reshape(n_blocks * MOE_ROWS)
    xs = x_bf[slot_tok]
    y = _expert_ffn(xs, blk_exp, w_up, b_up, w_down, b_down)
    yk = y[slot.T.reshape(nk)].reshape(TOP_K, n, d)
    return yk, gate


def kernel(x, rel_bias, w_in, b_forget, b_gate, w_br_sb, w_br_dil, w_br_moba, w_br_fox, w_out,
           ln1_g, ln1_b, w_router, b_router, w_up, b_up, w_down, b_down, ln2_g, ln2_b):
    batch, seq, d = x.shape
    n = batch * seq
    depth = w_in.shape[0]

    w_in_bf = lax.optimization_barrier(w_in.astype(BF16))

    def qkv_weight(heads):
        cols = []
        for sec, scale in ((0, HEAD_DIM ** -0.5), (1, 1.0), (2, 1.0)):
            for lo, cnt in heads:
                a = sec * W_MIX + lo * HEAD_DIM
                cols.append(w_in_bf[:, :, a:a + cnt * HEAD_DIM] * jnp.asarray(scale, BF16))
        return jnp.concatenate(cols, axis=2)

    w_main = qkv_weight(((SB_OFF, H_SB), (DIL_OFF, H_DIL_PER_GROUP), (MOBA_OFF, H_MOBA), (FOX_OFF, H_FOX)))
    w_dil = [qkv_weight(((DIL_OFF + g * H_DIL_PER_GROUP, H_DIL_PER_GROUP),)) for g in range(1, len(DIL_GROUPS))]
    w_f = jnp.pad(w_in[:, :, C_QKV:C_QKV + H_FOX], ((0, 0), (0, 0), (0, LANES - H_FOX)))
    b_f = jnp.pad(b_forget, ((0, 0), (0, LANES - H_FOX))).reshape(depth, 1, LANES)
    w_gate = w_in_bf[:, :, C_QKV + H_FOX:]
    b_gate3 = b_gate.reshape(depth, N_BRANCH, 1, d)
    w_branch = [w.astype(BF16) for w in (w_br_sb, w_br_dil, w_br_moba, w_br_fox)]
    w_out_bf = w_out.astype(BF16)
    w_r = jnp.pad(w_router, ((0, 0), (0, 0), (0, LANES - N_EXPERTS)))
    b_r = jnp.pad(b_router, ((0, 0), (0, LANES - N_EXPERTS))).reshape(depth, 1, LANES)
    w_up_bf = w_up.astype(BF16)
    w_down_bf = w_down.astype(BF16)
    b_up3 = b_up.reshape(depth, N_EXPERTS, 1, 2 * D_FF)
    b_down3 = b_down.reshape(depth, N_EXPERTS, 1, d)
    moba_bias = _moba_bias_tiles(rel_bias)
    dil_bias = _dilated_bias_tiles(rel_bias)

    xf = x.reshape(n, d)
    xb = xf.astype(BF16)
    for l in range(depth):
        qkv = _matmul(xb, w_main[l], 1024, 768, BF16, "qkv_proj")
        aug_q, aug_k = _decay(xf, w_f[l], b_f[l], batch)
        o_sb = _sb_attention(qkv, batch)
        os, lses = [], []
        o, lse = _dilated_group(qkv, MAIN_DIL, MAIN_W, dil_bias, batch, 0)
        os.append(o)
        lses.append(lse)
        for g in range(1, len(DIL_GROUPS)):
            dil = DIL_GROUPS[g][1]
            qkv_g = _matmul(_by_residue(xb, batch, dil), w_dil[g - 1][l], 1024, 3 * DIL_GROUP_W, BF16,
                            f"qkv_proj_g{g}")
            o, lse = _dilated_group(qkv_g, 0, DIL_GROUP_W, dil_bias, batch, g)
            os.append(_by_position(o, batch, dil))
            lses.append(_by_position(lse, batch, dil))
        o_dil = _dilated_combine(os, lses)
        o_moba = _moba_attention(qkv, moba_bias, batch)
        o_fox = _fox_attention(qkv, aug_q, aug_k, batch)
        merged = _gated_merge(xb, (o_sb, o_dil, o_moba, o_fox), w_gate[l], b_gate3[l],
                              [w[l] for w in w_branch])
        xf, xb = _out_proj_ln(merged, w_out_bf[l], xf, ln1_g[l].reshape(1, d), ln1_b[l].reshape(1, d))
        yk, gate = _moe(xf, xb, w_r[l], b_r[l], w_up_bf[l], b_up3[l], w_down_bf[l], b_down3[l])
        xf, xb = _combine_ln(yk, gate, xf, ln2_g[l].reshape(1, d), ln2_b[l].reshape(1, d))
    return xf.reshape(batch, seq, d)
```

```python
import functools
import math

import numpy as np
import jax
import jax.numpy as jnp
from jax import lax
from jax.experimental import pallas as pl
from jax.experimental.pallas import tpu as pltpu

D_MODEL = 2048
DEPTH = 4
HEAD_DIM = 64
H_SB = 8
H_DIL_PER_GROUP = 4
DIL_GROUPS = ((128, 1), (512, 4), (2048, 16))
H_DIL = H_DIL_PER_GROUP * len(DIL_GROUPS)
H_MOBA = 8
H_FOX = 8
H_ALL = H_SB + H_DIL + H_MOBA + H_FOX
SB_OFF = 0
DIL_OFF = SB_OFF + H_SB
MOBA_OFF = DIL_OFF + H_DIL
FOX_OFF = MOBA_OFF + H_MOBA
W_MIX = H_ALL * HEAD_DIM
N_BRANCH = 4
C_QKV = 3 * W_MIX
MOBA_BLOCK = 256
MOBA_TOPK = 3
N_BUCKETS = 32
MAX_DIST = 2048
N_EXPERTS = 32
TOP_K = 4
D_FF = 768
SWIGLU_ALPHA = 1.702
SWIGLU_LIMIT = 7.0
LN_EPS = 1e-5
DN_ALPHA = (2 * DEPTH) ** 0.25

F32 = jnp.float32
BF16 = jnp.bfloat16
LANES = 128
NEG = -1e30
HIGHEST = lax.Precision.HIGHEST
MIB = 1024 * 1024
NT = (((1,), (1,)), ((), ()))
SIGN_BIT = np.uint32(0x80000000)

MAIN_SB = 0
MAIN_DIL = MAIN_SB + H_SB
MAIN_MOBA = MAIN_DIL + H_DIL_PER_GROUP
MAIN_FOX = MAIN_MOBA + H_MOBA
MAIN_W = (MAIN_FOX + H_FOX) * HEAD_DIM
DIL_GROUP_W = H_DIL_PER_GROUP * HEAD_DIM
DIL_W = 128
MOE_ROWS = 256
SB_T = 1024
SB_SEG = 256
SB_DEAD_LOG = -110.0
FOX_TQ = 1024
FOX_TK = 1024
MOBA_T = 1024
MOBA_NT = 8
AUG_PARTS = 3


def _params(sem, vmem_mib):
    return pltpu.CompilerParams(dimension_semantics=sem, vmem_limit_bytes=vmem_mib * MIB)


def _t5_bucket(n):
    exact = N_BUCKETS // 2
    nf = jnp.maximum(n, 1).astype(F32)
    large = exact + (jnp.log(nf / exact) / math.log(MAX_DIST / exact) * (N_BUCKETS - exact)).astype(jnp.int32)
    large = jnp.minimum(large, N_BUCKETS - 1)
    return jnp.where(n < exact, n, large)


def _bias_lookup(table, bucket):
    onehot = (bucket[..., None] == jnp.arange(N_BUCKETS)).astype(F32)
    return jnp.einsum('...k,kh->...h', onehot, table.astype(F32), precision=HIGHEST)


def _head_pair(q):
    lane = lax.broadcasted_iota(jnp.int32, q.shape, 1)
    zero = jnp.zeros_like(q)
    return [jnp.where(lane < HEAD_DIM, q, zero), jnp.where(lane >= HEAD_DIM, q, zero)], lane


def _attention_specs(seq, nq, tq, q_col, k_col, v_col):
    return [pl.BlockSpec((tq, LANES), lambda b, p, i: (b * nq + i, q_col + p)),
            pl.BlockSpec((seq, LANES), lambda b, p, i: (b, k_col + p)),
            pl.BlockSpec((seq, LANES), lambda b, p, i: (b, v_col + p))]


def _head_cols(off, width):
    q_col = off * HEAD_DIM // LANES
    return q_col, q_col + width // LANES, q_col + 2 * width // LANES


def _matmul_kernel(a_ref, w_ref, o_ref):
    o_ref[...] = jnp.dot(a_ref[...], w_ref[...], preferred_element_type=F32).astype(o_ref.dtype)


def _matmul(a, w, tm, tn, out_dtype, name):
    m, k = a.shape
    n = w.shape[1]
    return pl.pallas_call(
        _matmul_kernel,
        grid=(n // tn, m // tm),
        in_specs=[pl.BlockSpec((tm, k), lambda j, i: (i, 0)),
                  pl.BlockSpec((k, tn), lambda j, i: (0, j))],
        out_specs=pl.BlockSpec((tm, tn), lambda j, i: (i, j)),
        out_shape=jax.ShapeDtypeStruct((m, n), out_dtype),
        compiler_params=_params(("parallel", "parallel"), 40),
        name=name,
    )(a, w)


GATE_COL0 = C_QKV + H_FOX
GATE_ALIGNED = GATE_COL0 // LANES * LANES
GATE_TILE = 256


def _gate_weight_kernel(a_ref, b_ref, o_ref):
    shift = GATE_COL0 - GATE_ALIGNED
    both = jnp.concatenate([a_ref[...], b_ref[...]], axis=1)
    o_ref[...] = both[:, shift:shift + GATE_TILE].astype(BF16)


def _gate_weights(w_in):
    depth, d, _ = w_in.shape
    width = N_BRANCH * d
    assert GATE_ALIGNED % GATE_TILE == 0 and width % GATE_TILE == 0
    a0 = GATE_ALIGNED // GATE_TILE
    per = GATE_TILE // LANES
    return pl.pallas_call(
        _gate_weight_kernel,
        grid=(depth, width // GATE_TILE),
        in_specs=[pl.BlockSpec((None, d, GATE_TILE), lambda l, j: (l, 0, a0 + j)),
                  pl.BlockSpec((None, d, LANES), lambda l, j: (l, 0, (a0 + j + 1) * per))],
        out_specs=pl.BlockSpec((None, d, GATE_TILE), lambda l, j: (l, 0, j)),
        out_shape=jax.ShapeDtypeStruct((depth, d, width), BF16),
        compiler_params=_params(("parallel", "parallel"), 24),
        name="gate_weights",
    )(w_in, w_in)


def _aug_constants():
    pairs = H_FOX // 2
    place_k = np.zeros((AUG_PARTS, LANES, pairs * LANES), np.float32)
    place_q = np.zeros((AUG_PARTS, LANES, pairs * LANES), np.float32)
    ones_k = np.zeros((1, pairs * LANES), np.float32)
    ones_q = np.zeros((1, pairs * LANES), np.float32)
    for g in range(H_FOX):
        p, h = divmod(g, 2)
        base = p * LANES + HEAD_DIM * (1 - h)
        for j in range(AUG_PARTS):
            place_k[j, g, base + j] = 1.0
            place_q[j, g, base + AUG_PARTS + j] = 1.0
            ones_k[0, base + AUG_PARTS + j] = 1.0
            ones_q[0, base + j] = 1.0
    return place_k, place_q, ones_k, ones_q


def _split_parts(x):
    parts = []
    for _ in range(AUG_PARTS):
        piece = x.astype(BF16)
        parts.append(piece)
        x = x - piece.astype(F32)
    return parts


def _decay_kernel(x_ref, wf_ref, bf_ref, pk_ref, pq_ref, ok_ref, oq_ref, augq_ref, augk_ref, carry_ref):
    @pl.when(pl.program_id(1) == 0)
    def _():
        carry_ref[...] = jnp.zeros_like(carry_ref)

    tm = x_ref.shape[0]
    f = jnp.dot(x_ref[...], wf_ref[...], preferred_element_type=F32, precision=HIGHEST) + bf_ref[...]
    logf = jnp.minimum(f, 0.0) - jnp.log1p(jnp.exp(-jnp.abs(f)))
    row = lax.broadcasted_iota(jnp.int32, (tm, tm), 0)
    col = lax.broadcasted_iota(jnp.int32, (tm, tm), 1)
    tri = jnp.where(col <= row, 1.0, 0.0).astype(F32)
    c = jnp.dot(tri, logf, preferred_element_type=F32, precision=HIGHEST) + carry_ref[...]
    carry_ref[...] = c[tm - 1:tm, :]
    aug_q = oq_ref[...]
    aug_k = ok_ref[...]
    for j, piece in enumerate(_split_parts(c)):
        aug_q = aug_q + jnp.dot(piece, pq_ref[j], preferred_element_type=F32)
        aug_k = aug_k - jnp.dot(piece, pk_ref[j], preferred_element_type=F32)
    augq_ref[...] = aug_q.astype(BF16)
    augk_ref[...] = aug_k.astype(BF16)


def _decay(x, w_f, b_f, batch, tm=256):
    n, d = x.shape
    nb = n // batch // tm
    width = H_FOX // 2 * LANES
    place_k, place_q, ones_k, ones_q = _aug_constants()
    const = lambda a: pl.BlockSpec(a.shape, lambda b, i: (0,) * a.ndim)
    out = pl.BlockSpec((tm, width), lambda b, i: (b * nb + i, 0))
    return pl.pallas_call(
        _decay_kernel,
        grid=(batch, nb),
        in_specs=[pl.BlockSpec((tm, d), lambda b, i: (b * nb + i, 0)),
                  pl.BlockSpec((d, LANES), lambda b, i: (0, 0)),
                  pl.BlockSpec((1, LANES), lambda b, i: (0, 0)),
                  const(place_k), const(place_q), const(ones_k), const(ones_q)],
        out_specs=[out, out],
        out_shape=[jax.ShapeDtypeStruct((n, width), BF16)] * 2,
        scratch_shapes=[pltpu.VMEM((1, LANES), F32)],
        compiler_params=_params(("arbitrary", "arbitrary"), 24),
        name="fox_decay",
    )(x, w_f, b_f, jnp.asarray(place_k, BF16), jnp.asarray(place_q, BF16), jnp.asarray(ones_k), jnp.asarray(ones_q))


def _sb_kernel(q_ref, k_ref, v_ref, o_ref):
    t, seg = SB_T, SB_SEG
    nseg = t // seg
    i = pl.program_id(2)
    qh, lane = _head_pair(q_ref[...])
    qneg = [-x for x in qh]
    r = lax.broadcasted_iota(jnp.int32, (seg, seg), 0)
    c = lax.broadcasted_iota(jnp.int32, (seg, seg), 1)
    later = jnp.where(r > c, 1.0, 0.0).astype(BF16)
    row = lax.broadcasted_iota(jnp.int32, (t, t), 0)
    col = lax.broadcasted_iota(jnp.int32, (t, t), 1)
    past = col < row

    def step(j, carry, diagonal):
        start = pl.multiple_of(j * t, t)
        k = k_ref[pl.ds(start, t), :]
        v = v_ref[pl.ds(start, t), :]
        out = []
        for h in range(2):
            tail, acc = carry[h]
            nz = lax.dot_general(qneg[h], k, NT, preferred_element_type=F32)
            minus_abs = lax.bitcast_convert_type(lax.bitcast_convert_type(nz, jnp.uint32) | SIGN_BIT, F32)
            log_1m = jnp.minimum(nz, 0.0) - jnp.log(1.0 + jnp.exp(minus_abs))
            base = log_1m - nz
            if diagonal:
                log_1m = jnp.where(past, log_1m, 0.0)
            sums = [jnp.dot(log_1m[:, b * seg:(b + 1) * seg].astype(BF16), later, preferred_element_type=F32)
                    for b in range(nseg)]
            for b in reversed(range(nseg)):
                sl = slice(b * seg, (b + 1) * seg)
                w = jnp.exp(base[:, sl] + (sums[b] + tail))
                if diagonal:
                    w = jnp.where(past[:, sl], w, 0.0)
                acc = acc + jnp.dot(w.astype(BF16), v[sl], preferred_element_type=F32)
                tail = tail + (sums[b][:, 0:1] + log_1m[:, b * seg:b * seg + 1])
            out.append((tail, acc))
        return tuple(out)

    init = tuple((jnp.zeros((t, 1), F32), jnp.zeros((t, LANES), F32)) for _ in range(2))
    carry = step(i, init, True)

    def live(state):
        jj, carry = state
        return (jj < i) & (jnp.maximum(jnp.max(carry[0][0]), jnp.max(carry[1][0])) > SB_DEAD_LOG)

    def walk(state):
        jj, carry = state
        return jj + 1, step(i - 1 - jj, carry, False)

    _, carry = lax.while_loop(live, walk, (jnp.int32(0), carry))
    o_ref[...] = jnp.where(lane < HEAD_DIM, carry[0][1], carry[1][1]).astype(o_ref.dtype)


def _sb_attention(qkv, batch):
    n = qkv.shape[0]
    seq = n // batch
    nq = seq // SB_T
    pairs = H_SB // 2
    return pl.pallas_call(
        _sb_kernel,
        grid=(batch, pairs, nq),
        in_specs=_attention_specs(seq, nq, SB_T, *_head_cols(MAIN_SB, MAIN_W)),
        out_specs=pl.BlockSpec((SB_T, LANES), lambda b, p, i: (b * nq + i, p)),
        out_shape=jax.ShapeDtypeStruct((n, pairs * LANES), BF16),
        compiler_params=_params(("parallel", "parallel", "arbitrary"), 56),
        name="sb_attention",
    )(qkv, qkv, qkv)


def _fox_kernel(q_ref, k_ref, v_ref, aq_ref, ak_ref, o_ref):
    tq, tk = FOX_TQ, FOX_TK
    i = pl.program_id(2)
    q = q_ref[...]
    lane = lax.broadcasted_iota(jnp.int32, q.shape, 1)
    own = [lane < HEAD_DIM, lane >= HEAD_DIM]
    qh = [jnp.where(own[h], q, aq_ref[...]) for h in range(2)]
    klane = lax.broadcasted_iota(jnp.int32, (tk, LANES), 1)
    kown = [klane < HEAD_DIM, klane >= HEAD_DIM]
    row = lax.broadcasted_iota(jnp.int32, (tq, tk), 0)
    col = lax.broadcasted_iota(jnp.int32, (tq, tk), 1)

    def step(j, carry, diagonal):
        start = pl.multiple_of(j * tk, tk)
        k = k_ref[pl.ds(start, tk), :]
        v = v_ref[pl.ds(start, tk), :]
        ak = ak_ref[pl.ds(start, tk), :]
        out = []
        for h in range(2):
            s = lax.dot_general(qh[h], jnp.where(kown[h], k, ak), NT, preferred_element_type=F32)
            if diagonal:
                s = jnp.where(col + j * tk <= row + i * tq, s, NEG)
            m, acc = carry[h]
            m_new = jnp.maximum(m, jnp.max(s, axis=1, keepdims=True))
            p = jnp.exp(s - m_new).astype(BF16)
            vh = jnp.where(kown[h], v, jnp.ones_like(v))
            acc = jnp.exp(m - m_new) * acc + jnp.dot(p, vh, preferred_element_type=F32)
            out.append((m_new, acc))
        return tuple(out)

    carry = tuple((jnp.full((tq, 1), NEG, F32), jnp.zeros((tq, LANES), F32)) for _ in range(2))
    n_full = (i * tq) // tk
    carry = lax.fori_loop(0, n_full, lambda j, c: step(j, c, False), carry)
    for d in range(max(tq // tk, 1)):
        carry = step(n_full + d, carry, True)
    o = []
    for h in range(2):
        acc = carry[h][1]
        o.append(acc / acc[:, HEAD_DIM * (1 - h):HEAD_DIM * (1 - h) + 1])
    o_ref[...] = jnp.where(own[0], o[0], o[1]).astype(o_ref.dtype)


def _fox_attention(qkv, aug_q, aug_k, batch):
    n = qkv.shape[0]
    seq = n // batch
    nq = seq // FOX_TQ
    pairs = H_FOX // 2
    specs = _attention_specs(seq, nq, FOX_TQ, *_head_cols(MAIN_FOX, MAIN_W))
    specs += [pl.BlockSpec((FOX_TQ, LANES), lambda b, p, i: (b * nq + i, p)),
              pl.BlockSpec((seq, LANES), lambda b, p, i: (b, p))]
    return pl.pallas_call(
        _fox_kernel,
        grid=(batch, pairs, nq),
        in_specs=specs,
        out_specs=pl.BlockSpec((FOX_TQ, LANES), lambda b, p, i: (b * nq + i, p)),
        out_shape=jax.ShapeDtypeStruct((n, pairs * LANES), BF16),
        compiler_params=_params(("parallel", "parallel", "arbitrary"), 48),
        name="fox_attention",
    )(qkv, qkv, qkv, aug_q, aug_k)


def _moba_kernel(q_ref, k_ref, v_ref, onehot_ref, bias_ref, o_ref, kmean_ref):
    t, mb = MOBA_T, MOBA_BLOCK
    nb = t // mb
    seq = k_ref.shape[0]
    i = pl.program_id(2)

    @pl.when(i == 0)
    def _():
        blk = lax.broadcasted_iota(jnp.int32, (LANES, seq), 0)
        pos = lax.broadcasted_iota(jnp.int32, (LANES, seq), 1)
        member = jnp.where((pos >= blk * mb) & (pos < (blk + 1) * mb), 1.0, 0.0).astype(BF16)
        mean = jnp.dot(member, k_ref[...], preferred_element_type=F32) * (1.0 / mb)
        hi = mean.astype(BF16)
        kmean_ref[0] = hi
        kmean_ref[1] = (mean - hi.astype(F32)).astype(BF16)

    q = q_ref[...]
    lane = lax.broadcasted_iota(jnp.int32, (t, LANES), 1)
    own_half = [lane < HEAD_DIM, lane >= HEAD_DIM]
    zero = jnp.zeros_like(q)
    blk_id = lane.astype(F32)
    own_blk = ((i * t + lax.broadcasted_iota(jnp.int32, (t, LANES), 0)) // mb).astype(F32)
    fully_past = blk_id < own_blk

    qh = []
    for h in range(2):
        q_only = jnp.where(own_half[h], q, zero)
        gate = (lax.dot_general(q_only, kmean_ref[0], NT, preferred_element_type=F32)
                + lax.dot_general(q_only, kmean_ref[1], NT, preferred_element_type=F32))
        gate = jnp.where(fully_past, gate, -jnp.inf)
        picked = jnp.zeros((t, LANES), jnp.bool_)
        for _ in range(MOBA_TOPK):
            best = jnp.max(gate, axis=1, keepdims=True)
            first = jnp.min(jnp.where(gate == best, blk_id, float(LANES)), axis=1, keepdims=True)
            pick = blk_id == first
            picked = picked | pick
            gate = jnp.where(pick, -jnp.inf, gate)
        allowed = (picked & fully_past) | (blk_id == own_blk)
        penalty = jnp.where(allowed, 0.0, NEG)
        if h == 0:
            penalty = pltpu.roll(penalty, HEAD_DIM, axis=1)
        qh.append(jnp.where(own_half[h], q, penalty.astype(BF16)))

    def step(j, carry):
        start = pl.multiple_of(j * t, t)
        k = k_ref[pl.ds(start, t), :]
        v = v_ref[pl.ds(start, t), :]
        onehot = onehot_ref[pl.ds(start, t), :]
        out = []
        for h in range(2):
            s = lax.dot_general(qh[h], jnp.where(own_half[h], k, onehot), NT, preferred_element_type=F32)
            rows = []
            for a in range(nb):
                tiles = [bias_ref[jnp.clip((i - j) * nb + a - b, 0, MOBA_NT), h] for b in range(nb)]
                rows.append(jnp.concatenate(tiles, axis=1))
            s = s + jnp.concatenate(rows, axis=0)
            m, acc = carry[h]
            m_new = jnp.maximum(m, jnp.max(s, axis=1, keepdims=True))
            p = jnp.exp(s - m_new).astype(BF16)
            vh = jnp.where(own_half[h], v, jnp.ones_like(v))
            acc = jnp.exp(m - m_new) * acc + jnp.dot(p, vh, preferred_element_type=F32)
            out.append((m_new, acc))
        return tuple(out)

    carry = tuple((jnp.full((t, 1), NEG, F32), jnp.zeros((t, LANES), F32)) for _ in range(2))
    carry = lax.fori_loop(0, i + 1, step, carry)
    o = []
    for h in range(2):
        acc = carry[h][1]
        o.append(acc / acc[:, HEAD_DIM * (1 - h):HEAD_DIM * (1 - h) + 1])
    o_ref[...] = jnp.where(own_half[0], o[0], o[1]).astype(o_ref.dtype)


def _moba_attention(qkv, bias_tiles, batch):
    n = qkv.shape[0]
    seq = n // batch
    assert seq % MOBA_T == 0 and seq // MOBA_BLOCK <= HEAD_DIM
    nq = seq // MOBA_T
    pairs = H_MOBA // 2
    blk = np.arange(seq)[:, None] // MOBA_BLOCK
    onehot = jnp.asarray(blk == (np.arange(LANES)[None, :] % HEAD_DIM), BF16)
    specs = _attention_specs(seq, nq, MOBA_T, *_head_cols(MAIN_MOBA, MAIN_W))
    specs.append(pl.BlockSpec((seq, LANES), lambda b, p, i: (0, 0)))
    specs.append(pl.BlockSpec((None, MOBA_NT + 1, 2, MOBA_BLOCK, MOBA_BLOCK), lambda b, p, i: (p, 0, 0, 0, 0)))
    return pl.pallas_call(
        _moba_kernel,
        grid=(batch, pairs, nq),
        in_specs=specs,
        out_specs=pl.BlockSpec((MOBA_T, LANES), lambda b, p, i: (b * nq + i, p)),
        out_shape=jax.ShapeDtypeStruct((n, pairs * LANES), BF16),
        scratch_shapes=[pltpu.VMEM((2, LANES, LANES), BF16)],
        compiler_params=_params(("arbitrary", "arbitrary", "arbitrary"), 56),
        name="moba_attention",
    )(qkv, qkv, qkv, onehot, bias_tiles)


def _moba_bias_tiles(rel_bias):
    r = jnp.arange(MOBA_BLOCK)
    dist = (jnp.arange(MOBA_NT + 1)[:, None, None] * MOBA_BLOCK + r[None, :, None] - r[None, None, :])
    tiles = _bias_lookup(rel_bias[:, H_DIL:], _t5_bucket(jnp.maximum(dist, 0)))
    tiles = jnp.where((dist >= 0)[..., None], tiles, NEG)
    tiles = tiles.transpose(3, 0, 1, 2).reshape(H_MOBA // 2, 2, MOBA_NT + 1, MOBA_BLOCK, MOBA_BLOCK)
    return tiles.transpose(0, 2, 1, 3, 4).astype(F32)


def _check_moba_saturation():
    n = np.float32(MOBA_NT * MOBA_BLOCK - (MOBA_BLOCK - 1))
    exact = N_BUCKETS // 2
    large = exact + int(np.log(n / np.float32(exact)) / math.log(MAX_DIST / exact) * (N_BUCKETS - exact))
    assert large >= N_BUCKETS - 1, "MOBA_NT too small for the bias bucket table"


_check_moba_saturation()


def _dilated_kernel(q_ref, k_ref, v_ref, bias_ref, o_ref, lse_ref):
    w = DIL_W
    nb = q_ref.shape[0] // w
    row = lax.broadcasted_iota(jnp.int32, (w, w), 0)
    col = lax.broadcasted_iota(jnp.int32, (w, w), 1)
    in_cur = col <= row
    in_prev = col >= row

    def body(n, _):
        cur = pl.multiple_of(n * w, w)
        prev = pl.multiple_of(jnp.maximum(n - 1, 0) * w, w)
        qh, lane = _head_pair(q_ref[pl.ds(cur, w), :])
        k_cur, v_cur = k_ref[pl.ds(cur, w), :], v_ref[pl.ds(cur, w), :]
        k_prev, v_prev = k_ref[pl.ds(prev, w), :], v_ref[pl.ds(prev, w), :]
        has_prev = n > 0
        outs, lses = [], []
        for h in range(2):
            z_prev = lax.dot_general(qh[h], k_prev, NT, preferred_element_type=F32) + bias_ref[h, :, 0:w]
            z_cur = lax.dot_general(qh[h], k_cur, NT, preferred_element_type=F32) + bias_ref[h, :, w:2 * w]
            z_prev = jnp.where(in_prev & has_prev, z_prev, NEG)
            z_cur = jnp.where(in_cur, z_cur, NEG)
            m = jnp.maximum(jnp.max(z_prev, axis=1, keepdims=True), jnp.max(z_cur, axis=1, keepdims=True))
            p_prev = jnp.exp(z_prev - m)
            p_cur = jnp.exp(z_cur - m)
            l = jnp.sum(p_prev, axis=1, keepdims=True) + jnp.sum(p_cur, axis=1, keepdims=True)
            o = (jnp.dot(p_prev.astype(BF16), v_prev, preferred_element_type=F32)
                 + jnp.dot(p_cur.astype(BF16), v_cur, preferred_element_type=F32))
            outs.append(o / l)
            lses.append(m + jnp.log(l))
        o_ref[pl.ds(cur, w), :] = jnp.where(lane < HEAD_DIM, outs[0], outs[1])
        lse_ref[pl.ds(cur, w), :] = jnp.where(lane < HEAD_DIM, lses[0], lses[1])
        return 0

    lax.fori_loop(0, nb, body, 0)


def _dilated_group(qkv, head_off, width, bias, batch, group):
    window, dil = DIL_GROUPS[group]
    assert window // dil == DIL_W
    n = qkv.shape[0]
    sub = n // batch // dil
    assert sub % DIL_W == 0
    pairs = H_DIL_PER_GROUP // 2
    q_col, k_col, v_col = _head_cols(head_off, width)

    def spec(col0):
        return pl.BlockSpec((sub, LANES), lambda b, p, r: (b * dil + r, col0 + p))

    out_spec = pl.BlockSpec((sub, LANES), lambda b, p, r: (b * dil + r, p))
    out_shape = jax.ShapeDtypeStruct((n, pairs * LANES), F32)
    return pl.pallas_call(
        _dilated_kernel,
        grid=(batch, pairs, dil),
        in_specs=[spec(q_col), spec(k_col), spec(v_col),
                  pl.BlockSpec((None, 2, DIL_W, 2 * DIL_W), lambda b, p, r: (group * pairs + p, 0, 0, 0))],
        out_specs=[out_spec, out_spec],
        out_shape=[out_shape, out_shape],
        compiler_params=_params(("parallel", "parallel", "parallel"), 48),
        name=f"dilated_attention_g{group}",
    )(qkv, qkv, qkv, bias)


def _by_residue(a, batch, dil):
    n, c = a.shape
    return a.reshape(batch, n // batch // dil, dil, c).transpose(0, 2, 1, 3).reshape(n, c)


def _by_position(a, batch, dil):
    n, c = a.shape
    return a.reshape(batch, dil, n // batch // dil, c).transpose(0, 2, 1, 3).reshape(n, c)


def _dilated_bias_tiles(rel_bias):
    steps = jnp.arange(DIL_W)[:, None] + DIL_W - jnp.arange(2 * DIL_W)[None, :]
    tiles = []
    for g, (_, dil) in enumerate(DIL_GROUPS):
        tab = rel_bias[:, g * H_DIL_PER_GROUP:(g + 1) * H_DIL_PER_GROUP]
        tiles.append(_bias_lookup(tab, _t5_bucket(jnp.maximum(steps, 0) * dil)).transpose(2, 0, 1))
    tiles = jnp.stack(tiles, 0).astype(F32)
    return tiles.reshape(len(DIL_GROUPS) * H_DIL_PER_GROUP // 2, 2, DIL_W, 2 * DIL_W)


def _dilated_combine_kernel(o0, o1, o2, l0, l1, l2, out_ref):
    lse = [l0[...], l1[...], l2[...]]
    m = jnp.maximum(jnp.maximum(lse[0], lse[1]), lse[2])
    e = [jnp.exp(x - m) for x in lse]
    den = e[0] + e[1] + e[2]
    out = (e[0] / den) * o0[...] + (e[1] / den) * o1[...] + (e[2] / den) * o2[...]
    out_ref[...] = out.astype(out_ref.dtype)


def _dilated_combine(os, lses, tm=1024):
    n, c = os[0].shape
    spec = pl.BlockSpec((tm, c), lambda i: (i, 0))
    return pl.pallas_call(
        _dilated_combine_kernel,
        grid=(n // tm,),
        in_specs=[spec] * 6,
        out_specs=spec,
        out_shape=jax.ShapeDtypeStruct((n, c), BF16),
        compiler_params=_params(("parallel",), 32),
        name="dilated_combine",
    )(*os, *lses)


def _merge_kernel(x_ref, o_sb, o_dil, o_moba, o_fox, wg0, wg1, wg2, wg3, bg_ref,
                  w_sb, w_dil, w_moba, w_fox, out_ref):
    x = x_ref[...]
    acc = None
    branches = ((o_sb, w_sb, wg0), (o_dil, w_dil, wg1), (o_moba, w_moba, wg2), (o_fox, w_fox, wg3))
    for b, (o_ref, w_ref, wg_ref) in enumerate(branches):
        gate = jax.nn.sigmoid(jnp.dot(x, wg_ref[...], preferred_element_type=F32) + bg_ref[b])
        term = gate * jnp.dot(o_ref[...], w_ref[...], preferred_element_type=F32)
        acc = term if acc is None else acc + term
    out_ref[...] = acc.astype(out_ref.dtype)


def _gated_merge(x_bf, outs, w_gate, b_gate, w_branch, tm=512, tn=512):
    n, d = x_bf.shape
    nj = d // tn
    row = lambda width: pl.BlockSpec((tm, width), lambda j, i: (i, 0))
    in_specs = [row(d)] + [row(o.shape[1]) for o in outs]
    in_specs += [pl.BlockSpec((d, tn), functools.partial(lambda j, i, b: (0, b * nj + j), b=b)) for b in range(N_BRANCH)]
    in_specs.append(pl.BlockSpec((N_BRANCH, 1, tn), lambda j, i: (0, 0, j)))
    in_specs += [pl.BlockSpec((w.shape[0], tn), lambda j, i: (0, j)) for w in w_branch]
    return pl.pallas_call(
        _merge_kernel,
        grid=(nj, n // tm),
        in_specs=in_specs,
        out_specs=pl.BlockSpec((tm, tn), lambda j, i: (i, j)),
        out_shape=jax.ShapeDtypeStruct((n, d), BF16),
        compiler_params=_params(("parallel", "parallel"), 48),
        name="gated_merge",
    )(x_bf, *outs, w_gate, w_gate, w_gate, w_gate, b_gate, *w_branch)


def _layer_norm_store(z, g_ref, b_ref, xo_ref, xb_ref):
    mu = jnp.mean(z, axis=1, keepdims=True)
    zc = z - mu
    var = jnp.mean(zc * zc, axis=1, keepdims=True)
    out = zc * lax.rsqrt(var + LN_EPS) * g_ref[...] + b_ref[...]
    xo_ref[...] = out
    xb_ref[...] = out.astype(BF16)


def _out_ln_kernel(m_ref, w_ref, x_ref, g_ref, b_ref, xo_ref, xb_ref):
    y = jnp.dot(m_ref[...], w_ref[...], preferred_element_type=F32)
    _layer_norm_store(DN_ALPHA * x_ref[...] + y, g_ref, b_ref, xo_ref, xb_ref)


def _combine_ln_kernel(y_ref, gate_ref, x_ref, g_ref, b_ref, xo_ref, xb_ref):
    gate = gate_ref[...]
    z = DN_ALPHA * x_ref[...]
    for k in range(TOP_K):
        z = z + gate[:, k:k + 1] * y_ref[k].astype(F32)
    _layer_norm_store(z, g_ref, b_ref, xo_ref, xb_ref)


def _ln_call(kernel, lead_specs, lead_args, x, g, b, tm, name):
    n, d = x.shape
    row = pl.BlockSpec((tm, d), lambda i: (i, 0))
    vec = pl.BlockSpec((1, d), lambda i: (0, 0))
    return pl.pallas_call(
        kernel,
        grid=(n // tm,),
        in_specs=lead_specs + [row, vec, vec],
        out_specs=[row, row],
        out_shape=[jax.ShapeDtypeStruct((n, d), F32), jax.ShapeDtypeStruct((n, d), BF16)],
        compiler_params=_params(("parallel",), 48),
        name=name,
    )(*lead_args, x, g, b)


def _out_proj_ln(merged, w_out, x, g, b, tm=256):
    d = x.shape[1]
    specs = [pl.BlockSpec((tm, d), lambda i: (i, 0)), pl.BlockSpec((d, d), lambda i: (0, 0))]
    return _ln_call(_out_ln_kernel, specs, (merged, w_out), x, g, b, tm, "out_proj_ln")


def _combine_ln(yk, gate, x, g, b, tm=256):
    d = x.shape[1]
    specs = [pl.BlockSpec((TOP_K, tm, d), lambda i: (0, i, 0)), pl.BlockSpec((tm, LANES), lambda i: (i, 0))]
    return _ln_call(_combine_ln_kernel, specs, (yk, gate), x, g, b, tm, "moe_combine_ln")


def _router_kernel(x_ref, w_ref, b_ref, idx_ref, gate_ref, rank_ref, count_ref, seen_ref):
    @pl.when(pl.program_id(0) == 0)
    def _():
        seen_ref[...] = jnp.zeros_like(seen_ref)

    tm = x_ref.shape[0]
    logits = jnp.dot(x_ref[...], w_ref[...], preferred_element_type=F32, precision=HIGHEST) + b_ref[...]
    col = lax.broadcasted_iota(jnp.int32, logits.shape, 1)
    colf = col.astype(F32)
    logits = jnp.where(col < N_EXPERTS, logits, -jnp.inf)
    vals, idxs, hits = [], [], []
    for _ in range(TOP_K):
        best = jnp.max(logits, axis=1, keepdims=True)
        first = jnp.min(jnp.where(logits == best, colf, float(LANES)), axis=1, keepdims=True)
        hit = colf == first
        logits = jnp.where(hit, -jnp.inf, logits)
        vals.append(best)
        idxs.append(first)
        hits.append(jnp.where(hit, 1.0, 0.0))
    es = [jnp.exp(v - vals[0]) for v in vals]
    den = es[0] + es[1] + es[2] + es[3]

    chosen = hits[0] + hits[1] + hits[2] + hits[3]
    row = lax.broadcasted_iota(jnp.int32, (tm, tm), 0)
    before = jnp.where(lax.broadcasted_iota(jnp.int32, (tm, tm), 1) < row, 1.0, 0.0).astype(BF16)
    earlier = jnp.dot(before, chosen.astype(BF16), preferred_element_type=F32) + seen_ref[...]
    seen = seen_ref[...] + jnp.sum(chosen, axis=0, keepdims=True)
    seen_ref[...] = seen
    count_ref[...] = seen

    idx_out = jnp.zeros(logits.shape, F32)
    gate_out = jnp.zeros(logits.shape, F32)
    rank_out = jnp.zeros(logits.shape, F32)
    for r in range(TOP_K):
        idx_out = jnp.where(col == r, idxs[r], idx_out)
        gate_out = jnp.where(col == r, es[r] / den, gate_out)
        rank_out = jnp.where(col == r, jnp.sum(hits[r] * earlier, axis=1, keepdims=True), rank_out)
    idx_ref[...] = idx_out.astype(jnp.int32)
    gate_ref[...] = gate_out
    rank_ref[...] = rank_out.astype(jnp.int32)


def _router(x, w_r, b_r, tm=512):
    n, d = x.shape
    out = pl.BlockSpec((tm, LANES), lambda i: (i, 0))
    return pl.pallas_call(
        _router_kernel,
        grid=(n // tm,),
        in_specs=[pl.BlockSpec((tm, d), lambda i: (i, 0)),
                  pl.BlockSpec((d, LANES), lambda i: (0, 0)),
                  pl.BlockSpec((1, LANES), lambda i: (0, 0))],
        out_specs=[out, out, out, pl.BlockSpec((1, LANES), lambda i: (0, 0))],
        out_shape=[jax.ShapeDtypeStruct((n, LANES), jnp.int32), jax.ShapeDtypeStruct((n, LANES), F32),
                   jax.ShapeDtypeStruct((n, LANES), jnp.int32), jax.ShapeDtypeStruct((1, LANES), F32)],
        scratch_shapes=[pltpu.VMEM((1, LANES), F32)],
        compiler_params=_params(("arbitrary",), 32),
        name="moe_router",
    )(x, w_r, b_r)


def _expert_kernel(blk_exp_ref, xs_ref, wu_ref, bu_ref, wd_ref, bd_ref, y_ref):
    del blk_exp_ref
    h = jnp.dot(xs_ref[...], wu_ref[...], preferred_element_type=F32) + bu_ref[...]
    glu = jnp.minimum(h[:, :D_FF], SWIGLU_LIMIT)
    lin = jnp.clip(h[:, D_FF:], -SWIGLU_LIMIT, SWIGLU_LIMIT)
    act = glu * jax.nn.sigmoid(SWIGLU_ALPHA * glu) * (lin + 1.0)
    y = jnp.dot(act.astype(BF16), wd_ref[...], preferred_element_type=F32) + bd_ref[...]
    y_ref[...] = y.astype(y_ref.dtype)


def _expert_ffn(xs, blk_exp, w_up, b_up, w_down, b_down):
    p, d = xs.shape
    grid_spec = pltpu.PrefetchScalarGridSpec(
        num_scalar_prefetch=1,
        grid=(p // MOE_ROWS,),
        in_specs=[pl.BlockSpec((MOE_ROWS, d), lambda i, e: (i, 0)),
                  pl.BlockSpec((None, d, 2 * D_FF), lambda i, e: (e[i], 0, 0)),
                  pl.BlockSpec((None, 1, 2 * D_FF), lambda i, e: (e[i], 0, 0)),
                  pl.BlockSpec((None, D_FF, d), lambda i, e: (e[i], 0, 0)),
                  pl.BlockSpec((None, 1, d), lambda i, e: (e[i], 0, 0))],
        out_specs=pl.BlockSpec((MOE_ROWS, d), lambda i, e: (i, 0)),
    )
    return pl.pallas_call(
        _expert_kernel,
        grid_spec=grid_spec,
        out_shape=jax.ShapeDtypeStruct((p, d), BF16),
        compiler_params=_params(("arbitrary",), 48),
        name="expert_ffn",
    )(blk_exp, xs, w_up, b_up, w_down, b_down)


def _moe(x, x_bf, w_r, b_r, w_up, b_up, w_down, b_down):
    n, d = x.shape
    nk = n * TOP_K
    idx, gate, rank, count = _router(x, w_r, b_r)
    counts = count[0, :N_EXPERTS].astype(jnp.int32)
    start = jnp.cumsum(counts) - counts
    padded = (counts + MOE_ROWS - 1) // MOE_ROWS * MOE_ROWS
    pend = jnp.cumsum(padded)
    pstart = pend - padded
    experts = jnp.arange(N_EXPERTS, dtype=jnp.int32)
    chosen = idx[:, :TOP_K, None] == experts
    slot = jnp.sum(jnp.where(chosen, pstart, 0), axis=2) + rank[:, :TOP_K]
    n_blocks = -(-(nk + N_EXPERTS * (MOE_ROWS - 1)) // MOE_ROWS)
    first_row = jnp.arange(n_blocks, dtype=jnp.int32) * MOE_ROWS
    blk_exp = jnp.minimum(jnp.sum(pend[None, :] <= first_row[:, None], axis=1), N_EXPERTS - 1).astype(jnp.int32)
    order = jnp.argsort(slot.reshape(nk)).astype(jnp.int32)
    blk_is = blk_exp[:, None] == experts
    blk_start = jnp.sum(jnp.where(blk_is, start, 0), axis=1)
    blk_pstart = jnp.sum(jnp.where(blk_is, pstart, 0), axis=1)
    blk_count = jnp.sum(jnp.where(blk_is, counts, 0), axis=1)
    within = first_row[:, None] + jnp.arange(MOE_ROWS, dtype=jnp.int32)[None, :] - blk_pstart[:, None]
    pair = jnp.clip(blk_start[:, None] + within, 0, nk - 1)
    slot_tok = jnp.where(within < blk_count[:, None], order[pair] // TOP_K, 0).reshape(n_blocks * MOE_ROWS)
    xs = x_bf[slot_tok]
    y = _expert_ffn(xs, blk_exp, w_up, b_up, w_down, b_down)
    return y[slot.T.reshape(nk)].reshape(TOP_K, n, d), gate


def kernel(x, rel_bias, w_in, b_forget, b_gate, w_br_sb, w_br_dil, w_br_moba, w_br_fox, w_out,
           ln1_g, ln1_b, w_router, b_router, w_up, b_up, w_down, b_down, ln2_g, ln2_b):
    batch, seq, d = x.shape
    n = batch * seq
    depth = w_in.shape[0]

    w_in_bf = lax.optimization_barrier(w_in[:, :, :C_QKV].astype(BF16))

    def qkv_weight(heads):
        cols = []
        for sec, scale in ((0, HEAD_DIM ** -0.5), (1, 1.0), (2, 1.0)):
            for lo, cnt in heads:
                a = sec * W_MIX + lo * HEAD_DIM
                cols.append(w_in_bf[:, :, a:a + cnt * HEAD_DIM] * jnp.asarray(scale, BF16))
        return jnp.concatenate(cols, axis=2)

    w_main = qkv_weight(((SB_OFF, H_SB), (DIL_OFF, H_DIL_PER_GROUP), (MOBA_OFF, H_MOBA), (FOX_OFF, H_FOX)))
    w_dil = [qkv_weight(((DIL_OFF + g * H_DIL_PER_GROUP, H_DIL_PER_GROUP),)) for g in range(1, len(DIL_GROUPS))]
    w_f = jnp.pad(w_in[:, :, C_QKV:C_QKV + H_FOX], ((0, 0), (0, 0), (0, LANES - H_FOX)))
    b_f = jnp.pad(b_forget, ((0, 0), (0, LANES - H_FOX))).reshape(depth, 1, LANES)
    w_gate = _gate_weights(w_in)
    b_gate3 = b_gate.reshape(depth, N_BRANCH, 1, d)
    w_branch = [w.astype(BF16) for w in (w_br_sb, w_br_dil, w_br_moba, w_br_fox)]
    w_out_bf = w_out.astype(BF16)
    w_r = jnp.pad(w_router, ((0, 0), (0, 0), (0, LANES - N_EXPERTS)))
    b_r = jnp.pad(b_router, ((0, 0), (0, LANES - N_EXPERTS))).reshape(depth, 1, LANES)
    w_up_bf = w_up.astype(BF16)
    w_down_bf = w_down.astype(BF16)
    b_up3 = b_up.reshape(depth, N_EXPERTS, 1, 2 * D_FF)
    b_down3 = b_down.reshape(depth, N_EXPERTS, 1, d)
    moba_bias = _moba_bias_tiles(rel_bias)
    dil_bias = _dilated_bias_tiles(rel_bias)

    xf = x.reshape(n, d)
    xb = xf.astype(BF16)
    for l in range(depth):
        qkv = _matmul(xb, w_main[l], 1024, 768, BF16, "qkv_proj")
        aug_q, aug_k = _decay(xf, w_f[l], b_f[l], batch)
        o_sb = _sb_attention(qkv, batch)
        os, lses = [], []
        o, lse = _dilated_group(qkv, MAIN_DIL, MAIN_W, dil_bias, batch, 0)
        os.append(o)
        lses.append(lse)
        for g in range(1, len(DIL_GROUPS)):
            dil = DIL_GROUPS[g][1]
            qkv_g = _matmul(_by_residue(xb, batch, dil), w_dil[g - 1][l], 1024, 3 * DIL_GROUP_W, BF16,
                            f"qkv_proj_g{g}")
            o, lse = _dilated_group(qkv_g, 0, DIL_GROUP_W, dil_bias, batch, g)
            os.append(_by_position(o, batch, dil))
            lses.append(_by_position(lse, batch, dil))
        o_dil = _dilated_combine(os, lses)
        o_moba = _moba_attention(qkv, moba_bias, batch)
        o_fox = _fox_attention(qkv, aug_q, aug_k, batch)
        merged = _gated_merge(xb, (o_sb, o_dil, o_moba, o_fox), w_gate[l], b_gate3[l],
                              [w[l] for w in w_branch])
        xf, xb = _out_proj_ln(merged, w_out_bf[l], xf, ln1_g[l].reshape(1, d), ln1_b[l].reshape(1, d))
        yk, gate = _moe(xf, xb, w_r[l], b_r[l], w_up_bf[l], b_up3[l], w_down_bf[l], b_down3[l])
        xf, xb = _combine_ln(yk, gate, xf, ln2_g[l].reshape(1, d), ln2_b[l].reshape(1, d))
    return xf.reshape(batch, seq, d)
```

```python
import functools
import math

import numpy as np
import jax
import jax.numpy as jnp
from jax import lax
from jax.experimental import pallas as pl
from jax.experimental.pallas import tpu as pltpu

D_MODEL = 2048
DEPTH = 4
HEAD_DIM = 64
H_SB = 8
H_DIL_PER_GROUP = 4
DIL_GROUPS = ((128, 1), (512, 4), (2048, 16))
H_DIL = H_DIL_PER_GROUP * len(DIL_GROUPS)
H_MOBA = 8
H_FOX = 8
H_ALL = H_SB + H_DIL + H_MOBA + H_FOX
SB_OFF = 0
DIL_OFF = SB_OFF + H_SB
MOBA_OFF = DIL_OFF + H_DIL
FOX_OFF = MOBA_OFF + H_MOBA
W_MIX = H_ALL * HEAD_DIM
N_BRANCH = 4
C_QKV = 3 * W_MIX
MOBA_BLOCK = 256
MOBA_TOPK = 3
N_BUCKETS = 32
MAX_DIST = 2048
N_EXPERTS = 32
TOP_K = 4
D_FF = 768
SWIGLU_ALPHA = 1.702
SWIGLU_LIMIT = 7.0
LN_EPS = 1e-5
DN_ALPHA = (2 * DEPTH) ** 0.25

F32 = jnp.float32
BF16 = jnp.bfloat16
LANES = 128
NEG = -1e30
HIGHEST = lax.Precision.HIGHEST
MIB = 1024 * 1024
NT = (((1,), (1,)), ((), ()))
SIGN_BIT = np.uint32(0x80000000)

MAIN_SB = 0
MAIN_DIL = MAIN_SB + H_SB
MAIN_MOBA = MAIN_DIL + H_DIL_PER_GROUP
MAIN_FOX = MAIN_MOBA + H_MOBA
MAIN_W = (MAIN_FOX + H_FOX) * HEAD_DIM
DIL_GROUP_W = H_DIL_PER_GROUP * HEAD_DIM
DIL_W = 128
MOE_ROWS = 256
SB_T = 1024
SB_SEG = 256
SB_DEAD_LOG = -110.0
FOX_TQ = 1024
FOX_TK = 1024
MOBA_T = 1024
MOBA_NT = 8
AUG_PARTS = 3


def _params(sem, vmem_mib):
    return pltpu.CompilerParams(dimension_semantics=sem, vmem_limit_bytes=vmem_mib * MIB)


def _t5_bucket(n):
    exact = N_BUCKETS // 2
    nf = jnp.maximum(n, 1).astype(F32)
    large = exact + (jnp.log(nf / exact) / math.log(MAX_DIST / exact) * (N_BUCKETS - exact)).astype(jnp.int32)
    large = jnp.minimum(large, N_BUCKETS - 1)
    return jnp.where(n < exact, n, large)


def _bias_lookup(table, bucket):
    onehot = (bucket[..., None] == jnp.arange(N_BUCKETS)).astype(F32)
    return jnp.einsum('...k,kh->...h', onehot, table.astype(F32), precision=HIGHEST)


def _head_pair(q):
    lane = lax.broadcasted_iota(jnp.int32, q.shape, 1)
    zero = jnp.zeros_like(q)
    return [jnp.where(lane < HEAD_DIM, q, zero), jnp.where(lane >= HEAD_DIM, q, zero)], lane


def _attention_specs(seq, nq, tq, q_col, k_col, v_col):
    return [pl.BlockSpec((tq, LANES), lambda b, p, i: (b * nq + i, q_col + p)),
            pl.BlockSpec((seq, LANES), lambda b, p, i: (b, k_col + p)),
            pl.BlockSpec((seq, LANES), lambda b, p, i: (b, v_col + p))]


def _head_cols(off, width):
    q_col = off * HEAD_DIM // LANES
    return q_col, q_col + width // LANES, q_col + 2 * width // LANES


def _matmul_kernel(a_ref, w_ref, o_ref):
    o_ref[...] = jnp.dot(a_ref[...], w_ref[...], preferred_element_type=F32).astype(o_ref.dtype)


def _matmul(a, w, layer, col0, n, tm, tn, out_dtype, name):
    m, k = a.shape
    first = col0 // tn
    return pl.pallas_call(
        _matmul_kernel,
        grid=(n // tn, m // tm),
        in_specs=[pl.BlockSpec((tm, k), lambda j, i: (i, 0)),
                  pl.BlockSpec((None, k, tn), lambda j, i: (layer, 0, first + j))],
        out_specs=pl.BlockSpec((tm, tn), lambda j, i: (i, j)),
        out_shape=jax.ShapeDtypeStruct((m, n), out_dtype),
        compiler_params=_params(("parallel", "parallel"), 40),
        name=name,
    )(a, w)


QKV_TILE = 256
MAIN_HEADS = ((SB_OFF, H_SB), (DIL_OFF, H_DIL_PER_GROUP), (MOBA_OFF, H_MOBA), (FOX_OFF, H_FOX))


def _qkv_column_plan():
    groups = [MAIN_HEADS] + [((DIL_OFF + g * H_DIL_PER_GROUP, H_DIL_PER_GROUP),) for g in range(1, len(DIL_GROUPS))]
    src, is_q = [], []
    for heads in groups:
        for sec in range(3):
            for lo, cnt in heads:
                start = sec * W_MIX + lo * HEAD_DIM
                assert start % QKV_TILE == 0 and (cnt * HEAD_DIM) % QKV_TILE == 0
                for b in range(cnt * HEAD_DIM // QKV_TILE):
                    src.append(start // QKV_TILE + b)
                    is_q.append(int(sec == 0))
    assert sorted(src) == list(range(C_QKV // QKV_TILE))
    return np.asarray(src, np.int32), np.asarray(is_q, np.int32)


def _qkv_weight_kernel(src_ref, is_q_ref, w_ref, o_ref):
    del src_ref
    scale = jnp.where(is_q_ref[pl.program_id(1)] == 1, HEAD_DIM ** -0.5, 1.0)
    o_ref[...] = (w_ref[...] * scale).astype(BF16)


def _qkv_weights(w_in):
    depth, d, _ = w_in.shape
    src, is_q = _qkv_column_plan()
    grid_spec = pltpu.PrefetchScalarGridSpec(
        num_scalar_prefetch=2,
        grid=(depth, len(src)),
        in_specs=[pl.BlockSpec((None, d, QKV_TILE), lambda l, j, src, is_q: (l, 0, src[j]))],
        out_specs=pl.BlockSpec((None, d, QKV_TILE), lambda l, j, src, is_q: (l, 0, j)),
    )
    return pl.pallas_call(
        _qkv_weight_kernel,
        grid_spec=grid_spec,
        out_shape=jax.ShapeDtypeStruct((depth, d, C_QKV), BF16),
        compiler_params=_params(("parallel", "parallel"), 24),
        name="qkv_weights",
    )(jnp.asarray(src), jnp.asarray(is_q), w_in)


GATE_COL0 = C_QKV + H_FOX
GATE_ALIGNED = GATE_COL0 // LANES * LANES
GATE_TILE = 256


def _gate_weight_kernel(a_ref, b_ref, o_ref):
    shift = GATE_COL0 - GATE_ALIGNED
    both = jnp.concatenate([a_ref[...], b_ref[...]], axis=1)
    o_ref[...] = both[:, shift:shift + GATE_TILE].astype(BF16)


def _gate_weights(w_in):
    depth, d, _ = w_in.shape
    width = N_BRANCH * d
    assert GATE_ALIGNED % GATE_TILE == 0 and width % GATE_TILE == 0
    a0 = GATE_ALIGNED // GATE_TILE
    per = GATE_TILE // LANES
    return pl.pallas_call(
        _gate_weight_kernel,
        grid=(depth, width // GATE_TILE),
        in_specs=[pl.BlockSpec((None, d, GATE_TILE), lambda l, j: (l, 0, a0 + j)),
                  pl.BlockSpec((None, d, LANES), lambda l, j: (l, 0, (a0 + j + 1) * per))],
        out_specs=pl.BlockSpec((None, d, GATE_TILE), lambda l, j: (l, 0, j)),
        out_shape=jax.ShapeDtypeStruct((depth, d, width), BF16),
        compiler_params=_params(("parallel", "parallel"), 24),
        name="gate_weights",
    )(w_in, w_in)


def _aug_constants():
    pairs = H_FOX // 2
    place_k = np.zeros((AUG_PARTS, LANES, pairs * LANES), np.float32)
    place_q = np.zeros((AUG_PARTS, LANES, pairs * LANES), np.float32)
    ones_k = np.zeros((1, pairs * LANES), np.float32)
    ones_q = np.zeros((1, pairs * LANES), np.float32)
    for g in range(H_FOX):
        p, h = divmod(g, 2)
        base = p * LANES + HEAD_DIM * (1 - h)
        for j in range(AUG_PARTS):
            place_k[j, g, base + j] = 1.0
            place_q[j, g, base + AUG_PARTS + j] = 1.0
            ones_k[0, base + AUG_PARTS + j] = 1.0
            ones_q[0, base + j] = 1.0
    return place_k, place_q, ones_k, ones_q


def _split_parts(x):
    parts = []
    for _ in range(AUG_PARTS):
        piece = x.astype(BF16)
        parts.append(piece)
        x = x - piece.astype(F32)
    return parts


def _decay_kernel(x_ref, wf_ref, bf_ref, pk_ref, pq_ref, ok_ref, oq_ref, augq_ref, augk_ref, carry_ref):
    @pl.when(pl.program_id(1) == 0)
    def _():
        carry_ref[...] = jnp.zeros_like(carry_ref)

    tm = x_ref.shape[0]
    f = jnp.dot(x_ref[...], wf_ref[...], preferred_element_type=F32, precision=HIGHEST) + bf_ref[...]
    logf = jnp.minimum(f, 0.0) - jnp.log1p(jnp.exp(-jnp.abs(f)))
    row = lax.broadcasted_iota(jnp.int32, (tm, tm), 0)
    col = lax.broadcasted_iota(jnp.int32, (tm, tm), 1)
    tri = jnp.where(col <= row, 1.0, 0.0).astype(F32)
    c = jnp.dot(tri, logf, preferred_element_type=F32, precision=HIGHEST) + carry_ref[...]
    carry_ref[...] = c[tm - 1:tm, :]
    aug_q = oq_ref[...]
    aug_k = ok_ref[...]
    for j, piece in enumerate(_split_parts(c)):
        aug_q = aug_q + jnp.dot(piece, pq_ref[j], preferred_element_type=F32)
        aug_k = aug_k - jnp.dot(piece, pk_ref[j], preferred_element_type=F32)
    augq_ref[...] = aug_q.astype(BF16)
    augk_ref[...] = aug_k.astype(BF16)


def _decay(x, w_f, b_f, batch, tm=256):
    n, d = x.shape
    nb = n // batch // tm
    width = H_FOX // 2 * LANES
    place_k, place_q, ones_k, ones_q = _aug_constants()
    const = lambda a: pl.BlockSpec(a.shape, lambda b, i: (0,) * a.ndim)
    out = pl.BlockSpec((tm, width), lambda b, i: (b * nb + i, 0))
    return pl.pallas_call(
        _decay_kernel,
        grid=(batch, nb),
        in_specs=[pl.BlockSpec((tm, d), lambda b, i: (b * nb + i, 0)),
                  pl.BlockSpec((d, LANES), lambda b, i: (0, 0)),
                  pl.BlockSpec((1, LANES), lambda b, i: (0, 0)),
                  const(place_k), const(place_q), const(ones_k), const(ones_q)],
        out_specs=[out, out],
        out_shape=[jax.ShapeDtypeStruct((n, width), BF16)] * 2,
        scratch_shapes=[pltpu.VMEM((1, LANES), F32)],
        compiler_params=_params(("arbitrary", "arbitrary"), 24),
        name="fox_decay",
    )(x, w_f, b_f, jnp.asarray(place_k, BF16), jnp.asarray(place_q, BF16), jnp.asarray(ones_k), jnp.asarray(ones_q))


def _sb_kernel(q_ref, k_ref, v_ref, o_ref):
    t, seg = SB_T, SB_SEG
    nseg = t // seg
    i = pl.program_id(2)
    qh, lane = _head_pair(q_ref[...])
    qneg = [-x for x in qh]
    r = lax.broadcasted_iota(jnp.int32, (seg, seg), 0)
    c = lax.broadcasted_iota(jnp.int32, (seg, seg), 1)
    later = jnp.where(r > c, 1.0, 0.0).astype(BF16)
    row = lax.broadcasted_iota(jnp.int32, (t, t), 0)
    col = lax.broadcasted_iota(jnp.int32, (t, t), 1)
    past = col < row

    def step(j, carry, diagonal):
        start = pl.multiple_of(j * t, t)
        k = k_ref[pl.ds(start, t), :]
        v = v_ref[pl.ds(start, t), :]
        out = []
        for h in range(2):
            tail, acc = carry[h]
            nz = lax.dot_general(qneg[h], k, NT, preferred_element_type=F32)
            minus_abs = lax.bitcast_convert_type(lax.bitcast_convert_type(nz, jnp.uint32) | SIGN_BIT, F32)
            log_1m = jnp.minimum(nz, 0.0) - jnp.log(1.0 + jnp.exp(minus_abs))
            base = log_1m - nz
            if diagonal:
                log_1m = jnp.where(past, log_1m, 0.0)
            sums = [jnp.dot(log_1m[:, b * seg:(b + 1) * seg].astype(BF16), later, preferred_element_type=F32)
                    for b in range(nseg)]
            for b in reversed(range(nseg)):
                sl = slice(b * seg, (b + 1) * seg)
                w = jnp.exp(base[:, sl] + (sums[b] + tail))
                if diagonal:
                    w = jnp.where(past[:, sl], w, 0.0)
                acc = acc + jnp.dot(w.astype(BF16), v[sl], preferred_element_type=F32)
                tail = tail + (sums[b][:, 0:1] + log_1m[:, b * seg:b * seg + 1])
            out.append((tail, acc))
        return tuple(out)

    init = tuple((jnp.zeros((t, 1), F32), jnp.zeros((t, LANES), F32)) for _ in range(2))
    carry = step(i, init, True)

    def live(state):
        jj, carry = state
        return (jj < i) & (jnp.maximum(jnp.max(carry[0][0]), jnp.max(carry[1][0])) > SB_DEAD_LOG)

    def walk(state):
        jj, carry = state
        return jj + 1, step(i - 1 - jj, carry, False)

    _, carry = lax.while_loop(live, walk, (jnp.int32(0), carry))
    o_ref[...] = jnp.where(lane < HEAD_DIM, carry[0][1], carry[1][1]).astype(o_ref.dtype)


def _sb_attention(qkv, batch):
    n = qkv.shape[0]
    seq = n // batch
    nq = seq // SB_T
    pairs = H_SB // 2
    return pl.pallas_call(
        _sb_kernel,
        grid=(batch, pairs, nq),
        in_specs=_attention_specs(seq, nq, SB_T, *_head_cols(MAIN_SB, MAIN_W)),
        out_specs=pl.BlockSpec((SB_T, LANES), lambda b, p, i: (b * nq + i, p)),
        out_shape=jax.ShapeDtypeStruct((n, pairs * LANES), BF16),
        compiler_params=_params(("parallel", "parallel", "arbitrary"), 56),
        name="sb_attention",
    )(qkv, qkv, qkv)


def _fox_kernel(q_ref, k_ref, v_ref, aq_ref, ak_ref, o_ref):
    tq, tk = FOX_TQ, FOX_TK
    i = pl.program_id(2)
    q = q_ref[...]
    lane = lax.broadcasted_iota(jnp.int32, q.shape, 1)
    own = [lane < HEAD_DIM, lane >= HEAD_DIM]
    qh = [jnp.where(own[h], q, aq_ref[...]) for h in range(2)]
    klane = lax.broadcasted_iota(jnp.int32, (tk, LANES), 1)
    kown = [klane < HEAD_DIM, klane >= HEAD_DIM]
    row = lax.broadcasted_iota(jnp.int32, (tq, tk), 0)
    col = lax.broadcasted_iota(jnp.int32, (tq, tk), 1)

    def step(j, carry, diagonal):
        start = pl.multiple_of(j * tk, tk)
        k = k_ref[pl.ds(start, tk), :]
        v = v_ref[pl.ds(start, tk), :]
        ak = ak_ref[pl.ds(start, tk), :]
        out = []
        for h in range(2):
            s = lax.dot_general(qh[h], jnp.where(kown[h], k, ak), NT, preferred_element_type=F32)
            if diagonal:
                s = jnp.where(col + j * tk <= row + i * tq, s, NEG)
            m, acc = carry[h]
            m_new = jnp.maximum(m, jnp.max(s, axis=1, keepdims=True))
            p = jnp.exp(s - m_new).astype(BF16)
            vh = jnp.where(kown[h], v, jnp.ones_like(v))
            acc = jnp.exp(m - m_new) * acc + jnp.dot(p, vh, preferred_element_type=F32)
            out.append((m_new, acc))
        return tuple(out)

    carry = tuple((jnp.full((tq, 1), NEG, F32), jnp.zeros((tq, LANES), F32)) for _ in range(2))
    n_full = (i * tq) // tk
    carry = lax.fori_loop(0, n_full, lambda j, c: step(j, c, False), carry)
    for d in range(max(tq // tk, 1)):
        carry = step(n_full + d, carry, True)
    o = []
    for h in range(2):
        acc = carry[h][1]
        o.append(acc / acc[:, HEAD_DIM * (1 - h):HEAD_DIM * (1 - h) + 1])
    o_ref[...] = jnp.where(own[0], o[0], o[1]).astype(o_ref.dtype)


def _fox_attention(qkv, aug_q, aug_k, batch):
    n = qkv.shape[0]
    seq = n // batch
    nq = seq // FOX_TQ
    pairs = H_FOX // 2
    specs = _attention_specs(seq, nq, FOX_TQ, *_head_cols(MAIN_FOX, MAIN_W))
    specs += [pl.BlockSpec((FOX_TQ, LANES), lambda b, p, i: (b * nq + i, p)),
              pl.BlockSpec((seq, LANES), lambda b, p, i: (b, p))]
    return pl.pallas_call(
        _fox_kernel,
        grid=(batch, pairs, nq),
        in_specs=specs,
        out_specs=pl.BlockSpec((FOX_TQ, LANES), lambda b, p, i: (b * nq + i, p)),
        out_shape=jax.ShapeDtypeStruct((n, pairs * LANES), BF16),
        compiler_params=_params(("parallel", "parallel", "arbitrary"), 48),
        name="fox_attention",
    )(qkv, qkv, qkv, aug_q, aug_k)


def _moba_kernel(q_ref, k_ref, v_ref, onehot_ref, bias_ref, o_ref, kmean_ref):
    t, mb = MOBA_T, MOBA_BLOCK
    nb = t // mb
    seq = k_ref.shape[0]
    i = pl.program_id(2)

    @pl.when(i == 0)
    def _():
        blk = lax.broadcasted_iota(jnp.int32, (LANES, seq), 0)
        pos = lax.broadcasted_iota(jnp.int32, (LANES, seq), 1)
        member = jnp.where((pos >= blk * mb) & (pos < (blk + 1) * mb), 1.0, 0.0).astype(BF16)
        mean = jnp.dot(member, k_ref[...], preferred_element_type=F32) * (1.0 / mb)
        hi = mean.astype(BF16)
        kmean_ref[0] = hi
        kmean_ref[1] = (mean - hi.astype(F32)).astype(BF16)

    q = q_ref[...]
    lane = lax.broadcasted_iota(jnp.int32, (t, LANES), 1)
    own_half = [lane < HEAD_DIM, lane >= HEAD_DIM]
    zero = jnp.zeros_like(q)
    blk_id = lane.astype(F32)
    own_blk = ((i * t + lax.broadcasted_iota(jnp.int32, (t, LANES), 0)) // mb).astype(F32)
    fully_past = blk_id < own_blk

    qh = []
    for h in range(2):
        q_only = jnp.where(own_half[h], q, zero)
        gate = (lax.dot_general(q_only, kmean_ref[0], NT, preferred_element_type=F32)
                + lax.dot_general(q_only, kmean_ref[1], NT, preferred_element_type=F32))
        gate = jnp.where(fully_past, gate, -jnp.inf)
        picked = jnp.zeros((t, LANES), jnp.bool_)
        for _ in range(MOBA_TOPK):
            best = jnp.max(gate, axis=1, keepdims=True)
            first = jnp.min(jnp.where(gate == best, blk_id, float(LANES)), axis=1, keepdims=True)
            pick = blk_id == first
            picked = picked | pick
            gate = jnp.where(pick, -jnp.inf, gate)
        allowed = (picked & fully_past) | (blk_id == own_blk)
        penalty = jnp.where(allowed, 0.0, NEG)
        if h == 0:
            penalty = pltpu.roll(penalty, HEAD_DIM, axis=1)
        qh.append(jnp.where(own_half[h], q, penalty.astype(BF16)))

    def step(j, carry):
        start = pl.multiple_of(j * t, t)
        k = k_ref[pl.ds(start, t), :]
        v = v_ref[pl.ds(start, t), :]
        onehot = onehot_ref[pl.ds(start, t), :]
        out = []
        for h in range(2):
            s = lax.dot_general(qh[h], jnp.where(own_half[h], k, onehot), NT, preferred_element_type=F32)
            rows = []
            for a in range(nb):
                tiles = [bias_ref[jnp.clip((i - j) * nb + a - b, 0, MOBA_NT), h] for b in range(nb)]
                rows.append(jnp.concatenate(tiles, axis=1))
            s = s + jnp.concatenate(rows, axis=0)
            m, acc = carry[h]
            m_new = jnp.maximum(m, jnp.max(s, axis=1, keepdims=True))
            p = jnp.exp(s - m_new).astype(BF16)
            vh = jnp.where(own_half[h], v, jnp.ones_like(v))
            acc = jnp.exp(m - m_new) * acc + jnp.dot(p, vh, preferred_element_type=F32)
            out.append((m_new, acc))
        return tuple(out)

    carry = tuple((jnp.full((t, 1), NEG, F32), jnp.zeros((t, LANES), F32)) for _ in range(2))
    carry = lax.fori_loop(0, i + 1, step, carry)
    o = []
    for h in range(2):
        acc = carry[h][1]
        o.append(acc / acc[:, HEAD_DIM * (1 - h):HEAD_DIM * (1 - h) + 1])
    o_ref[...] = jnp.where(own_half[0], o[0], o[1]).astype(o_ref.dtype)


def _moba_attention(qkv, bias_tiles, batch):
    n = qkv.shape[0]
    seq = n // batch
    assert seq % MOBA_T == 0 and seq // MOBA_BLOCK <= HEAD_DIM
    nq = seq // MOBA_T
    pairs = H_MOBA // 2
    blk = np.arange(seq)[:, None] // MOBA_BLOCK
    onehot = jnp.asarray(blk == (np.arange(LANES)[None, :] % HEAD_DIM), BF16)
    specs = _attention_specs(seq, nq, MOBA_T, *_head_cols(MAIN_MOBA, MAIN_W))
    specs.append(pl.BlockSpec((seq, LANES), lambda b, p, i: (0, 0)))
    specs.append(pl.BlockSpec((None, MOBA_NT + 1, 2, MOBA_BLOCK, MOBA_BLOCK), lambda b, p, i: (p, 0, 0, 0, 0)))
    return pl.pallas_call(
        _moba_kernel,
        grid=(batch, pairs, nq),
        in_specs=specs,
        out_specs=pl.BlockSpec((MOBA_T, LANES), lambda b, p, i: (b * nq + i, p)),
        out_shape=jax.ShapeDtypeStruct((n, pairs * LANES), BF16),
        scratch_shapes=[pltpu.VMEM((2, LANES, LANES), BF16)],
        compiler_params=_params(("arbitrary", "arbitrary", "arbitrary"), 56),
        name="moba_attention",
    )(qkv, qkv, qkv, onehot, bias_tiles)


def _moba_bias_tiles(rel_bias):
    r = jnp.arange(MOBA_BLOCK)
    dist = (jnp.arange(MOBA_NT + 1)[:, None, None] * MOBA_BLOCK + r[None, :, None] - r[None, None, :])
    tiles = _bias_lookup(rel_bias[:, H_DIL:], _t5_bucket(jnp.maximum(dist, 0)))
    tiles = jnp.where((dist >= 0)[..., None], tiles, NEG)
    tiles = tiles.transpose(3, 0, 1, 2).reshape(H_MOBA // 2, 2, MOBA_NT + 1, MOBA_BLOCK, MOBA_BLOCK)
    return tiles.transpose(0, 2, 1, 3, 4).astype(F32)


def _check_moba_saturation():
    n = np.float32(MOBA_NT * MOBA_BLOCK - (MOBA_BLOCK - 1))
    exact = N_BUCKETS // 2
    large = exact + int(np.log(n / np.float32(exact)) / math.log(MAX_DIST / exact) * (N_BUCKETS - exact))
    assert large >= N_BUCKETS - 1, "MOBA_NT too small for the bias bucket table"


_check_moba_saturation()


def _dilated_kernel(q_ref, k_ref, v_ref, bias_ref, o_ref, lse_ref):
    w = DIL_W
    nb = q_ref.shape[0] // w
    row = lax.broadcasted_iota(jnp.int32, (w, w), 0)
    col = lax.broadcasted_iota(jnp.int32, (w, w), 1)
    in_cur = col <= row
    in_prev = col >= row

    def body(n, _):
        cur = pl.multiple_of(n * w, w)
        prev = pl.multiple_of(jnp.maximum(n - 1, 0) * w, w)
        qh, lane = _head_pair(q_ref[pl.ds(cur, w), :])
        k_cur, v_cur = k_ref[pl.ds(cur, w), :], v_ref[pl.ds(cur, w), :]
        k_prev, v_prev = k_ref[pl.ds(prev, w), :], v_ref[pl.ds(prev, w), :]
        has_prev = n > 0
        outs, lses = [], []
        for h in range(2):
            z_prev = lax.dot_general(qh[h], k_prev, NT, preferred_element_type=F32) + bias_ref[h, :, 0:w]
            z_cur = lax.dot_general(qh[h], k_cur, NT, preferred_element_type=F32) + bias_ref[h, :, w:2 * w]
            z_prev = jnp.where(in_prev & has_prev, z_prev, NEG)
            z_cur = jnp.where(in_cur, z_cur, NEG)
            m = jnp.maximum(jnp.max(z_prev, axis=1, keepdims=True), jnp.max(z_cur, axis=1, keepdims=True))
            p_prev = jnp.exp(z_prev - m)
            p_cur = jnp.exp(z_cur - m)
            l = jnp.sum(p_prev, axis=1, keepdims=True) + jnp.sum(p_cur, axis=1, keepdims=True)
            o = (jnp.dot(p_prev.astype(BF16), v_prev, preferred_element_type=F32)
                 + jnp.dot(p_cur.astype(BF16), v_cur, preferred_element_type=F32))
            outs.append(o / l)
            lses.append(m + jnp.log(l))
        o_ref[pl.ds(cur, w), :] = jnp.where(lane < HEAD_DIM, outs[0], outs[1])
        lse_ref[pl.ds(cur, w), :] = jnp.where(lane < HEAD_DIM, lses[0], lses[1])
        return 0

    lax.fori_loop(0, nb, body, 0)


def _dilated_group(qkv, head_off, width, bias, batch, group):
    window, dil = DIL_GROUPS[group]
    assert window // dil == DIL_W
    n = qkv.shape[0]
    sub = n // batch // dil
    assert sub % DIL_W == 0
    pairs = H_DIL_PER_GROUP // 2
    q_col, k_col, v_col = _head_cols(head_off, width)

    def spec(col0):
        return pl.BlockSpec((sub, LANES), lambda b, p, r: (b * dil + r, col0 + p))

    out_spec = pl.BlockSpec((sub, LANES), lambda b, p, r: (b * dil + r, p))
    out_shape = jax.ShapeDtypeStruct((n, pairs * LANES), F32)
    return pl.pallas_call(
        _dilated_kernel,
        grid=(batch, pairs, dil),
        in_specs=[spec(q_col), spec(k_col), spec(v_col),
                  pl.BlockSpec((None, 2, DIL_W, 2 * DIL_W), lambda b, p, r: (group * pairs + p, 0, 0, 0))],
        out_specs=[out_spec, out_spec],
        out_shape=[out_shape, out_shape],
        compiler_params=_params(("parallel", "parallel", "parallel"), 48),
        name=f"dilated_attention_g{group}",
    )(qkv, qkv, qkv, bias)


def _by_residue(a, batch, dil):
    n, c = a.shape
    return a.reshape(batch, n // batch // dil, dil, c).transpose(0, 2, 1, 3).reshape(n, c)


def _by_position(a, batch, dil):
    n, c = a.shape
    return a.reshape(batch, dil, n // batch // dil, c).transpose(0, 2, 1, 3).reshape(n, c)


def _dilated_bias_tiles(rel_bias):
    steps = jnp.arange(DIL_W)[:, None] + DIL_W - jnp.arange(2 * DIL_W)[None, :]
    tiles = []
    for g, (_, dil) in enumerate(DIL_GROUPS):
        tab = rel_bias[:, g * H_DIL_PER_GROUP:(g + 1) * H_DIL_PER_GROUP]
        tiles.append(_bias_lookup(tab, _t5_bucket(jnp.maximum(steps, 0) * dil)).transpose(2, 0, 1))
    tiles = jnp.stack(tiles, 0).astype(F32)
    return tiles.reshape(len(DIL_GROUPS) * H_DIL_PER_GROUP // 2, 2, DIL_W, 2 * DIL_W)


def _dilated_combine_kernel(o0, o1, o2, l0, l1, l2, out_ref):
    lse = [l0[...], l1[...], l2[...]]
    m = jnp.maximum(jnp.maximum(lse[0], lse[1]), lse[2])
    e = [jnp.exp(x - m) for x in lse]
    den = e[0] + e[1] + e[2]
    out = (e[0] / den) * o0[...] + (e[1] / den) * o1[...] + (e[2] / den) * o2[...]
    out_ref[...] = out.astype(out_ref.dtype)


def _dilated_combine(os, lses, tm=1024):
    n, c = os[0].shape
    spec = pl.BlockSpec((tm, c), lambda i: (i, 0))
    return pl.pallas_call(
        _dilated_combine_kernel,
        grid=(n // tm,),
        in_specs=[spec] * 6,
        out_specs=spec,
        out_shape=jax.ShapeDtypeStruct((n, c), BF16),
        compiler_params=_params(("parallel",), 32),
        name="dilated_combine",
    )(*os, *lses)


def _merge_kernel(x_ref, o_sb, o_dil, o_moba, o_fox, wg0, wg1, wg2, wg3, bg_ref,
                  w_sb, w_dil, w_moba, w_fox, out_ref):
    x = x_ref[...]
    acc = None
    branches = ((o_sb, w_sb, wg0), (o_dil, w_dil, wg1), (o_moba, w_moba, wg2), (o_fox, w_fox, wg3))
    for b, (o_ref, w_ref, wg_ref) in enumerate(branches):
        gate = jax.nn.sigmoid(jnp.dot(x, wg_ref[...], preferred_element_type=F32) + bg_ref[b])
        term = gate * jnp.dot(o_ref[...], w_ref[...], preferred_element_type=F32)
        acc = term if acc is None else acc + term
    out_ref[...] = acc.astype(out_ref.dtype)


def _gated_merge(x_bf, outs, w_gate, b_gate, w_branch, layer, tm=512, tn=512):
    n, d = x_bf.shape
    nj = d // tn
    row = lambda width: pl.BlockSpec((tm, width), lambda j, i: (i, 0))
    in_specs = [row(d)] + [row(o.shape[1]) for o in outs]
    in_specs += [pl.BlockSpec((None, d, tn), functools.partial(lambda j, i, b: (layer, 0, b * nj + j), b=b))
                 for b in range(N_BRANCH)]
    in_specs.append(pl.BlockSpec((None, N_BRANCH, 1, tn), lambda j, i: (layer, 0, 0, j)))
    in_specs += [pl.BlockSpec((None, w.shape[1], tn), lambda j, i: (layer, 0, j)) for w in w_branch]
    return pl.pallas_call(
        _merge_kernel,
        grid=(nj, n // tm),
        in_specs=in_specs,
        out_specs=pl.BlockSpec((tm, tn), lambda j, i: (i, j)),
        out_shape=jax.ShapeDtypeStruct((n, d), BF16),
        compiler_params=_params(("parallel", "parallel"), 48),
        name="gated_merge",
    )(x_bf, *outs, w_gate, w_gate, w_gate, w_gate, b_gate, *w_branch)


def _layer_norm_store(z, g_ref, b_ref, xo_ref, xb_ref):
    mu = jnp.mean(z, axis=1, keepdims=True)
    zc = z - mu
    var = jnp.mean(zc * zc, axis=1, keepdims=True)
    out = zc * lax.rsqrt(var + LN_EPS) * g_ref[...] + b_ref[...]
    xo_ref[...] = out
    xb_ref[...] = out.astype(BF16)


def _out_ln_kernel(m_ref, w_ref, x_ref, g_ref, b_ref, xo_ref, xb_ref):
    y = jnp.dot(m_ref[...], w_ref[...], preferred_element_type=F32)
    _layer_norm_store(DN_ALPHA * x_ref[...] + y, g_ref, b_ref, xo_ref, xb_ref)


def _combine_ln_kernel(y_ref, gate_ref, x_ref, g_ref, b_ref, xo_ref, xb_ref):
    gate = gate_ref[...]
    z = DN_ALPHA * x_ref[...]
    for k in range(TOP_K):
        z = z + gate[:, k:k + 1] * y_ref[k].astype(F32)
    _layer_norm_store(z, g_ref, b_ref, xo_ref, xb_ref)


def _ln_call(kernel, lead_specs, lead_args, x, g, b, tm, name):
    n, d = x.shape
    row = pl.BlockSpec((tm, d), lambda i: (i, 0))
    vec = pl.BlockSpec((1, d), lambda i: (0, 0))
    return pl.pallas_call(
        kernel,
        grid=(n // tm,),
        in_specs=lead_specs + [row, vec, vec],
        out_specs=[row, row],
        out_shape=[jax.ShapeDtypeStruct((n, d), F32), jax.ShapeDtypeStruct((n, d), BF16)],
        compiler_params=_params(("parallel",), 48),
        name=name,
    )(*lead_args, x, g, b)


def _out_proj_ln(merged, w_out, layer, x, g, b, tm=256):
    d = x.shape[1]
    specs = [pl.BlockSpec((tm, d), lambda i: (i, 0)), pl.BlockSpec((None, d, d), lambda i: (layer, 0, 0))]
    return _ln_call(_out_ln_kernel, specs, (merged, w_out), x, g, b, tm, "out_proj_ln")


def _combine_ln(yk, gate, x, g, b, tm=256):
    d = x.shape[1]
    specs = [pl.BlockSpec((TOP_K, tm, d), lambda i: (0, i, 0)), pl.BlockSpec((tm, LANES), lambda i: (i, 0))]
    return _ln_call(_combine_ln_kernel, specs, (yk, gate), x, g, b, tm, "moe_combine_ln")


def _router_kernel(x_ref, w_ref, b_ref, idx_ref, gate_ref, rank_ref, count_ref, seen_ref):
    @pl.when(pl.program_id(0) == 0)
    def _():
        seen_ref[...] = jnp.zeros_like(seen_ref)

    tm = x_ref.shape[0]
    logits = jnp.dot(x_ref[...], w_ref[...], preferred_element_type=F32, precision=HIGHEST) + b_ref[...]
    col = lax.broadcasted_iota(jnp.int32, logits.shape, 1)
    colf = col.astype(F32)
    logits = jnp.where(col < N_EXPERTS, logits, -jnp.inf)
    vals, idxs, hits = [], [], []
    for _ in range(TOP_K):
        best = jnp.max(logits, axis=1, keepdims=True)
        first = jnp.min(jnp.where(logits == best, colf, float(LANES)), axis=1, keepdims=True)
        hit = colf == first
        logits = jnp.where(hit, -jnp.inf, logits)
        vals.append(best)
        idxs.append(first)
        hits.append(jnp.where(hit, 1.0, 0.0))
    es = [jnp.exp(v - vals[0]) for v in vals]
    den = es[0] + es[1] + es[2] + es[3]

    chosen = hits[0] + hits[1] + hits[2] + hits[3]
    row = lax.broadcasted_iota(jnp.int32, (tm, tm), 0)
    before = jnp.where(lax.broadcasted_iota(jnp.int32, (tm, tm), 1) < row, 1.0, 0.0).astype(BF16)
    earlier = jnp.dot(before, chosen.astype(BF16), preferred_element_type=F32) + seen_ref[...]
    seen = seen_ref[...] + jnp.sum(chosen, axis=0, keepdims=True)
    seen_ref[...] = seen
    count_ref[...] = seen

    idx_out = jnp.zeros(logits.shape, F32)
    gate_out = jnp.zeros(logits.shape, F32)
    rank_out = jnp.zeros(logits.shape, F32)
    for r in range(TOP_K):
        idx_out = jnp.where(col == r, idxs[r], idx_out)
        gate_out = jnp.where(col == r, es[r] / den, gate_out)
        rank_out = jnp.where(col == r, jnp.sum(hits[r] * earlier, axis=1, keepdims=True), rank_out)
    idx_ref[...] = idx_out.astype(jnp.int32)
    gate_ref[...] = gate_out
    rank_ref[...] = rank_out.astype(jnp.int32)


def _router(x, w_r, b_r, tm=512):
    n, d = x.shape
    out = pl.BlockSpec((tm, LANES), lambda i: (i, 0))
    return pl.pallas_call(
        _router_kernel,
        grid=(n // tm,),
        in_specs=[pl.BlockSpec((tm, d), lambda i: (i, 0)),
                  pl.BlockSpec((d, LANES), lambda i: (0, 0)),
                  pl.BlockSpec((1, LANES), lambda i: (0, 0))],
        out_specs=[out, out, out, pl.BlockSpec((1, LANES), lambda i: (0, 0))],
        out_shape=[jax.ShapeDtypeStruct((n, LANES), jnp.int32), jax.ShapeDtypeStruct((n, LANES), F32),
                   jax.ShapeDtypeStruct((n, LANES), jnp.int32), jax.ShapeDtypeStruct((1, LANES), F32)],
        scratch_shapes=[pltpu.VMEM((1, LANES), F32)],
        compiler_params=_params(("arbitrary",), 32),
        name="moe_router",
    )(x, w_r, b_r)


def _expert_kernel(blk_exp_ref, xs_ref, wu_ref, bu_ref, wd_ref, bd_ref, y_ref):
    del blk_exp_ref
    h = jnp.dot(xs_ref[...], wu_ref[...], preferred_element_type=F32) + bu_ref[...]
    glu = jnp.minimum(h[:, :D_FF], SWIGLU_LIMIT)
    lin = jnp.clip(h[:, D_FF:], -SWIGLU_LIMIT, SWIGLU_LIMIT)
    act = glu * jax.nn.sigmoid(SWIGLU_ALPHA * glu) * (lin + 1.0)
    y = jnp.dot(act.astype(BF16), wd_ref[...], preferred_element_type=F32) + bd_ref[...]
    y_ref[...] = y.astype(y_ref.dtype)


def _expert_ffn(xs, blk_exp, w_up, b_up, w_down, b_down, layer):
    p, d = xs.shape
    grid_spec = pltpu.PrefetchScalarGridSpec(
        num_scalar_prefetch=1,
        grid=(p // MOE_ROWS,),
        in_specs=[pl.BlockSpec((MOE_ROWS, d), lambda i, e: (i, 0)),
                  pl.BlockSpec((None, None, d, 2 * D_FF), lambda i, e: (layer, e[i], 0, 0)),
                  pl.BlockSpec((None, None, 1, 2 * D_FF), lambda i, e: (layer, e[i], 0, 0)),
                  pl.BlockSpec((None, None, D_FF, d), lambda i, e: (layer, e[i], 0, 0)),
                  pl.BlockSpec((None, None, 1, d), lambda i, e: (layer, e[i], 0, 0))],
        out_specs=pl.BlockSpec((MOE_ROWS, d), lambda i, e: (i, 0)),
    )
    return pl.pallas_call(
        _expert_kernel,
        grid_spec=grid_spec,
        out_shape=jax.ShapeDtypeStruct((p, d), BF16),
        compiler_params=_params(("arbitrary",), 48),
        name="expert_ffn",
    )(blk_exp, xs, w_up, b_up, w_down, b_down)


def _moe(x, x_bf, w_r, b_r, w_up, b_up, w_down, b_down, layer):
    n, d = x.shape
    nk = n * TOP_K
    idx, gate, rank, count = _router(x, w_r, b_r)
    counts = count[0, :N_EXPERTS].astype(jnp.int32)
    start = jnp.cumsum(counts) - counts
    padded = (counts + MOE_ROWS - 1) // MOE_ROWS * MOE_ROWS
    pend = jnp.cumsum(padded)
    pstart = pend - padded
    experts = jnp.arange(N_EXPERTS, dtype=jnp.int32)
    chosen = idx[:, :TOP_K, None] == experts
    slot = jnp.sum(jnp.where(chosen, pstart, 0), axis=2) + rank[:, :TOP_K]
    n_blocks = -(-(nk + N_EXPERTS * (MOE_ROWS - 1)) // MOE_ROWS)
    first_row = jnp.arange(n_blocks, dtype=jnp.int32) * MOE_ROWS
    blk_exp = jnp.minimum(jnp.sum(pend[None, :] <= first_row[:, None], axis=1), N_EXPERTS - 1).astype(jnp.int32)
    order = jnp.argsort(slot.reshape(nk)).astype(jnp.int32)
    blk_is = blk_exp[:, None] == experts
    blk_start = jnp.sum(jnp.where(blk_is, start, 0), axis=1)
    blk_pstart = jnp.sum(jnp.where(blk_is, pstart, 0), axis=1)
    blk_count = jnp.sum(jnp.where(blk_is, counts, 0), axis=1)
    within = first_row[:, None] + jnp.arange(MOE_ROWS, dtype=jnp.int32)[None, :] - blk_pstart[:, None]
    pair = jnp.clip(blk_start[:, None] + within, 0, nk - 1)
    slot_tok = jnp.where(within < blk_count[:, None], order[pair] // TOP_K, 0).reshape(n_blocks * MOE_ROWS)
    slot_tok, slot_km = lax.optimization_barrier((slot_tok, slot.T.reshape(nk)))
    xs = x_bf[slot_tok]
    y = _expert_ffn(xs, blk_exp, w_up, b_up, w_down, b_down, layer)
    return y[slot_km].reshape(TOP_K, n, d), gate


def kernel(x, rel_bias, w_in, b_forget, b_gate, w_br_sb, w_br_dil, w_br_moba, w_br_fox, w_out,
           ln1_g, ln1_b, w_router, b_router, w_up, b_up, w_down, b_down, ln2_g, ln2_b):
    batch, seq, d = x.shape
    n = batch * seq
    depth = w_in.shape[0]

    w_qkv = _qkv_weights(w_in)
    w_f =jnp.pad(w_in[:, :, C_QKV:C_QKV + H_FOX], ((0, 0), (0, 0), (0, LANES - H_FOX)))
    b_f = jnp.pad(b_forget, ((0, 0), (0, LANES - H_FOX))).reshape(depth, 1, LANES)
    w_gate = _gate_weights(w_in)
    b_gate3 = b_gate.reshape(depth, N_BRANCH, 1, d)
    w_branch = [w.astype(BF16) for w in (w_br_sb, w_br_dil, w_br_moba, w_br_fox)]
    w_out_bf = w_out.astype(BF16)
    w_r = jnp.pad(w_router, ((0, 0), (0, 0), (0, LANES - N_EXPERTS)))
    b_r = jnp.pad(b_router, ((0, 0), (0, LANES - N_EXPERTS))).reshape(depth, 1, LANES)
    w_up_bf = w_up.astype(BF16)
    w_down_bf = w_down.astype(BF16)
    b_up3 = b_up.reshape(depth, N_EXPERTS, 1, 2 * D_FF)
    b_down3 = b_down.reshape(depth, N_EXPERTS, 1, d)
    moba_bias = _moba_bias_tiles(rel_bias)
    dil_bias = _dilated_bias_tiles(rel_bias)

    xf = x.reshape(n, d)
    xb = xf.astype(BF16)
    for l in range(depth):
        qkv = _matmul(xb, w_qkv, l, 0, 3 * MAIN_W, 1024, 768, BF16, "qkv_proj")
        aug_q, aug_k = _decay(xf, w_f[l], b_f[l], batch)
        o_sb = _sb_attention(qkv, batch)
        os, lses = [], []
        o, lse = _dilated_group(qkv, MAIN_DIL, MAIN_W, dil_bias, batch, 0)
        os.append(o)
        lses.append(lse)
        for g in range(1, len(DIL_GROUPS)):
            dil = DIL_GROUPS[g][1]
            qkv_g = _matmul(_by_residue(xb, batch, dil), w_qkv, l, 3 * (MAIN_W + (g - 1) * DIL_GROUP_W),
                            3 * DIL_GROUP_W, 1024, 3 * DIL_GROUP_W, BF16, f"qkv_proj_g{g}")
            o, lse = _dilated_group(qkv_g, 0, DIL_GROUP_W, dil_bias, batch, g)
            os.append(_by_position(o, batch, dil))
            lses.append(_by_position(lse, batch, dil))
        o_dil = _dilated_combine(os, lses)
        o_moba = _moba_attention(qkv, moba_bias, batch)
        o_fox = _fox_attention(qkv, aug_q, aug_k, batch)
        merged = _gated_merge(xb, (o_sb, o_dil, o_moba, o_fox), w_gate, b_gate3, w_branch, l)
        xf, xb = _out_proj_ln(merged, w_out_bf, l, xf, ln1_g[l].reshape(1, d), ln1_b[l].reshape(1, d))
        yk, gate = _moe(xf, xb, w_r[l], b_r[l], w_up_bf, b_up3, w_down_bf, b_down3, l)
        xf, xb = _combine_ln(yk, gate, xf, ln2_g[l].reshape(1, d), ln2_b[l].reshape(1, d))
    return xf.reshape(batch, seq, d)
```

```python
import functools
import math

import numpy as np
import jax
import jax.numpy as jnp
from jax import lax
from jax.experimental import pallas as pl
from jax.experimental.pallas import tpu as pltpu

D_MODEL = 2048
DEPTH = 4
HEAD_DIM = 64
H_SB = 8
H_DIL_PER_GROUP = 4
DIL_GROUPS = ((128, 1), (512, 4), (2048, 16))
H_DIL = H_DIL_PER_GROUP * len(DIL_GROUPS)
H_MOBA = 8
H_FOX = 8
H_ALL = H_SB + H_DIL + H_MOBA + H_FOX
SB_OFF = 0
DIL_OFF = SB_OFF + H_SB
MOBA_OFF = DIL_OFF + H_DIL
FOX_OFF = MOBA_OFF + H_MOBA
W_MIX = H_ALL * HEAD_DIM
N_BRANCH = 4
C_QKV = 3 * W_MIX
MOBA_BLOCK = 256
MOBA_TOPK = 3
N_BUCKETS = 32
MAX_DIST = 2048
N_EXPERTS = 32
TOP_K = 4
D_FF = 768
SWIGLU_ALPHA = 1.702
SWIGLU_LIMIT = 7.0
LN_EPS = 1e-5
DN_ALPHA = (2 * DEPTH) ** 0.25

F32 = jnp.float32
BF16 = jnp.bfloat16
LANES = 128
NEG = -1e30
HIGHEST = lax.Precision.HIGHEST
MIB = 1024 * 1024
NT = (((1,), (1,)), ((), ()))
SIGN_BIT = np.uint32(0x80000000)

MAIN_SB = 0
MAIN_DIL = MAIN_SB + H_SB
MAIN_MOBA = MAIN_DIL + H_DIL_PER_GROUP
MAIN_FOX = MAIN_MOBA + H_MOBA
MAIN_W = (MAIN_FOX + H_FOX) * HEAD_DIM
DIL_GROUP_W = H_DIL_PER_GROUP * HEAD_DIM
DIL_W = 128
MOE_ROWS = 256
SB_T = 1024
SB_SEG = 256
SB_DEAD_LOG = -110.0
FOX_TQ = 1024
FOX_TK = 1024
MOBA_T = 1024
MOBA_NT = 8
AUG_PARTS = 3


def _params(sem, vmem_mib):
    return pltpu.CompilerParams(dimension_semantics=sem, vmem_limit_bytes=vmem_mib * MIB)


def _t5_bucket(n):
    exact = N_BUCKETS // 2
    nf = jnp.maximum(n, 1).astype(F32)
    large = exact + (jnp.log(nf / exact) / math.log(MAX_DIST / exact) * (N_BUCKETS - exact)).astype(jnp.int32)
    large = jnp.minimum(large, N_BUCKETS - 1)
    return jnp.where(n < exact, n, large)


def _bias_lookup(table, bucket):
    onehot = (bucket[..., None] == jnp.arange(N_BUCKETS)).astype(F32)
    return jnp.einsum('...k,kh->h...', onehot, table.astype(F32), precision=HIGHEST)


def _head_pair(q):
    lane = lax.broadcasted_iota(jnp.int32, q.shape, 1)
    zero = jnp.zeros_like(q)
    return [jnp.where(lane < HEAD_DIM, q, zero), jnp.where(lane >= HEAD_DIM, q, zero)], lane


def _attention_specs(seq, nq, tq, q_col, k_col, v_col):
    return [pl.BlockSpec((tq, LANES), lambda b, p, i: (b * nq + i, q_col + p)),
            pl.BlockSpec((seq, LANES), lambda b, p, i: (b, k_col + p)),
            pl.BlockSpec((seq, LANES), lambda b, p, i: (b, v_col + p))]


def _head_cols(off, width):
    q_col = off * HEAD_DIM // LANES
    return q_col, q_col + width // LANES, q_col + 2 * width // LANES


def _matmul_kernel(a_ref, w_ref, o_ref):
    o_ref[...] = jnp.dot(a_ref[...], w_ref[...], preferred_element_type=F32).astype(o_ref.dtype)


def _matmul(a, w, layer, col0, n, tm, tn, out_dtype, name):
    m, k = a.shape
    first = col0 // tn
    return pl.pallas_call(
        _matmul_kernel,
        grid=(n // tn, m // tm),
        in_specs=[pl.BlockSpec((tm, k), lambda j, i: (i, 0)),
                  pl.BlockSpec((None, k, tn), lambda j, i: (layer, 0, first + j))],
        out_specs=pl.BlockSpec((tm, tn), lambda j, i: (i, j)),
        out_shape=jax.ShapeDtypeStruct((m, n), out_dtype),
        compiler_params=_params(("parallel", "parallel"), 40),
        name=name,
    )(a, w)


QKV_TILE = 256
MAIN_HEADS = ((SB_OFF, H_SB), (DIL_OFF, H_DIL_PER_GROUP), (MOBA_OFF, H_MOBA), (FOX_OFF, H_FOX))


def _qkv_column_plan():
    groups = [MAIN_HEADS] + [((DIL_OFF + g * H_DIL_PER_GROUP, H_DIL_PER_GROUP),) for g in range(1, len(DIL_GROUPS))]
    src, is_q = [], []
    for heads in groups:
        for sec in range(3):
            for lo, cnt in heads:
                start = sec * W_MIX + lo * HEAD_DIM
                assert start % QKV_TILE == 0 and (cnt * HEAD_DIM) % QKV_TILE == 0
                for b in range(cnt * HEAD_DIM // QKV_TILE):
                    src.append(start // QKV_TILE + b)
                    is_q.append(int(sec == 0))
    assert sorted(src) == list(range(C_QKV // QKV_TILE))
    return np.asarray(src, np.int32), np.asarray(is_q, np.int32)


def _qkv_weight_kernel(src_ref, is_q_ref, w_ref, o_ref):
    del src_ref
    scale = jnp.where(is_q_ref[pl.program_id(1)] == 1, HEAD_DIM ** -0.5, 1.0)
    o_ref[...] = (w_ref[...] * scale).astype(BF16)


def _qkv_weights(w_in):
    depth, d, _ = w_in.shape
    src, is_q = _qkv_column_plan()
    grid_spec = pltpu.PrefetchScalarGridSpec(
        num_scalar_prefetch=2,
        grid=(depth, len(src)),
        in_specs=[pl.BlockSpec((None, d, QKV_TILE), lambda l, j, src, is_q: (l, 0, src[j]))],
        out_specs=pl.BlockSpec((None, d, QKV_TILE), lambda l, j, src, is_q: (l, 0, j)),
    )
    return pl.pallas_call(
        _qkv_weight_kernel,
        grid_spec=grid_spec,
        out_shape=jax.ShapeDtypeStruct((depth, d, C_QKV), BF16),
        compiler_params=_params(("parallel", "parallel"), 24),
        name="qkv_weights",
    )(jnp.asarray(src), jnp.asarray(is_q), w_in)


GATE_COL0 = C_QKV + H_FOX
GATE_ALIGNED = GATE_COL0 // LANES * LANES
GATE_TILE = 256


def _gate_weight_kernel(a_ref, b_ref, o_ref):
    shift = GATE_COL0 - GATE_ALIGNED
    both = jnp.concatenate([a_ref[...], b_ref[...]], axis=1)
    o_ref[...] = both[:, shift:shift + GATE_TILE].astype(BF16)


def _gate_weights(w_in):
    depth, d, _ = w_in.shape
    width = N_BRANCH * d
    assert GATE_ALIGNED % GATE_TILE == 0 and width % GATE_TILE == 0
    a0 = GATE_ALIGNED // GATE_TILE
    per = GATE_TILE // LANES
    return pl.pallas_call(
        _gate_weight_kernel,
        grid=(depth, width // GATE_TILE),
        in_specs=[pl.BlockSpec((None, d, GATE_TILE), lambda l, j: (l, 0, a0 + j)),
                  pl.BlockSpec((None, d, LANES), lambda l, j: (l, 0, (a0 + j + 1) * per))],
        out_specs=pl.BlockSpec((None, d, GATE_TILE), lambda l, j: (l, 0, j)),
        out_shape=jax.ShapeDtypeStruct((depth, d, width), BF16),
        compiler_params=_params(("parallel", "parallel"), 24),
        name="gate_weights",
    )(w_in, w_in)


def _aug_constants():
    pairs = H_FOX // 2
    place_k = np.zeros((AUG_PARTS, LANES, pairs * LANES), np.float32)
    place_q = np.zeros((AUG_PARTS, LANES, pairs * LANES), np.float32)
    ones_k = np.zeros((1, pairs * LANES), np.float32)
    ones_q = np.zeros((1, pairs * LANES), np.float32)
    for g in range(H_FOX):
        p, h = divmod(g, 2)
        base = p * LANES + HEAD_DIM * (1 - h)
        for j in range(AUG_PARTS):
            place_k[j, g, base + j] = 1.0
            place_q[j, g, base + AUG_PARTS + j] = 1.0
            ones_k[0, base + AUG_PARTS + j] = 1.0
            ones_q[0, base + j] = 1.0
    return place_k, place_q, ones_k, ones_q


def _split_parts(x):
    parts = []
    for _ in range(AUG_PARTS):
        piece = x.astype(BF16)
        parts.append(piece)
        x = x - piece.astype(F32)
    return parts


def _decay_kernel(x_ref, wf_ref, bf_ref, pk_ref, pq_ref, ok_ref, oq_ref, augq_ref, augk_ref, carry_ref):
    @pl.when(pl.program_id(1) == 0)
    def _():
        carry_ref[...] = jnp.zeros_like(carry_ref)

    tm = x_ref.shape[0]
    f = jnp.dot(x_ref[...], wf_ref[...], preferred_element_type=F32, precision=HIGHEST) + bf_ref[...]
    logf = jnp.minimum(f, 0.0) - jnp.log1p(jnp.exp(-jnp.abs(f)))
    row = lax.broadcasted_iota(jnp.int32, (tm, tm), 0)
    col = lax.broadcasted_iota(jnp.int32, (tm, tm), 1)
    tri = jnp.where(col <= row, 1.0, 0.0).astype(F32)
    c = jnp.dot(tri, logf, preferred_element_type=F32, precision=HIGHEST) + carry_ref[...]
    carry_ref[...] = c[tm - 1:tm, :]
    aug_q = oq_ref[...]
    aug_k = ok_ref[...]
    for j, piece in enumerate(_split_parts(c)):
        aug_q = aug_q + jnp.dot(piece, pq_ref[j], preferred_element_type=F32)
        aug_k = aug_k - jnp.dot(piece, pk_ref[j], preferred_element_type=F32)
    augq_ref[...] = aug_q.astype(BF16)
    augk_ref[...] = aug_k.astype(BF16)


def _decay(x, w_f, b_f, batch, tm=256):
    n, d = x.shape
    nb = n // batch // tm
    width = H_FOX // 2 * LANES
    place_k, place_q, ones_k, ones_q = _aug_constants()
    const = lambda a: pl.BlockSpec(a.shape, lambda b, i: (0,) * a.ndim)
    out = pl.BlockSpec((tm, width), lambda b, i: (b * nb + i, 0))
    return pl.pallas_call(
        _decay_kernel,
        grid=(batch, nb),
        in_specs=[pl.BlockSpec((tm, d), lambda b, i: (b * nb + i, 0)),
                  pl.BlockSpec((d, LANES), lambda b, i: (0, 0)),
                  pl.BlockSpec((1, LANES), lambda b, i: (0, 0)),
                  const(place_k), const(place_q), const(ones_k), const(ones_q)],
        out_specs=[out, out],
        out_shape=[jax.ShapeDtypeStruct((n, width), BF16)] * 2,
        scratch_shapes=[pltpu.VMEM((1, LANES), F32)],
        compiler_params=_params(("arbitrary", "arbitrary"), 24),
        name="fox_decay",
    )(x, w_f, b_f, jnp.asarray(place_k, BF16), jnp.asarray(place_q, BF16), jnp.asarray(ones_k), jnp.asarray(ones_q))


def _sb_kernel(q_ref, k_ref, v_ref, o_ref):
    t, seg = SB_T, SB_SEG
    nseg = t // seg
    i = pl.program_id(2)
    qh, lane = _head_pair(q_ref[...])
    qneg = [-x for x in qh]
    r = lax.broadcasted_iota(jnp.int32, (seg, seg), 0)
    c = lax.broadcasted_iota(jnp.int32, (seg, seg), 1)
    later = jnp.where(r > c, 1.0, 0.0).astype(BF16)
    row = lax.broadcasted_iota(jnp.int32, (t, t), 0)
    col = lax.broadcasted_iota(jnp.int32, (t, t), 1)
    past = col < row

    def step(j, carry, diagonal):
        start = pl.multiple_of(j * t, t)
        k = k_ref[pl.ds(start, t), :]
        v = v_ref[pl.ds(start, t), :]
        out = []
        for h in range(2):
            tail, acc = carry[h]
            nz = lax.dot_general(qneg[h], k, NT, preferred_element_type=F32)
            minus_abs = lax.bitcast_convert_type(lax.bitcast_convert_type(nz, jnp.uint32) | SIGN_BIT, F32)
            log_1m = jnp.minimum(nz, 0.0) - jnp.log(1.0 + jnp.exp(minus_abs))
            base = log_1m - nz
            if diagonal:
                log_1m = jnp.where(past, log_1m, 0.0)
            sums = [jnp.dot(log_1m[:, b * seg:(b + 1) * seg].astype(BF16), later, preferred_element_type=F32)
                    for b in range(nseg)]
            for b in reversed(range(nseg)):
                sl = slice(b * seg, (b + 1) * seg)
                w = jnp.exp(base[:, sl] + (sums[b] + tail))
                if diagonal:
                    w = jnp.where(past[:, sl], w, 0.0)
                acc = acc + jnp.dot(w.astype(BF16), v[sl], preferred_element_type=F32)
                tail = tail + (sums[b][:, 0:1] + log_1m[:, b * seg:b * seg + 1])
            out.append((tail, acc))
        return tuple(out)

    init = tuple((jnp.zeros((t, 1), F32), jnp.zeros((t, LANES), F32)) for _ in range(2))
    carry = step(i, init, True)

    def live(state):
        jj, carry = state
        return (jj < i) & (jnp.maximum(jnp.max(carry[0][0]), jnp.max(carry[1][0])) > SB_DEAD_LOG)

    def walk(state):
        jj, carry = state
        return jj + 1, step(i - 1 - jj, carry, False)

    _, carry = lax.while_loop(live, walk, (jnp.int32(0), carry))
    o_ref[...] = jnp.where(lane < HEAD_DIM, carry[0][1], carry[1][1]).astype(o_ref.dtype)


def _sb_attention(qkv, batch):
    n = qkv.shape[0]
    seq = n // batch
    nq = seq // SB_T
    pairs = H_SB // 2
    return pl.pallas_call(
        _sb_kernel,
        grid=(batch, pairs, nq),
        in_specs=_attention_specs(seq, nq, SB_T, *_head_cols(MAIN_SB, MAIN_W)),
        out_specs=pl.BlockSpec((SB_T, LANES), lambda b, p, i: (b * nq + i, p)),
        out_shape=jax.ShapeDtypeStruct((n, pairs * LANES), BF16),
        compiler_params=_params(("parallel", "parallel", "arbitrary"), 56),
        name="sb_attention",
    )(qkv, qkv, qkv)


def _fox_kernel(q_ref, k_ref, v_ref, aq_ref, ak_ref, o_ref):
    tq, tk = FOX_TQ, FOX_TK
    i = pl.program_id(2)
    q = q_ref[...]
    lane = lax.broadcasted_iota(jnp.int32, q.shape, 1)
    own = [lane < HEAD_DIM, lane >= HEAD_DIM]
    qh = [jnp.where(own[h], q, aq_ref[...]) for h in range(2)]
    klane = lax.broadcasted_iota(jnp.int32, (tk, LANES), 1)
    kown = [klane < HEAD_DIM, klane >= HEAD_DIM]
    row = lax.broadcasted_iota(jnp.int32, (tq, tk), 0)
    col = lax.broadcasted_iota(jnp.int32, (tq, tk), 1)

    def step(j, carry, diagonal):
        start = pl.multiple_of(j * tk, tk)
        k = k_ref[pl.ds(start, tk), :]
        v = v_ref[pl.ds(start, tk), :]
        ak = ak_ref[pl.ds(start, tk), :]
        out = []
        for h in range(2):
            s = lax.dot_general(qh[h], jnp.where(kown[h], k, ak), NT, preferred_element_type=F32)
            if diagonal:
                s = jnp.where(col + j * tk <= row + i * tq, s, NEG)
            m, acc = carry[h]
            m_new = jnp.maximum(m, jnp.max(s, axis=1, keepdims=True))
            p = jnp.exp(s - m_new).astype(BF16)
            vh = jnp.where(kown[h], v, jnp.ones_like(v))
            acc = jnp.exp(m - m_new) * acc + jnp.dot(p, vh, preferred_element_type=F32)
            out.append((m_new, acc))
        return tuple(out)

    carry = tuple((jnp.full((tq, 1), NEG, F32), jnp.zeros((tq, LANES), F32)) for _ in range(2))
    n_full = (i * tq) // tk
    carry = lax.fori_loop(0, n_full, lambda j, c: step(j, c, False), carry)
    for d in range(max(tq // tk, 1)):
        carry = step(n_full + d, carry, True)
    o = []
    for h in range(2):
        acc = carry[h][1]
        o.append(acc / acc[:, HEAD_DIM * (1 - h):HEAD_DIM * (1 - h) + 1])
    o_ref[...] = jnp.where(own[0], o[0], o[1]).astype(o_ref.dtype)


def _fox_attention(qkv, aug_q, aug_k, batch):
    n = qkv.shape[0]
    seq = n // batch
    nq = seq // FOX_TQ
    pairs = H_FOX // 2
    specs = _attention_specs(seq, nq, FOX_TQ, *_head_cols(MAIN_FOX, MAIN_W))
    specs += [pl.BlockSpec((FOX_TQ, LANES), lambda b, p, i: (b * nq + i, p)),
              pl.BlockSpec((seq, LANES), lambda b, p, i: (b, p))]
    return pl.pallas_call(
        _fox_kernel,
        grid=(batch, pairs, nq),
        in_specs=specs,
        out_specs=pl.BlockSpec((FOX_TQ, LANES), lambda b, p, i: (b * nq + i, p)),
        out_shape=jax.ShapeDtypeStruct((n, pairs * LANES), BF16),
        compiler_params=_params(("parallel", "parallel", "arbitrary"), 48),
        name="fox_attention",
    )(qkv, qkv, qkv, aug_q, aug_k)


def _moba_kernel(q_ref, k_ref, v_ref, onehot_ref, bias_ref, o_ref, kmean_ref):
    t, mb = MOBA_T, MOBA_BLOCK
    nb = t // mb
    seq = k_ref.shape[0]
    i = pl.program_id(2)

    @pl.when(i == 0)
    def _():
        blk = lax.broadcasted_iota(jnp.int32, (LANES, seq), 0)
        pos = lax.broadcasted_iota(jnp.int32, (LANES, seq), 1)
        member = jnp.where((pos >= blk * mb) & (pos < (blk + 1) * mb), 1.0, 0.0).astype(BF16)
        mean = jnp.dot(member, k_ref[...], preferred_element_type=F32) * (1.0 / mb)
        hi = mean.astype(BF16)
        kmean_ref[0] = hi
        kmean_ref[1] = (mean - hi.astype(F32)).astype(BF16)

    q = q_ref[...]
    lane = lax.broadcasted_iota(jnp.int32, (t, LANES), 1)
    own_half = [lane < HEAD_DIM, lane >= HEAD_DIM]
    zero = jnp.zeros_like(q)

    nblk = seq // mb
    blk_row = lax.broadcasted_iota(jnp.int32, (nblk, t), 0).astype(F32)
    own_row = ((i * t + lax.broadcasted_iota(jnp.int32, (nblk, t), 1)) // mb).astype(F32)
    fully_past = blk_row < own_row
    place_r = lax.broadcasted_iota(jnp.int32, (nblk, LANES), 0)
    place_c = lax.broadcasted_iota(jnp.int32, (nblk, LANES), 1)
    qh = []
    for h in range(2):
        q_only = jnp.where(own_half[h], q, zero)
        gate = (lax.dot_general(kmean_ref[0][:nblk], q_only, NT, preferred_element_type=F32)
                + lax.dot_general(kmean_ref[1][:nblk], q_only, NT, preferred_element_type=F32))
        gate = jnp.where(fully_past, gate, -jnp.inf)
        picked = jnp.zeros((nblk, t), jnp.bool_)
        for _ in range(MOBA_TOPK):
            best = jnp.max(gate, axis=0, keepdims=True)
            first = jnp.min(jnp.where(gate == best, blk_row, float(LANES)), axis=0, keepdims=True)
            pick = blk_row == first
            picked = picked | pick
            gate = jnp.where(pick, -jnp.inf, gate)
        allowed = jnp.where((picked & fully_past) | (blk_row == own_row), 1.0, 0.0).astype(BF16)
        place = jnp.where(place_c == place_r + HEAD_DIM * (1 - h), 1.0, 0.0).astype(BF16)
        spread = lax.dot_general(allowed, place, (((0,), (0,)), ((), ())), preferred_element_type=F32)
        penalty = jnp.where(spread > 0.5, 0.0, NEG)
        qh.append(jnp.where(own_half[h], q, penalty.astype(BF16)))

    def step(j, carry):
        start = pl.multiple_of(j * t, t)
        k = k_ref[pl.ds(start, t), :]
        v = v_ref[pl.ds(start, t), :]
        onehot = onehot_ref[pl.ds(start, t), :]
        out = []
        for h in range(2):
            s = lax.dot_general(qh[h], jnp.where(own_half[h], k, onehot), NT, preferred_element_type=F32)
            rows = []
            for a in range(nb):
                tiles = [bias_ref[jnp.clip((i - j) * nb + a - b, 0, MOBA_NT), h] for b in range(nb)]
                rows.append(jnp.concatenate(tiles, axis=1))
            s = s + jnp.concatenate(rows, axis=0)
            m, acc = carry[h]
            m_new = jnp.maximum(m, jnp.max(s, axis=1, keepdims=True))
            p = jnp.exp(s - m_new).astype(BF16)
            vh = jnp.where(own_half[h], v, jnp.ones_like(v))
            acc = jnp.exp(m - m_new) * acc + jnp.dot(p, vh, preferred_element_type=F32)
            out.append((m_new, acc))
        return tuple(out)

    carry = tuple((jnp.full((t, 1), NEG, F32), jnp.zeros((t, LANES), F32)) for _ in range(2))
    carry = lax.fori_loop(0, i + 1, step, carry)
    o = []
    for h in range(2):
        acc = carry[h][1]
        o.append(acc / acc[:, HEAD_DIM * (1 - h):HEAD_DIM * (1 - h) + 1])
    o_ref[...] = jnp.where(own_half[0], o[0], o[1]).astype(o_ref.dtype)


def _moba_attention(qkv, bias_tiles, batch):
    n = qkv.shape[0]
    seq = n // batch
    assert seq % MOBA_T == 0 and seq // MOBA_BLOCK <= HEAD_DIM
    nq = seq // MOBA_T
    pairs = H_MOBA // 2
    blk = np.arange(seq)[:, None] // MOBA_BLOCK
    onehot = jnp.asarray(blk == (np.arange(LANES)[None, :] % HEAD_DIM), BF16)
    specs = _attention_specs(seq, nq, MOBA_T, *_head_cols(MAIN_MOBA, MAIN_W))
    specs.append(pl.BlockSpec((seq, LANES), lambda b, p, i: (0, 0)))
    specs.append(pl.BlockSpec((None, MOBA_NT + 1, 2, MOBA_BLOCK, MOBA_BLOCK), lambda b, p, i: (p, 0, 0, 0, 0)))
    return pl.pallas_call(
        _moba_kernel,
        grid=(batch, pairs, nq),
        in_specs=specs,
        out_specs=pl.BlockSpec((MOBA_T, LANES), lambda b, p, i: (b * nq + i, p)),
        out_shape=jax.ShapeDtypeStruct((n, pairs * LANES), BF16),
        scratch_shapes=[pltpu.VMEM((2, LANES, LANES), BF16)],
        compiler_params=_params(("arbitrary", "arbitrary", "arbitrary"), 56),
        name="moba_attention",
    )(qkv, qkv, qkv, onehot, bias_tiles)


def _moba_bias_tiles(rel_bias):
    r = jnp.arange(MOBA_BLOCK)
    dist = (jnp.arange(MOBA_NT + 1)[:, None, None] * MOBA_BLOCK + r[None, :, None] - r[None, None, :])
    tiles = _bias_lookup(rel_bias[:, H_DIL:], _t5_bucket(jnp.maximum(dist, 0)))
    tiles = jnp.where(dist >= 0, tiles, NEG)
    tiles = tiles.reshape(H_MOBA // 2, 2, MOBA_NT + 1, MOBA_BLOCK, MOBA_BLOCK)
    return tiles.transpose(0, 2, 1, 3, 4).astype(F32)


def _check_moba_saturation():
    n = np.float32(MOBA_NT * MOBA_BLOCK - (MOBA_BLOCK - 1))
    exact = N_BUCKETS // 2
    large = exact + int(np.log(n / np.float32(exact)) / math.log(MAX_DIST / exact) * (N_BUCKETS - exact))
    assert large >= N_BUCKETS - 1, "MOBA_NT too small for the bias bucket table"


_check_moba_saturation()


def _dilated_kernel(q_ref, k_ref, v_ref, bias_ref, o_ref, lse_ref):
    w = DIL_W
    nb = q_ref.shape[0] // w
    row = lax.broadcasted_iota(jnp.int32, (w, w), 0)
    col = lax.broadcasted_iota(jnp.int32, (w, w), 1)
    in_cur = col <= row
    in_prev = col >= row

    def body(n, _):
        cur = pl.multiple_of(n * w, w)
        prev = pl.multiple_of(jnp.maximum(n - 1, 0) * w, w)
        qh, lane = _head_pair(q_ref[pl.ds(cur, w), :])
        k_cur, v_cur = k_ref[pl.ds(cur, w), :], v_ref[pl.ds(cur, w), :]
        k_prev, v_prev = k_ref[pl.ds(prev, w), :], v_ref[pl.ds(prev, w), :]
        has_prev = n > 0
        outs, lses = [], []
        for h in range(2):
            z_prev = lax.dot_general(qh[h], k_prev, NT, preferred_element_type=F32) + bias_ref[h, :, 0:w]
            z_cur = lax.dot_general(qh[h], k_cur, NT, preferred_element_type=F32) + bias_ref[h, :, w:2 * w]
            z_prev = jnp.where(in_prev & has_prev, z_prev, NEG)
            z_cur = jnp.where(in_cur, z_cur, NEG)
            m = jnp.maximum(jnp.max(z_prev, axis=1, keepdims=True), jnp.max(z_cur, axis=1, keepdims=True))
            p_prev = jnp.exp(z_prev - m)
            p_cur = jnp.exp(z_cur - m)
            l = jnp.sum(p_prev, axis=1, keepdims=True) + jnp.sum(p_cur, axis=1, keepdims=True)
            o = (jnp.dot(p_prev.astype(BF16), v_prev, preferred_element_type=F32)
                 + jnp.dot(p_cur.astype(BF16), v_cur, preferred_element_type=F32))
            outs.append(o / l)
            lses.append(m + jnp.log(l))
        o_ref[pl.ds(cur, w), :] = jnp.where(lane < HEAD_DIM, outs[0], outs[1])
        lse_ref[pl.ds(cur, w), :] = jnp.where(lane < HEAD_DIM, lses[0], lses[1])
        return 0

    lax.fori_loop(0, nb, body, 0)


def _dilated_group(qkv, head_off, width, bias, batch, group):
    window, dil = DIL_GROUPS[group]
    assert window // dil == DIL_W
    n = qkv.shape[0]
    sub = n // batch // dil
    assert sub % DIL_W == 0
    pairs = H_DIL_PER_GROUP // 2
    q_col, k_col, v_col = _head_cols(head_off, width)

    def spec(col0):
        return pl.BlockSpec((sub, LANES), lambda b, p, r: (b * dil + r, col0 + p))

    out_spec = pl.BlockSpec((sub, LANES), lambda b, p, r: (b * dil + r, p))
    out_shape = jax.ShapeDtypeStruct((n, pairs * LANES), F32)
    return pl.pallas_call(
        _dilated_kernel,
        grid=(batch, pairs, dil),
        in_specs=[spec(q_col), spec(k_col), spec(v_col),
                  pl.BlockSpec((None, 2, DIL_W, 2 * DIL_W), lambda b, p, r: (group * pairs + p, 0, 0, 0))],
        out_specs=[out_spec, out_spec],
        out_shape=[out_shape, out_shape],
        compiler_params=_params(("parallel", "parallel", "parallel"), 48),
        name=f"dilated_attention_g{group}",
    )(qkv, qkv, qkv, bias)


def _by_residue(a, batch, dil):
    n, c = a.shape
    return a.reshape(batch, n // batch // dil, dil, c).transpose(0, 2, 1, 3).reshape(n, c)


def _by_position(a, batch, dil):
    n, c = a.shape
    return a.reshape(batch, dil, n // batch // dil, c).transpose(0, 2, 1, 3).reshape(n, c)


def _dilated_bias_tiles(rel_bias):
    steps = jnp.arange(DIL_W)[:, None] + DIL_W - jnp.arange(2 * DIL_W)[None, :]
    tiles = []
    for g, (_, dil) in enumerate(DIL_GROUPS):
        tab = rel_bias[:, g * H_DIL_PER_GROUP:(g + 1) * H_DIL_PER_GROUP]
        tiles.append(_bias_lookup(tab, _t5_bucket(jnp.maximum(steps, 0) * dil)))
    tiles = jnp.stack(tiles, 0).astype(F32)
    return tiles.reshape(len(DIL_GROUPS) * H_DIL_PER_GROUP // 2, 2, DIL_W, 2 * DIL_W)


def _dilated_combine_kernel(o0, o1, o2, l0, l1, l2, out_ref):
    lse = [l0[...], l1[...], l2[...]]
    m = jnp.maximum(jnp.maximum(lse[0], lse[1]), lse[2])
    e = [jnp.exp(x - m) for x in lse]
    den = e[0] + e[1] + e[2]
    out = (e[0] / den) * o0[...] + (e[1] / den) * o1[...] + (e[2] / den) * o2[...]
    out_ref[...] = out.astype(out_ref.dtype)


def _dilated_combine(os, lses, tm=1024):
    n, c = os[0].shape
    spec = pl.BlockSpec((tm, c), lambda i: (i, 0))
    return pl.pallas_call(
        _dilated_combine_kernel,
        grid=(n // tm,),
        in_specs=[spec] * 6,
        out_specs=spec,
        out_shape=jax.ShapeDtypeStruct((n, c), BF16),
        compiler_params=_params(("parallel",), 32),
        name="dilated_combine",
    )(*os, *lses)


def _merge_kernel(x_ref, o_sb, o_dil, o_moba, o_fox, wg0, wg1, wg2, wg3, bg_ref,
                  w_sb, w_dil, w_moba, w_fox, out_ref):
    x = x_ref[...]
    acc = None
    branches = ((o_sb, w_sb, wg0), (o_dil, w_dil, wg1), (o_moba, w_moba, wg2), (o_fox, w_fox, wg3))
    for b, (o_ref, w_ref, wg_ref) in enumerate(branches):
        gate = jax.nn.sigmoid(jnp.dot(x, wg_ref[...], preferred_element_type=F32) + bg_ref[b])
        term = gate * jnp.dot(o_ref[...], w_ref[...], preferred_element_type=F32)
        acc = term if acc is None else acc + term
    out_ref[...] = acc.astype(out_ref.dtype)


def _gated_merge(x_bf, outs, w_gate, b_gate, w_branch, layer, tm=512, tn=512):
    n, d = x_bf.shape
    nj = d // tn
    row = lambda width: pl.BlockSpec((tm, width), lambda j, i: (i, 0))
    in_specs = [row(d)] + [row(o.shape[1]) for o in outs]
    in_specs += [pl.BlockSpec((None, d, tn), functools.partial(lambda j, i, b: (layer, 0, b * nj + j), b=b))
                 for b in range(N_BRANCH)]
    in_specs.append(pl.BlockSpec((None, N_BRANCH, 1, tn), lambda j, i: (layer, 0, 0, j)))
    in_specs += [pl.BlockSpec((None, w.shape[1], tn), lambda j, i: (layer, 0, j)) for w in w_branch]
    return pl.pallas_call(
        _merge_kernel,
        grid=(nj, n // tm),
        in_specs=in_specs,
        out_specs=pl.BlockSpec((tm, tn), lambda j, i: (i, j)),
        out_shape=jax.ShapeDtypeStruct((n, d), BF16),
        compiler_params=_params(("parallel", "parallel"), 48),
        name="gated_merge",
    )(x_bf, *outs, w_gate, w_gate, w_gate, w_gate, b_gate, *w_branch)


def _layer_norm_store(z, g_ref, b_ref, xo_ref, xb_ref):
    mu = jnp.mean(z, axis=1, keepdims=True)
    zc = z - mu
    var = jnp.mean(zc * zc, axis=1, keepdims=True)
    out = zc * lax.rsqrt(var + LN_EPS) * g_ref[...] + b_ref[...]
    xo_ref[...] = out
    xb_ref[...] = out.astype(BF16)


def _out_ln_kernel(m_ref, w_ref, x_ref, g_ref, b_ref, xo_ref, xb_ref):
    y = jnp.dot(m_ref[...], w_ref[...], preferred_element_type=F32)
    _layer_norm_store(DN_ALPHA * x_ref[...] + y, g_ref, b_ref, xo_ref, xb_ref)


def _combine_ln_kernel(y_ref, gate_ref, x_ref, g_ref, b_ref, xo_ref, xb_ref):
    gate = gate_ref[...]
    z = DN_ALPHA * x_ref[...]
    for k in range(TOP_K):
        z = z + gate[:, k:k + 1] * y_ref[k].astype(F32)
    _layer_norm_store(z, g_ref, b_ref, xo_ref, xb_ref)


def _ln_call(kernel, lead_specs, lead_args, x, g, b, tm, name):
    n, d = x.shape
    row = pl.BlockSpec((tm, d), lambda i: (i, 0))
    vec = pl.BlockSpec((1, d), lambda i: (0, 0))
    return pl.pallas_call(
        kernel,
        grid=(n // tm,),
        in_specs=lead_specs + [row, vec, vec],
        out_specs=[row, row],
        out_shape=[jax.ShapeDtypeStruct((n, d), F32), jax.ShapeDtypeStruct((n, d), BF16)],
        compiler_params=_params(("parallel",), 48),
        name=name,
    )(*lead_args, x, g, b)


def _out_proj_ln(merged, w_out, layer, x, g, b, tm=256):
    d = x.shape[1]
    specs = [pl.BlockSpec((tm, d), lambda i: (i, 0)), pl.BlockSpec((None, d, d), lambda i: (layer, 0, 0))]
    return _ln_call(_out_ln_kernel, specs, (merged, w_out), x, g, b, tm, "out_proj_ln")


def _combine_ln(yk, gate, x, g, b, tm=256):
    d = x.shape[1]
    specs = [pl.BlockSpec((TOP_K, tm, d), lambda i: (0, i, 0)), pl.BlockSpec((tm, LANES), lambda i: (i, 0))]
    return _ln_call(_combine_ln_kernel, specs, (yk, gate), x, g, b, tm, "moe_combine_ln")


def _router_kernel(x_ref, w_ref, b_ref, idx_ref, gate_ref, rank_ref, count_ref, seen_ref):
    @pl.when(pl.program_id(0) == 0)
    def _():
        seen_ref[...] = jnp.zeros_like(seen_ref)

    tm = x_ref.shape[0]
    logits = jnp.dot(x_ref[...], w_ref[...], preferred_element_type=F32, precision=HIGHEST) + b_ref[...]
    col = lax.broadcasted_iota(jnp.int32, logits.shape, 1)
    colf = col.astype(F32)
    logits = jnp.where(col < N_EXPERTS, logits, -jnp.inf)
    vals, idxs, hits = [], [], []
    for _ in range(TOP_K):
        best = jnp.max(logits, axis=1, keepdims=True)
        first = jnp.min(jnp.where(logits == best, colf, float(LANES)), axis=1, keepdims=True)
        hit = colf == first
        logits = jnp.where(hit, -jnp.inf, logits)
        vals.append(best)
        idxs.append(first)
        hits.append(jnp.where(hit, 1.0, 0.0))
    es = [jnp.exp(v - vals[0]) for v in vals]
    den = es[0] + es[1] + es[2] + es[3]

    chosen = hits[0] + hits[1] + hits[2] + hits[3]
    row = lax.broadcasted_iota(jnp.int32, (tm, tm), 0)
    before = jnp.where(lax.broadcasted_iota(jnp.int32, (tm, tm), 1) < row, 1.0, 0.0).astype(BF16)
    earlier = jnp.dot(before, chosen.astype(BF16), preferred_element_type=F32) + seen_ref[...]
    seen = seen_ref[...] + jnp.sum(chosen, axis=0, keepdims=True)
    seen_ref[...] = seen
    count_ref[...] = seen

    idx_out = jnp.zeros(logits.shape, F32)
    gate_out = jnp.zeros(logits.shape, F32)
    rank_out = jnp.zeros(logits.shape, F32)
    for r in range(TOP_K):
        idx_out = jnp.where(col == r, idxs[r], idx_out)
        gate_out = jnp.where(col == r, es[r] / den, gate_out)
        rank_out = jnp.where(col == r, jnp.sum(hits[r] * earlier, axis=1, keepdims=True), rank_out)
    idx_ref[...] = idx_out.astype(jnp.int32)
    gate_ref[...] = gate_out
    rank_ref[...] = rank_out.astype(jnp.int32)


def _router(x, w_r, b_r, tm=512):
    n, d = x.shape
    out = pl.BlockSpec((tm, LANES), lambda i: (i, 0))
    return pl.pallas_call(
        _router_kernel,
        grid=(n // tm,),
        in_specs=[pl.BlockSpec((tm, d), lambda i: (i, 0)),
                  pl.BlockSpec((d, LANES), lambda i: (0, 0)),
                  pl.BlockSpec((1, LANES), lambda i: (0, 0))],
        out_specs=[out, out, out, pl.BlockSpec((1, LANES), lambda i: (0, 0))],
        out_shape=[jax.ShapeDtypeStruct((n, LANES), jnp.int32), jax.ShapeDtypeStruct((n, LANES), F32),
                   jax.ShapeDtypeStruct((n, LANES), jnp.int32), jax.ShapeDtypeStruct((1, LANES), F32)],
        scratch_shapes=[pltpu.VMEM((1, LANES), F32)],
        compiler_params=_params(("arbitrary",), 32),
        name="moe_router",
    )(x, w_r, b_r)


def _expert_kernel(blk_exp_ref, xs_ref, wu_ref, bu_ref, wd_ref, bd_ref, y_ref):
    del blk_exp_ref
    h = jnp.dot(xs_ref[...], wu_ref[...].astype(BF16), preferred_element_type=F32) + bu_ref[...]
    glu = jnp.minimum(h[:, :D_FF], SWIGLU_LIMIT)
    lin = jnp.clip(h[:, D_FF:], -SWIGLU_LIMIT, SWIGLU_LIMIT)
    act = glu * jax.nn.sigmoid(SWIGLU_ALPHA * glu) * (lin + 1.0)
    y = jnp.dot(act.astype(BF16), wd_ref[...].astype(BF16), preferred_element_type=F32) + bd_ref[...]
    y_ref[...] = y.astype(y_ref.dtype)


def _expert_ffn(xs, blk_exp, w_up, b_up, w_down, b_down, layer):
    p, d = xs.shape
    grid_spec = pltpu.PrefetchScalarGridSpec(
        num_scalar_prefetch=1,
        grid=(p // MOE_ROWS,),
        in_specs=[pl.BlockSpec((MOE_ROWS, d), lambda i, e: (i, 0)),
                  pl.BlockSpec((None, None, d, 2 * D_FF), lambda i, e: (layer, e[i], 0, 0)),
                  pl.BlockSpec((None, None, 1, 2 * D_FF), lambda i, e: (layer, e[i], 0, 0)),
                  pl.BlockSpec((None, None, D_FF, d), lambda i, e: (layer, e[i], 0, 0)),
                  pl.BlockSpec((None, None, 1, d), lambda i, e: (layer, e[i], 0, 0))],
        out_specs=pl.BlockSpec((MOE_ROWS, d), lambda i, e: (i, 0)),
    )
    return pl.pallas_call(
        _expert_kernel,
        grid_spec=grid_spec,
        out_shape=jax.ShapeDtypeStruct((p, d), BF16),
        compiler_params=_params(("arbitrary",), 48),
        name="expert_ffn",
    )(blk_exp, xs, w_up, b_up, w_down, b_down)


def _moe(x, x_bf, w_r, b_r, w_up, b_up, w_down, b_down, layer):
    n, d = x.shape
    nk = n * TOP_K
    idx, gate, rank, count = _router(x, w_r, b_r)
    counts = count[0, :N_EXPERTS].astype(jnp.int32)
    start = jnp.cumsum(counts) - counts
    padded = (counts + MOE_ROWS - 1) // MOE_ROWS * MOE_ROWS
    pend = jnp.cumsum(padded)
    pstart = pend - padded
    experts = jnp.arange(N_EXPERTS, dtype=jnp.int32)
    chosen = idx[:, :TOP_K, None] == experts
    slot = jnp.sum(jnp.where(chosen, pstart, 0), axis=2) + rank[:, :TOP_K]
    n_blocks = -(-(nk + N_EXPERTS * (MOE_ROWS - 1)) // MOE_ROWS)
    first_row = jnp.arange(n_blocks, dtype=jnp.int32) * MOE_ROWS
    blk_exp = jnp.minimum(jnp.sum(pend[None, :] <= first_row[:, None], axis=1), N_EXPERTS - 1).astype(jnp.int32)
    order = jnp.argsort(slot.reshape(nk)).astype(jnp.int32)
    blk_is = blk_exp[:, None] == experts
    blk_start = jnp.sum(jnp.where(blk_is, start, 0), axis=1)
    blk_pstart = jnp.sum(jnp.where(blk_is, pstart, 0), axis=1)
    blk_count = jnp.sum(jnp.where(blk_is, counts, 0), axis=1)
    within = first_row[:, None] + jnp.arange(MOE_ROWS, dtype=jnp.int32)[None, :] - blk_pstart[:, None]
    pair = jnp.clip(blk_start[:, None] + within, 0, nk - 1)
    spare = (first_row[:, None] + jnp.arange(MOE_ROWS, dtype=jnp.int32)[None, :]) % n
    slot_tok = jnp.where(within < blk_count[:, None], order[pair] // TOP_K, spare).reshape(n_blocks * MOE_ROWS)
    slot_tok, slot_km = lax.optimization_barrier((slot_tok, slot.T.reshape(nk)))
    xs = x_bf[slot_tok]
    y = _expert_ffn(xs, blk_exp, w_up, b_up, w_down, b_down, layer)
    return y[slot_km].reshape(TOP_K, n, d), gate


def kernel(x, rel_bias, w_in, b_forget, b_gate, w_br_sb, w_br_dil, w_br_moba, w_br_fox, w_out,
           ln1_g, ln1_b, w_router, b_router, w_up, b_up, w_down, b_down, ln2_g, ln2_b):
    batch, seq, d = x.shape
    n = batch * seq
    depth = w_in.shape[0]

    w_qkv = _qkv_weights(w_in)
    w_f =jnp.pad(w_in[:, :, C_QKV:C_QKV + H_FOX], ((0, 0), (0, 0), (0, LANES - H_FOX)))
    b_f = jnp.pad(b_forget, ((0, 0), (0, LANES - H_FOX))).reshape(depth, 1, LANES)
    w_gate = _gate_weights(w_in)
    b_gate3 = b_gate.reshape(depth, N_BRANCH, 1, d)
    w_branch = [w.astype(BF16) for w in (w_br_sb, w_br_dil, w_br_moba, w_br_fox)]
    w_out_bf = w_out.astype(BF16)
    w_r = jnp.pad(w_router, ((0, 0), (0, 0), (0, LANES - N_EXPERTS)))
    b_r = jnp.pad(b_router, ((0, 0), (0, LANES - N_EXPERTS))).reshape(depth, 1, LANES)
    b_up3 = b_up.reshape(depth, N_EXPERTS, 1, 2 * D_FF)
    b_down3 = b_down.reshape(depth, N_EXPERTS, 1, d)
    moba_bias = _moba_bias_tiles(rel_bias)
    dil_bias = _dilated_bias_tiles(rel_bias)

    xf = x.reshape(n, d)
    xb = xf.astype(BF16)
    for l in range(depth):
        qkv = _matmul(xb, w_qkv, l, 0, 3 * MAIN_W, 1024, 768, BF16, "qkv_proj")
        aug_q, aug_k = _decay(xf, w_f[l], b_f[l], batch)
        o_sb = _sb_attention(qkv, batch)
        os, lses = [], []
        o, lse = _dilated_group(qkv, MAIN_DIL, MAIN_W, dil_bias, batch, 0)
        os.append(o)
        lses.append(lse)
        for g in range(1, len(DIL_GROUPS)):
            dil = DIL_GROUPS[g][1]
            qkv_g = _matmul(xb, w_qkv, l, 3 * (MAIN_W + (g - 1) * DIL_GROUP_W),
                            3 * DIL_GROUP_W, 1024, 3 * DIL_GROUP_W, BF16, f"qkv_proj_g{g}")
            o, lse = _dilated_group(_by_residue(qkv_g, batch, dil), 0, DIL_GROUP_W, dil_bias, batch, g)
            os.append(_by_position(o, batch, dil))
            lses.append(_by_position(lse, batch, dil))
        o_dil = _dilated_combine(os, lses)
        o_moba = _moba_attention(qkv, moba_bias, batch)
        o_fox = _fox_attention(qkv, aug_q, aug_k, batch)
        merged = _gated_merge(xb, (o_sb, o_dil, o_moba, o_fox), w_gate, b_gate3, w_branch, l)
        xf, xb = _out_proj_ln(merged, w_out_bf, l, xf, ln1_g[l].reshape(1, d), ln1_b[l].reshape(1, d))
        yk, gate = _moe(xf, xb, w_r[l], b_r[l], w_up, b_up3, w_down, b_down3, l)
        xf, xb = _combine_ln(yk, gate, xf, ln2_g[l].reshape(1, d), ln2_b[l].reshape(1, d))
    return xf.reshape(batch, seq, d)
```

```python
import functools
import math

import numpy as np
import jax
import jax.numpy as jnp
from jax import lax
from jax.experimental import pallas as pl
from jax.experimental.pallas import tpu as pltpu

D_MODEL = 2048
DEPTH = 4
HEAD_DIM = 64
H_SB = 8
H_DIL_PER_GROUP = 4
DIL_GROUPS = ((128, 1), (512, 4), (2048, 16))
H_DIL = H_DIL_PER_GROUP * len(DIL_GROUPS)
H_MOBA = 8
H_FOX = 8
H_ALL = H_SB + H_DIL + H_MOBA + H_FOX
SB_OFF = 0
DIL_OFF = SB_OFF + H_SB
MOBA_OFF = DIL_OFF + H_DIL
FOX_OFF = MOBA_OFF + H_MOBA
W_MIX = H_ALL * HEAD_DIM
N_BRANCH = 4
C_QKV = 3 * W_MIX
MOBA_BLOCK = 256
MOBA_TOPK = 3
N_BUCKETS = 32
MAX_DIST = 2048
N_EXPERTS = 32
TOP_K = 4
D_FF = 768
SWIGLU_ALPHA = 1.702
SWIGLU_LIMIT = 7.0
LN_EPS = 1e-5
DN_ALPHA = (2 * DEPTH) ** 0.25

F32 = jnp.float32
BF16 = jnp.bfloat16
LANES = 128
NEG = -1e30
HIGHEST = lax.Precision.HIGHEST
MIB = 1024 * 1024
NT = (((1,), (1,)), ((), ()))
SIGN_BIT = np.uint32(0x80000000)

MAIN_SB = 0
MAIN_DIL = MAIN_SB + H_SB
MAIN_MOBA = MAIN_DIL + H_DIL_PER_GROUP
MAIN_FOX = MAIN_MOBA + H_MOBA
MAIN_W = (MAIN_FOX + H_FOX) * HEAD_DIM
DIL_GROUP_W = H_DIL_PER_GROUP * HEAD_DIM
DIL_W = 128
DIL_TQ = 256
MOE_ROWS = 256
SB_T = 1024
SB_SEG = 256
SB_DEAD_LOG = -110.0
FOX_TQ = 1024
FOX_TK = 1024
MOBA_T = 1024
MOBA_NT = 8
AUG_PARTS = 3


def _params(sem, vmem_mib):
    return pltpu.CompilerParams(dimension_semantics=sem, vmem_limit_bytes=vmem_mib * MIB)


def _t5_bucket(n):
    exact = N_BUCKETS // 2
    nf = jnp.maximum(n, 1).astype(F32)
    large = exact + (jnp.log(nf / exact) / math.log(MAX_DIST / exact) * (N_BUCKETS - exact)).astype(jnp.int32)
    large = jnp.minimum(large, N_BUCKETS - 1)
    return jnp.where(n < exact, n, large)


def _bias_lookup(table, bucket):
    onehot = (bucket[..., None] == jnp.arange(N_BUCKETS)).astype(F32)
    return jnp.einsum('...k,kh->h...', onehot, table.astype(F32), precision=HIGHEST)


def _head_pair(q):
    lane = lax.broadcasted_iota(jnp.int32, q.shape, 1)
    zero = jnp.zeros_like(q)
    return [jnp.where(lane < HEAD_DIM, q, zero), jnp.where(lane >= HEAD_DIM, q, zero)], lane


def _attention_specs(seq, nq, tq, q_col, k_col, v_col):
    return [pl.BlockSpec((tq, LANES), lambda b, p, i: (b * nq + i, q_col + p)),
            pl.BlockSpec((seq, LANES), lambda b, p, i: (b, k_col + p)),
            pl.BlockSpec((seq, LANES), lambda b, p, i: (b, v_col + p))]


def _head_cols(off, width):
    q_col = off * HEAD_DIM // LANES
    return q_col, q_col + width // LANES, q_col + 2 * width // LANES


def _matmul_kernel(a_ref, w_ref, o_ref):
    o_ref[...] = jnp.dot(a_ref[...], w_ref[...], preferred_element_type=F32).astype(o_ref.dtype)


def _matmul(a, w, layer, col0, n, tm, tn, out_dtype, name):
    m, k = a.shape
    first = col0 // tn
    return pl.pallas_call(
        _matmul_kernel,
        grid=(n // tn, m // tm),
        in_specs=[pl.BlockSpec((tm, k), lambda j, i: (i, 0)),
                  pl.BlockSpec((None, k, tn), lambda j, i: (layer, 0, first + j))],
        out_specs=pl.BlockSpec((tm, tn), lambda j, i: (i, j)),
        out_shape=jax.ShapeDtypeStruct((m, n), out_dtype),
        compiler_params=_params(("parallel", "parallel"), 40),
        name=name,
    )(a, w)


QKV_TILE = 256
MAIN_HEADS = ((SB_OFF, H_SB), (DIL_OFF, H_DIL_PER_GROUP), (MOBA_OFF, H_MOBA), (FOX_OFF, H_FOX))


def _qkv_column_plan():
    groups = [MAIN_HEADS] + [((DIL_OFF + g * H_DIL_PER_GROUP, H_DIL_PER_GROUP),) for g in range(1, len(DIL_GROUPS))]
    src, is_q = [], []
    for heads in groups:
        for sec in range(3):
            for lo, cnt in heads:
                start = sec * W_MIX + lo * HEAD_DIM
                assert start % QKV_TILE == 0 and (cnt * HEAD_DIM) % QKV_TILE == 0
                for b in range(cnt * HEAD_DIM // QKV_TILE):
                    src.append(start // QKV_TILE + b)
                    is_q.append(int(sec == 0))
    assert sorted(src) == list(range(C_QKV // QKV_TILE))
    return np.asarray(src, np.int32), np.asarray(is_q, np.int32)


def _qkv_weight_kernel(src_ref, is_q_ref, w_ref, o_ref):
    del src_ref
    scale = jnp.where(is_q_ref[pl.program_id(1)] == 1, HEAD_DIM ** -0.5, 1.0)
    o_ref[...] = (w_ref[...] * scale).astype(BF16)


def _qkv_weights(w_in):
    depth, d, _ = w_in.shape
    src, is_q = _qkv_column_plan()
    grid_spec = pltpu.PrefetchScalarGridSpec(
        num_scalar_prefetch=2,
        grid=(depth, len(src)),
        in_specs=[pl.BlockSpec((None, d, QKV_TILE), lambda l, j, src, is_q: (l, 0, src[j]))],
        out_specs=pl.BlockSpec((None, d, QKV_TILE), lambda l, j, src, is_q: (l, 0, j)),
    )
    return pl.pallas_call(
        _qkv_weight_kernel,
        grid_spec=grid_spec,
        out_shape=jax.ShapeDtypeStruct((depth, d, C_QKV), BF16),
        compiler_params=_params(("parallel", "parallel"), 24),
        name="qkv_weights",
    )(jnp.asarray(src), jnp.asarray(is_q), w_in)


GATE_COL0 = C_QKV + H_FOX
GATE_ALIGNED = GATE_COL0 // LANES * LANES
GATE_TILE = 256


def _gate_weight_kernel(a_ref, b_ref, o_ref):
    shift = GATE_COL0 - GATE_ALIGNED
    both = jnp.concatenate([a_ref[...], b_ref[...]], axis=1)
    o_ref[...] = both[:, shift:shift + GATE_TILE].astype(BF16)


def _gate_weights(w_in):
    depth, d, _ = w_in.shape
    width = N_BRANCH * d
    assert GATE_ALIGNED % GATE_TILE == 0 and width % GATE_TILE == 0
    a0 = GATE_ALIGNED // GATE_TILE
    per = GATE_TILE // LANES
    return pl.pallas_call(
        _gate_weight_kernel,
        grid=(depth, width // GATE_TILE),
        in_specs=[pl.BlockSpec((None, d, GATE_TILE), lambda l, j: (l, 0, a0 + j)),
                  pl.BlockSpec((None, d, LANES), lambda l, j: (l, 0, (a0 + j + 1) * per))],
        out_specs=pl.BlockSpec((None, d, GATE_TILE), lambda l, j: (l, 0, j)),
        out_shape=jax.ShapeDtypeStruct((depth, d, width), BF16),
        compiler_params=_params(("parallel", "parallel"), 24),
        name="gate_weights",
    )(w_in, w_in)


def _aug_constants():
    pairs = H_FOX // 2
    place_k = np.zeros((AUG_PARTS, LANES, pairs * LANES), np.float32)
    place_q = np.zeros((AUG_PARTS, LANES, pairs * LANES), np.float32)
    ones_k = np.zeros((1, pairs * LANES), np.float32)
    ones_q = np.zeros((1, pairs * LANES), np.float32)
    for g in range(H_FOX):
        p, h = divmod(g, 2)
        base = p * LANES + HEAD_DIM * (1 - h)
        for j in range(AUG_PARTS):
            place_k[j, g, base + j] = 1.0
            place_q[j, g, base + AUG_PARTS + j] = 1.0
            ones_k[0, base + AUG_PARTS + j] = 1.0
            ones_q[0, base + j] = 1.0
    return place_k, place_q, ones_k, ones_q


def _split_parts(x):
    parts = []
    for _ in range(AUG_PARTS):
        piece = x.astype(BF16)
        parts.append(piece)
        x = x - piece.astype(F32)
    return parts


def _decay_kernel(x_ref, whi_ref, wlo_ref, bf_ref, pk_ref, pq_ref, ok_ref, oq_ref, augq_ref, augk_ref, carry_ref):
    @pl.when(pl.program_id(1) == 0)
    def _():
        carry_ref[...] = jnp.zeros_like(carry_ref)

    tm = x_ref.shape[0]
    x = x_ref[...]
    f = (jnp.dot(x, whi_ref[...], preferred_element_type=F32)
         + jnp.dot(x, wlo_ref[...], preferred_element_type=F32) + bf_ref[...])
    logf = jnp.minimum(f, 0.0) - jnp.log1p(jnp.exp(-jnp.abs(f)))
    row = lax.broadcasted_iota(jnp.int32, (tm, tm), 0)
    col = lax.broadcasted_iota(jnp.int32, (tm, tm), 1)
    tri = jnp.where(col <= row, 1.0, 0.0).astype(F32)
    c = jnp.dot(tri, logf, preferred_element_type=F32, precision=HIGHEST) + carry_ref[...]
    carry_ref[...] = c[tm - 1:tm, :]
    aug_q = oq_ref[...]
    aug_k = ok_ref[...]
    for j, piece in enumerate(_split_parts(c)):
        aug_q = aug_q + jnp.dot(piece, pq_ref[j], preferred_element_type=F32)
        aug_k = aug_k - jnp.dot(piece, pk_ref[j], preferred_element_type=F32)
    augq_ref[...] = aug_q.astype(BF16)
    augk_ref[...] = aug_k.astype(BF16)


def _decay(x, w_f, b_f, batch, tm=256):
    n, d = x.shape
    nb = n // batch // tm
    width = H_FOX // 2 * LANES
    place_k, place_q, ones_k, ones_q = _aug_constants()
    w_hi = w_f.astype(BF16)
    w_lo = (w_f - w_hi.astype(F32)).astype(BF16)
    const = lambda a: pl.BlockSpec(a.shape, lambda b, i: (0,) * a.ndim)
    out = pl.BlockSpec((tm, width), lambda b, i: (b * nb + i, 0))
    return pl.pallas_call(
        _decay_kernel,
        grid=(batch, nb),
        in_specs=[pl.BlockSpec((tm, d), lambda b, i: (b * nb + i, 0)),
                  pl.BlockSpec((d, LANES), lambda b, i: (0, 0)),
                  pl.BlockSpec((d, LANES), lambda b, i: (0, 0)),
                  pl.BlockSpec((1, LANES), lambda b, i: (0, 0)),
                  const(place_k), const(place_q), const(ones_k), const(ones_q)],
        out_specs=[out, out],
        out_shape=[jax.ShapeDtypeStruct((n, width), BF16)] * 2,
        scratch_shapes=[pltpu.VMEM((1, LANES), F32)],
        compiler_params=_params(("arbitrary", "arbitrary"), 24),
        name="fox_decay",
    )(x, w_hi, w_lo, b_f, jnp.asarray(place_k, BF16), jnp.asarray(place_q, BF16), jnp.asarray(ones_k),
      jnp.asarray(ones_q))


def _sb_kernel(q_ref, k_ref, v_ref, o_ref):
    t, seg = SB_T, SB_SEG
    nseg = t // seg
    i = pl.program_id(2)
    qh, lane = _head_pair(q_ref[...])
    qneg = [-x for x in qh]
    r = lax.broadcasted_iota(jnp.int32, (seg, seg), 0)
    c = lax.broadcasted_iota(jnp.int32, (seg, seg), 1)
    later = jnp.where(r > c, 1.0, 0.0).astype(BF16)
    row = lax.broadcasted_iota(jnp.int32, (t, t), 0)
    col = lax.broadcasted_iota(jnp.int32, (t, t), 1)
    past = col < row

    def step(j, carry, diagonal):
        start = pl.multiple_of(j * t, t)
        k = k_ref[pl.ds(start, t), :]
        v = v_ref[pl.ds(start, t), :]
        out = []
        for h in range(2):
            tail, acc = carry[h]
            nz = lax.dot_general(qneg[h], k, NT, preferred_element_type=F32)
            minus_abs = lax.bitcast_convert_type(lax.bitcast_convert_type(nz, jnp.uint32) | SIGN_BIT, F32)
            log_1m = jnp.minimum(nz, 0.0) - jnp.log(1.0 + jnp.exp(minus_abs))
            base = log_1m - nz
            if diagonal:
                log_1m = jnp.where(past, log_1m, 0.0)
            sums = [jnp.dot(log_1m[:, b * seg:(b + 1) * seg].astype(BF16), later, preferred_element_type=F32)
                    for b in range(nseg)]
            for b in reversed(range(nseg)):
                sl = slice(b * seg, (b + 1) * seg)
                w = jnp.exp(base[:, sl] + (sums[b] + tail))
                if diagonal:
                    w = jnp.where(past[:, sl], w, 0.0)
                acc = acc + jnp.dot(w.astype(BF16), v[sl], preferred_element_type=F32)
                tail = tail + (sums[b][:, 0:1] + log_1m[:, b * seg:b * seg + 1])
            out.append((tail, acc))
        return tuple(out)

    init = tuple((jnp.zeros((t, 1), F32), jnp.zeros((t, LANES), F32)) for _ in range(2))
    carry = step(i, init, True)

    def live(state):
        jj, carry = state
        return (jj < i) & (jnp.maximum(jnp.max(carry[0][0]), jnp.max(carry[1][0])) > SB_DEAD_LOG)

    def walk(state):
        jj, carry = state
        return jj + 1, step(i - 1 - jj, carry, False)

    _, carry = lax.while_loop(live, walk, (jnp.int32(0), carry))
    o_ref[...] = jnp.where(lane < HEAD_DIM, carry[0][1], carry[1][1]).astype(o_ref.dtype)


def _sb_attention(qkv, batch):
    n = qkv.shape[0]
    seq = n // batch
    nq = seq // SB_T
    pairs = H_SB // 2
    return pl.pallas_call(
        _sb_kernel,
        grid=(batch, pairs, nq),
        in_specs=_attention_specs(seq, nq, SB_T, *_head_cols(MAIN_SB, MAIN_W)),
        out_specs=pl.BlockSpec((SB_T, LANES), lambda b, p, i: (b * nq + i, p)),
        out_shape=jax.ShapeDtypeStruct((n, pairs * LANES), BF16),
        compiler_params=_params(("parallel", "parallel", "arbitrary"), 56),
        name="sb_attention",
    )(qkv, qkv, qkv)


def _fox_kernel(q_ref, k_ref, v_ref, aq_ref, ak_ref, o_ref):
    tq, tk = FOX_TQ, FOX_TK
    i = pl.program_id(2)
    q = q_ref[...]
    lane = lax.broadcasted_iota(jnp.int32, q.shape, 1)
    own = [lane < HEAD_DIM, lane >= HEAD_DIM]
    qh = [jnp.where(own[h], q, aq_ref[...]) for h in range(2)]
    klane = lax.broadcasted_iota(jnp.int32, (tk, LANES), 1)
    kown = [klane < HEAD_DIM, klane >= HEAD_DIM]
    row = lax.broadcasted_iota(jnp.int32, (tq, tk), 0)
    col = lax.broadcasted_iota(jnp.int32, (tq, tk), 1)

    def step(j, carry, diagonal):
        start = pl.multiple_of(j * tk, tk)
        k = k_ref[pl.ds(start, tk), :]
        v = v_ref[pl.ds(start, tk), :]
        ak = ak_ref[pl.ds(start, tk), :]
        out = []
        for h in range(2):
            s = lax.dot_general(qh[h], jnp.where(kown[h], k, ak), NT, preferred_element_type=F32)
            if diagonal:
                s = jnp.where(col + j * tk <= row + i * tq, s, NEG)
            m, acc = carry[h]
            m_new = jnp.maximum(m, jnp.max(s, axis=1, keepdims=True))
            p = jnp.exp(s - m_new).astype(BF16)
            vh = jnp.where(kown[h], v, jnp.ones_like(v))
            acc = jnp.exp(m - m_new) * acc + jnp.dot(p, vh, preferred_element_type=F32)
            out.append((m_new, acc))
        return tuple(out)

    carry = tuple((jnp.full((tq, 1), NEG, F32), jnp.zeros((tq, LANES), F32)) for _ in range(2))
    n_full = (i * tq) // tk
    carry = lax.fori_loop(0, n_full, lambda j, c: step(j, c, False), carry)
    for d in range(max(tq // tk, 1)):
        carry = step(n_full + d, carry, True)
    o = []
    for h in range(2):
        acc = carry[h][1]
        o.append(acc / acc[:, HEAD_DIM * (1 - h):HEAD_DIM * (1 - h) + 1])
    o_ref[...] = jnp.where(own[0], o[0], o[1]).astype(o_ref.dtype)


def _fox_attention(qkv, aug_q, aug_k, batch):
    n = qkv.shape[0]
    seq = n // batch
    nq = seq // FOX_TQ
    pairs = H_FOX // 2
    specs = _attention_specs(seq, nq, FOX_TQ, *_head_cols(MAIN_FOX, MAIN_W))
    specs += [pl.BlockSpec((FOX_TQ, LANES), lambda b, p, i: (b * nq + i, p)),
              pl.BlockSpec((seq, LANES), lambda b, p, i: (b, p))]
    return pl.pallas_call(
        _fox_kernel,
        grid=(batch, pairs, nq),
        in_specs=specs,
        out_specs=pl.BlockSpec((FOX_TQ, LANES), lambda b, p, i: (b * nq + i, p)),
        out_shape=jax.ShapeDtypeStruct((n, pairs * LANES), BF16),
        compiler_params=_params(("parallel", "parallel", "arbitrary"), 48),
        name="fox_attention",
    )(qkv, qkv, qkv, aug_q, aug_k)


def _moba_kernel(q_ref, k_ref, v_ref, onehot_ref, bias_ref, o_ref, kmean_ref):
    t, mb = MOBA_T, MOBA_BLOCK
    nb = t // mb
    seq = k_ref.shape[0]
    i = pl.program_id(2)

    @pl.when(i == 0)
    def _():
        blk = lax.broadcasted_iota(jnp.int32, (LANES, seq), 0)
        pos = lax.broadcasted_iota(jnp.int32, (LANES, seq), 1)
        member = jnp.where((pos >= blk * mb) & (pos < (blk + 1) * mb), 1.0, 0.0).astype(BF16)
        mean = jnp.dot(member, k_ref[...], preferred_element_type=F32) * (1.0 / mb)
        hi = mean.astype(BF16)
        kmean_ref[0] = hi
        kmean_ref[1] = (mean - hi.astype(F32)).astype(BF16)

    q = q_ref[...]
    lane = lax.broadcasted_iota(jnp.int32, (t, LANES), 1)
    own_half = [lane < HEAD_DIM, lane >= HEAD_DIM]
    zero = jnp.zeros_like(q)

    nblk = seq // mb
    blk_row = lax.broadcasted_iota(jnp.int32, (nblk, t), 0).astype(F32)
    own_row = ((i * t + lax.broadcasted_iota(jnp.int32, (nblk, t), 1)) // mb).astype(F32)
    fully_past = blk_row < own_row
    place_r = lax.broadcasted_iota(jnp.int32, (nblk, LANES), 0)
    place_c = lax.broadcasted_iota(jnp.int32, (nblk, LANES), 1)
    qh = []
    for h in range(2):
        q_only = jnp.where(own_half[h], q, zero)
        gate = (lax.dot_general(kmean_ref[0][:nblk], q_only, NT, preferred_element_type=F32)
                + lax.dot_general(kmean_ref[1][:nblk], q_only, NT, preferred_element_type=F32))
        gate = jnp.where(fully_past, gate, -jnp.inf)
        picked = jnp.zeros((nblk, t), jnp.bool_)
        for _ in range(MOBA_TOPK):
            best = jnp.max(gate, axis=0, keepdims=True)
            first = jnp.min(jnp.where(gate == best, blk_row, float(LANES)), axis=0, keepdims=True)
            pick = blk_row == first
            picked = picked | pick
            gate = jnp.where(pick, -jnp.inf, gate)
        allowed = jnp.where((picked & fully_past) | (blk_row == own_row), 1.0, 0.0).astype(BF16)
        place = jnp.where(place_c == place_r + HEAD_DIM * (1 - h), 1.0, 0.0).astype(BF16)
        spread = lax.dot_general(allowed, place, (((0,), (0,)), ((), ())), preferred_element_type=F32)
        penalty = jnp.where(spread > 0.5, 0.0, NEG)
        qh.append(jnp.where(own_half[h], q, penalty.astype(BF16)))

    def step(j, carry):
        start = pl.multiple_of(j * t, t)
        k = k_ref[pl.ds(start, t), :]
        v = v_ref[pl.ds(start, t), :]
        onehot = onehot_ref[pl.ds(start, t), :]
        out = []
        for h in range(2):
            s = lax.dot_general(qh[h], jnp.where(own_half[h], k, onehot), NT, preferred_element_type=F32)
            rows = []
            for a in range(nb):
                tiles = [bias_ref[h, jnp.clip((i - j) * nb + a - b, 0, MOBA_NT)] for b in range(nb)]
                rows.append(jnp.concatenate(tiles, axis=1))
            s = s + jnp.concatenate(rows, axis=0)
            m, acc = carry[h]
            m_new = jnp.maximum(m, jnp.max(s, axis=1, keepdims=True))
            p = jnp.exp(s - m_new).astype(BF16)
            vh = jnp.where(own_half[h], v, jnp.ones_like(v))
            acc = jnp.exp(m - m_new) * acc + jnp.dot(p, vh, preferred_element_type=F32)
            out.append((m_new, acc))
        return tuple(out)

    carry = tuple((jnp.full((t, 1), NEG, F32), jnp.zeros((t, LANES), F32)) for _ in range(2))
    carry = lax.fori_loop(0, i + 1, step, carry)
    o = []
    for h in range(2):
        acc = carry[h][1]
        o.append(acc / acc[:, HEAD_DIM * (1 - h):HEAD_DIM * (1 - h) + 1])
    o_ref[...] = jnp.where(own_half[0], o[0], o[1]).astype(o_ref.dtype)


def _moba_attention(qkv, bias_tiles, batch):
    n = qkv.shape[0]
    seq = n // batch
    assert seq % MOBA_T == 0 and seq // MOBA_BLOCK <= HEAD_DIM
    nq = seq // MOBA_T
    pairs = H_MOBA // 2
    blk = np.arange(seq)[:, None] // MOBA_BLOCK
    onehot = jnp.asarray(blk == (np.arange(LANES)[None, :] % HEAD_DIM), BF16)
    specs = _attention_specs(seq, nq, MOBA_T, *_head_cols(MAIN_MOBA, MAIN_W))
    specs.append(pl.BlockSpec((seq, LANES), lambda b, p, i: (0, 0)))
    specs.append(pl.BlockSpec((None, 2, MOBA_NT + 1, MOBA_BLOCK, MOBA_BLOCK), lambda b, p, i: (p, 0, 0, 0, 0)))
    return pl.pallas_call(
        _moba_kernel,
        grid=(batch, pairs, nq),
        in_specs=specs,
        out_specs=pl.BlockSpec((MOBA_T, LANES), lambda b, p, i: (b * nq + i, p)),
        out_shape=jax.ShapeDtypeStruct((n, pairs * LANES), BF16),
        scratch_shapes=[pltpu.VMEM((2, LANES, LANES), BF16)],
        compiler_params=_params(("arbitrary", "arbitrary", "arbitrary"), 56),
        name="moba_attention",
    )(qkv, qkv, qkv, onehot, bias_tiles)


def _moba_bias_tiles(rel_bias):
    r = jnp.arange(MOBA_BLOCK)
    dist = (jnp.arange(MOBA_NT + 1)[:, None, None] * MOBA_BLOCK + r[None, :, None] - r[None, None, :])
    tiles = _bias_lookup(rel_bias[:, H_DIL:], _t5_bucket(jnp.maximum(dist, 0)))
    tiles = jnp.where(dist >= 0, tiles, NEG)
    return tiles.reshape(H_MOBA // 2, 2, MOBA_NT + 1, MOBA_BLOCK, MOBA_BLOCK).astype(F32)


def _check_moba_saturation():
    n = np.float32(MOBA_NT * MOBA_BLOCK - (MOBA_BLOCK - 1))
    exact = N_BUCKETS // 2
    large = exact + int(np.log(n / np.float32(exact)) / math.log(MAX_DIST / exact) * (N_BUCKETS - exact))
    assert large >= N_BUCKETS - 1, "MOBA_NT too small for the bias bucket table"


_check_moba_saturation()


def _dilated_kernel(q_ref, k_ref, v_ref, bias_ref, o_ref, lse_ref):
    tq, tk = DIL_TQ, DIL_TQ + DIL_W
    n_tiles = q_ref.shape[0] // tq

    def body(u, _):
        q0 = pl.multiple_of(u * tq, tq)
        k0 = pl.multiple_of(jnp.maximum(q0 - DIL_W, 0), DIL_W)
        variant = jnp.minimum(u, 1)
        qh, lane = _head_pair(q_ref[pl.ds(q0, tq), :])
        k = k_ref[pl.ds(k0, tk), :]
        v = v_ref[pl.ds(k0, tk), :]
        outs, lses = [], []
        for h in range(2):
            z = lax.dot_general(qh[h], k, NT, preferred_element_type=F32) + bias_ref[h, variant]
            m = jnp.max(z, axis=1, keepdims=True)
            p = jnp.exp(z - m)
            l = jnp.sum(p, axis=1, keepdims=True)
            outs.append(jnp.dot(p.astype(BF16), v, preferred_element_type=F32) / l)
            lses.append(m + jnp.log(l))
        o_ref[pl.ds(q0, tq), :] = jnp.where(lane < HEAD_DIM, outs[0], outs[1])
        lse_ref[pl.ds(q0, tq), :] = jnp.where(lane < HEAD_DIM, lses[0], lses[1])
        return 0

    lax.fori_loop(0, n_tiles, body, 0)


def _dilated_group(qkv, head_off, width, bias, batch, group):
    window, dil = DIL_GROUPS[group]
    assert window // dil == DIL_W
    n = qkv.shape[0]
    sub = n // batch // dil
    assert sub % DIL_TQ == 0 and sub >= DIL_TQ + DIL_W
    pairs = H_DIL_PER_GROUP // 2
    q_col, k_col, v_col = _head_cols(head_off, width)

    def spec(col0):
        return pl.BlockSpec((sub, LANES), lambda b, p, r: (b * dil + r, col0 + p))

    out_spec = pl.BlockSpec((sub, LANES), lambda b, p, r: (b * dil + r, p))
    out_shape = jax.ShapeDtypeStruct((n, pairs * LANES), F32)
    return pl.pallas_call(
        _dilated_kernel,
        grid=(batch, pairs, dil),
        in_specs=[spec(q_col), spec(k_col), spec(v_col),
                  pl.BlockSpec((None, 2, 2, DIL_TQ, DIL_TQ + DIL_W),
                               lambda b, p, r: (group * pairs + p, 0, 0, 0, 0))],
        out_specs=[out_spec, out_spec],
        out_shape=[out_shape, out_shape],
        compiler_params=_params(("parallel", "parallel", "parallel"), 48),
        name=f"dilated_attention_g{group}",
    )(qkv, qkv, qkv, bias)


def _by_residue(a, batch, dil):
    n, c = a.shape
    return a.reshape(batch, n // batch // dil, dil, c).transpose(0, 2, 1, 3).reshape(n, c)


def _by_position(a, batch, dil):
    n, c = a.shape
    return a.reshape(batch, dil, n // batch // dil, c).transpose(0, 2, 1, 3).reshape(n, c)


def _dilated_bias_tiles(rel_bias):
    r = jnp.arange(DIL_TQ)[:, None]
    c = jnp.arange(DIL_TQ + DIL_W)[None, :]
    steps = jnp.stack([r - c, r + DIL_W - c], 0)
    inside = (steps >= 0) & (steps <= DIL_W)
    tiles = []
    for g, (_, dil) in enumerate(DIL_GROUPS):
        tab = rel_bias[:, g * H_DIL_PER_GROUP:(g + 1) * H_DIL_PER_GROUP]
        bias = _bias_lookup(tab, _t5_bucket(jnp.maximum(steps, 0) * dil))
        tiles.append(jnp.where(inside, bias, NEG))
    tiles = jnp.stack(tiles, 0).astype(F32)
    return tiles.reshape(len(DIL_GROUPS) * H_DIL_PER_GROUP // 2, 2, 2, DIL_TQ, DIL_TQ + DIL_W)


def _dilated_combine_kernel(o0, o1, o2, l0, l1, l2, out_ref):
    lse = [l0[...], l1[...], l2[...]]
    m = jnp.maximum(jnp.maximum(lse[0], lse[1]), lse[2])
    e = [jnp.exp(x - m) for x in lse]
    den = e[0] + e[1] + e[2]
    out = (e[0] / den) * o0[...] + (e[1] / den) * o1[...] + (e[2] / den) * o2[...]
    out_ref[...] = out.astype(out_ref.dtype)


def _dilated_combine(os, lses, tm=1024):
    n, c = os[0].shape
    spec = pl.BlockSpec((tm, c), lambda i: (i, 0))
    return pl.pallas_call(
        _dilated_combine_kernel,
        grid=(n // tm,),
        in_specs=[spec] * 6,
        out_specs=spec,
        out_shape=jax.ShapeDtypeStruct((n, c), BF16),
        compiler_params=_params(("parallel",), 32),
        name="dilated_combine",
    )(*os, *lses)


def _merge_kernel(x_ref, o_sb, o_dil, o_moba, o_fox, wg0, wg1, wg2, wg3, bg_ref,
                  w_sb, w_dil, w_moba, w_fox, out_ref):
    x = x_ref[...]
    acc = None
    branches = ((o_sb, w_sb, wg0), (o_dil, w_dil, wg1), (o_moba, w_moba, wg2), (o_fox, w_fox, wg3))
    for b, (o_ref, w_ref, wg_ref) in enumerate(branches):
        gate = jax.nn.sigmoid(jnp.dot(x, wg_ref[...], preferred_element_type=F32) + bg_ref[b])
        term = gate * jnp.dot(o_ref[...], w_ref[...], preferred_element_type=F32)
        acc = term if acc is None else acc + term
    out_ref[...] = acc.astype(out_ref.dtype)


def _gated_merge(x_bf, outs, w_gate, b_gate, w_branch, layer, tm=512, tn=512):
    n, d = x_bf.shape
    nj = d // tn
    row = lambda width: pl.BlockSpec((tm, width), lambda j, i: (i, 0))
    in_specs = [row(d)] + [row(o.shape[1]) for o in outs]
    in_specs += [pl.BlockSpec((None, d, tn), functools.partial(lambda j, i, b: (layer, 0, b * nj + j), b=b))
                 for b in range(N_BRANCH)]
    in_specs.append(pl.BlockSpec((None, N_BRANCH, 1, tn), lambda j, i: (layer, 0, 0, j)))
    in_specs += [pl.BlockSpec((None, w.shape[1], tn), lambda j, i: (layer, 0, j)) for w in w_branch]
    return pl.pallas_call(
        _merge_kernel,
        grid=(nj, n // tm),
        in_specs=in_specs,
        out_specs=pl.BlockSpec((tm, tn), lambda j, i: (i, j)),
        out_shape=jax.ShapeDtypeStruct((n, d), BF16),
        compiler_params=_params(("parallel", "parallel"), 48),
        name="gated_merge",
    )(x_bf, *outs, w_gate, w_gate, w_gate, w_gate, b_gate, *w_branch)


def _layer_norm_store(z, g_ref, b_ref, xo_ref, xb_ref):
    mu = jnp.mean(z, axis=1, keepdims=True)
    zc = z - mu
    var = jnp.mean(zc * zc, axis=1, keepdims=True)
    out = zc * lax.rsqrt(var + LN_EPS) * g_ref[...] + b_ref[...]
    xo_ref[...] = out
    xb_ref[...] = out.astype(BF16)


def _out_ln_kernel(m_ref, w_ref, x_ref, g_ref, b_ref, xo_ref, xb_ref):
    y = jnp.dot(m_ref[...], w_ref[...], preferred_element_type=F32)
    _layer_norm_store(DN_ALPHA * x_ref[...] + y, g_ref, b_ref, xo_ref, xb_ref)


def _combine_ln_kernel(y_ref, gate_ref, x_ref, g_ref, b_ref, xo_ref, xb_ref):
    gate = gate_ref[...]
    z = DN_ALPHA * x_ref[...]
    for k in range(TOP_K):
        z = z + gate[:, k:k + 1] * y_ref[k].astype(F32)
    _layer_norm_store(z, g_ref, b_ref, xo_ref, xb_ref)


def _ln_call(kernel, lead_specs, lead_args, x, g, b, tm, name):
    n, d = x.shape
    row = pl.BlockSpec((tm, d), lambda i: (i, 0))
    vec = pl.BlockSpec((1, d), lambda i: (0, 0))
    return pl.pallas_call(
        kernel,
        grid=(n // tm,),
        in_specs=lead_specs + [row, vec, vec],
        out_specs=[row, row],
        out_shape=[jax.ShapeDtypeStruct((n, d), F32), jax.ShapeDtypeStruct((n, d), BF16)],
        compiler_params=_params(("parallel",), 48),
        name=name,
    )(*lead_args, x, g, b)


def _out_proj_ln(merged, w_out, layer, x, g, b, tm=256):
    d = x.shape[1]
    specs = [pl.BlockSpec((tm, d), lambda i: (i, 0)), pl.BlockSpec((None, d, d), lambda i: (layer, 0, 0))]
    return _ln_call(_out_ln_kernel, specs, (merged, w_out), x, g, b, tm, "out_proj_ln")


def _combine_ln(yk, gate, x, g, b, tm=256):
    d = x.shape[1]
    specs = [pl.BlockSpec((TOP_K, tm, d), lambda i: (0, i, 0)), pl.BlockSpec((tm, LANES), lambda i: (i, 0))]
    return _ln_call(_combine_ln_kernel, specs, (yk, gate), x, g, b, tm, "moe_combine_ln")


def _router_kernel(x_ref, w_ref, b_ref, idx_ref, gate_ref, rank_ref, count_ref, seen_ref):
    @pl.when(pl.program_id(0) == 0)
    def _():
        seen_ref[...] = jnp.zeros_like(seen_ref)

    tm = x_ref.shape[0]
    logits = jnp.dot(x_ref[...], w_ref[...], preferred_element_type=F32, precision=HIGHEST) + b_ref[...]
    col = lax.broadcasted_iota(jnp.int32, logits.shape, 1)
    colf = col.astype(F32)
    logits = jnp.where(col < N_EXPERTS, logits, -jnp.inf)
    vals, idxs, hits = [], [], []
    for _ in range(TOP_K):
        best = jnp.max(logits, axis=1, keepdims=True)
        first = jnp.min(jnp.where(logits == best, colf, float(LANES)), axis=1, keepdims=True)
        hit = colf == first
        logits = jnp.where(hit, -jnp.inf, logits)
        vals.append(best)
        idxs.append(first)
        hits.append(jnp.where(hit, 1.0, 0.0))
    es = [jnp.exp(v - vals[0]) for v in vals]
    den = es[0] + es[1] + es[2] + es[3]

    chosen = hits[0] + hits[1] + hits[2] + hits[3]
    row = lax.broadcasted_iota(jnp.int32, (tm, tm), 0)
    before = jnp.where(lax.broadcasted_iota(jnp.int32, (tm, tm), 1) < row, 1.0, 0.0).astype(BF16)
    earlier = jnp.dot(before, chosen.astype(BF16), preferred_element_type=F32) + seen_ref[...]
    seen = seen_ref[...] + jnp.sum(chosen, axis=0, keepdims=True)
    seen_ref[...] = seen
    count_ref[...] = seen

    idx_out = jnp.zeros(logits.shape, F32)
    gate_out = jnp.zeros(logits.shape, F32)
    rank_out = jnp.zeros(logits.shape, F32)
    for r in range(TOP_K):
        idx_out = jnp.where(col == r, idxs[r], idx_out)
        gate_out = jnp.where(col == r, es[r] / den, gate_out)
        rank_out = jnp.where(col == r, jnp.sum(hits[r] * earlier, axis=1, keepdims=True), rank_out)
    idx_ref[...] = idx_out.astype(jnp.int32)
    gate_ref[...] = gate_out
    rank_ref[...] = rank_out.astype(jnp.int32)


def _router(x, w_r, b_r, tm=512):
    n, d = x.shape
    out = pl.BlockSpec((tm, LANES), lambda i: (i, 0))
    return pl.pallas_call(
        _router_kernel,
        grid=(n // tm,),
        in_specs=[pl.BlockSpec((tm, d), lambda i: (i, 0)),
                  pl.BlockSpec((d, LANES), lambda i: (0, 0)),
                  pl.BlockSpec((1, LANES), lambda i: (0, 0))],
        out_specs=[out, out, out, pl.BlockSpec((1, LANES), lambda i: (0, 0))],
        out_shape=[jax.ShapeDtypeStruct((n, LANES), jnp.int32), jax.ShapeDtypeStruct((n, LANES), F32),
                   jax.ShapeDtypeStruct((n, LANES), jnp.int32), jax.ShapeDtypeStruct((1, LANES), F32)],
        scratch_shapes=[pltpu.VMEM((1, LANES), F32)],
        compiler_params=_params(("arbitrary",), 32),
        name="moe_router",
    )(x, w_r, b_r)


def _expert_kernel(blk_exp_ref, xs_ref, wu_ref, bu_ref, wd_ref, bd_ref, y_ref):
    del blk_exp_ref
    h = jnp.dot(xs_ref[...], wu_ref[...].astype(BF16), preferred_element_type=F32) + bu_ref[...]
    glu = jnp.minimum(h[:, :D_FF], SWIGLU_LIMIT)
    lin = jnp.clip(h[:, D_FF:], -SWIGLU_LIMIT, SWIGLU_LIMIT)
    act = glu * jax.nn.sigmoid(SWIGLU_ALPHA * glu) * (lin + 1.0)
    y = jnp.dot(act.astype(BF16), wd_ref[...].astype(BF16), preferred_element_type=F32) + bd_ref[...]
    y_ref[...] = y.astype(y_ref.dtype)


def _expert_ffn(xs, blk_exp, w_up, b_up, w_down, b_down, layer):
    p, d = xs.shape
    grid_spec = pltpu.PrefetchScalarGridSpec(
        num_scalar_prefetch=1,
        grid=(p // MOE_ROWS,),
        in_specs=[pl.BlockSpec((MOE_ROWS, d), lambda i, e: (i, 0)),
                  pl.BlockSpec((None, None, d, 2 * D_FF), lambda i, e: (layer, e[i], 0, 0)),
                  pl.BlockSpec((None, None, 1, 2 * D_FF), lambda i, e: (layer, e[i], 0, 0)),
                  pl.BlockSpec((None, None, D_FF, d), lambda i, e: (layer, e[i], 0, 0)),
                  pl.BlockSpec((None, None, 1, d), lambda i, e: (layer, e[i], 0, 0))],
        out_specs=pl.BlockSpec((MOE_ROWS, d), lambda i, e: (i, 0)),
    )
    return pl.pallas_call(
        _expert_kernel,
        grid_spec=grid_spec,
        out_shape=jax.ShapeDtypeStruct((p, d), BF16),
        compiler_params=_params(("arbitrary",), 48),
        name="expert_ffn",
    )(blk_exp, xs, w_up, b_up, w_down, b_down)


def _moe(x, x_bf, w_r, b_r, w_up, b_up, w_down, b_down, layer):
    n, d = x.shape
    nk = n * TOP_K
    idx, gate, rank, count = _router(x, w_r, b_r)
    counts = count[0, :N_EXPERTS].astype(jnp.int32)
    start = jnp.cumsum(counts) - counts
    padded = (counts + MOE_ROWS - 1) // MOE_ROWS * MOE_ROWS
    pend = jnp.cumsum(padded)
    pstart = pend - padded
    experts = jnp.arange(N_EXPERTS, dtype=jnp.int32)
    chosen = idx[:, :TOP_K, None] == experts
    slot = jnp.sum(jnp.where(chosen, pstart, 0), axis=2) + rank[:, :TOP_K]
    n_blocks = -(-(nk + N_EXPERTS * (MOE_ROWS - 1)) // MOE_ROWS)
    first_row = jnp.arange(n_blocks, dtype=jnp.int32) * MOE_ROWS
    blk_exp = jnp.minimum(jnp.sum(pend[None, :] <= first_row[:, None], axis=1), N_EXPERTS - 1).astype(jnp.int32)
    order = jnp.argsort(slot.reshape(nk)).astype(jnp.int32)
    blk_is = blk_exp[:, None] == experts
    blk_start = jnp.sum(jnp.where(blk_is, start, 0), axis=1)
    blk_pstart = jnp.sum(jnp.where(blk_is, pstart, 0), axis=1)
    blk_count = jnp.sum(jnp.where(blk_is, counts, 0), axis=1)
    within = first_row[:, None] + jnp.arange(MOE_ROWS, dtype=jnp.int32)[None, :] - blk_pstart[:, None]
    pair = jnp.clip(blk_start[:, None] + within, 0, nk - 1)
    spare = (first_row[:, None] + jnp.arange(MOE_ROWS, dtype=jnp.int32)[None, :]) % n
    slot_tok = jnp.where(within < blk_count[:, None], order[pair] // TOP_K, spare).reshape(n_blocks * MOE_ROWS)
    slot_tok, slot_km = lax.optimization_barrier((slot_tok, slot.T.reshape(nk)))
    xs = x_bf[slot_tok]
    y = _expert_ffn(xs, blk_exp, w_up, b_up, w_down, b_down, layer)
    return y[slot_km].reshape(TOP_K, n, d), gate


def kernel(x, rel_bias, w_in, b_forget, b_gate, w_br_sb, w_br_dil, w_br_moba, w_br_fox, w_out,
           ln1_g, ln1_b, w_router, b_router, w_up, b_up, w_down, b_down, ln2_g, ln2_b):
    batch, seq, d = x.shape
    n = batch * seq
    depth = w_in.shape[0]

    w_qkv = _qkv_weights(w_in)
    w_f =jnp.pad(w_in[:, :, C_QKV:C_QKV + H_FOX], ((0, 0), (0, 0), (0, LANES - H_FOX)))
    b_f = jnp.pad(b_forget, ((0, 0), (0, LANES - H_FOX))).reshape(depth, 1, LANES)
    w_gate = _gate_weights(w_in)
    b_gate3 = b_gate.reshape(depth, N_BRANCH, 1, d)
    w_branch = [w.astype(BF16) for w in (w_br_sb, w_br_dil, w_br_moba, w_br_fox)]
    w_out_bf = w_out.astype(BF16)
    w_r = jnp.pad(w_router, ((0, 0), (0, 0), (0, LANES - N_EXPERTS)))
    b_r = jnp.pad(b_router, ((0, 0), (0, LANES - N_EXPERTS))).reshape(depth, 1, LANES)
    b_up3 = b_up.reshape(depth, N_EXPERTS, 1, 2 * D_FF)
    b_down3 = b_down.reshape(depth, N_EXPERTS, 1, d)
    moba_bias = _moba_bias_tiles(rel_bias)
    dil_bias = _dilated_bias_tiles(rel_bias)

    xf = x.reshape(n, d)
    xb = xf.astype(BF16)
    for l in range(depth):
        qkv = _matmul(xb, w_qkv, l, 0, 3 * MAIN_W, 1024, 768, BF16, "qkv_proj")
        aug_q, aug_k = _decay(xb, w_f[l], b_f[l], batch)
        o_sb = _sb_attention(qkv, batch)
        os, lses = [], []
        o, lse = _dilated_group(qkv, MAIN_DIL, MAIN_W, dil_bias, batch, 0)
        os.append(o)
        lses.append(lse)
        for g in range(1, len(DIL_GROUPS)):
            dil = DIL_GROUPS[g][1]
            qkv_g = _matmul(xb, w_qkv, l, 3 * (MAIN_W + (g - 1) * DIL_GROUP_W),
                            3 * DIL_GROUP_W, 1024, 3 * DIL_GROUP_W, BF16, f"qkv_proj_g{g}")
            o, lse = _dilated_group(_by_residue(qkv_g, batch, dil), 0, DIL_GROUP_W, dil_bias, batch, g)
            os.append(_by_position(o, batch, dil))
            lses.append(_by_position(lse, batch, dil))
        o_dil = _dilated_combine(os, lses)
        o_moba = _moba_attention(qkv, moba_bias, batch)
        o_fox = _fox_attention(qkv, aug_q, aug_k, batch)
        merged = _gated_merge(xb, (o_sb, o_dil, o_moba, o_fox), w_gate, b_gate3, w_branch, l)
        xf, xb = _out_proj_ln(merged, w_out_bf, l, xf, ln1_g[l].reshape(1, d), ln1_b[l].reshape(1, d))
        yk, gate = _moe(xf, xb, w_r[l], b_r[l], w_up, b_up3, w_down, b_down3, l)
        xf, xb = _combine_ln(yk, gate, xf, ln2_g[l].reshape(1, d), ln2_b[l].reshape(1, d))
    return xf.reshape(batch, seq, d)
```

```python
import functools
import math

import numpy as np
import jax
import jax.numpy as jnp
from jax import lax
from jax.experimental import pallas as pl
from jax.experimental.pallas import tpu as pltpu

D_MODEL = 2048
DEPTH = 4
HEAD_DIM = 64
H_SB = 8
H_DIL_PER_GROUP = 4
DIL_GROUPS = ((128, 1), (512, 4), (2048, 16))
H_DIL = H_DIL_PER_GROUP * len(DIL_GROUPS)
H_MOBA = 8
H_FOX = 8
H_ALL = H_SB + H_DIL + H_MOBA + H_FOX
SB_OFF = 0
DIL_OFF = SB_OFF + H_SB
MOBA_OFF = DIL_OFF + H_DIL
FOX_OFF = MOBA_OFF + H_MOBA
W_MIX = H_ALL * HEAD_DIM
N_BRANCH = 4
C_QKV = 3 * W_MIX
MOBA_BLOCK = 256
MOBA_TOPK = 3
N_BUCKETS = 32
MAX_DIST = 2048
N_EXPERTS = 32
TOP_K = 4
D_FF = 768
SWIGLU_ALPHA = 1.702
SWIGLU_LIMIT = 7.0
LN_EPS = 1e-5
DN_ALPHA = (2 * DEPTH) ** 0.25

F32 = jnp.float32
BF16 = jnp.bfloat16
LANES = 128
NEG = -1e30
HIGHEST = lax.Precision.HIGHEST
MIB = 1024 * 1024
NT = (((1,), (1,)), ((), ()))
SIGN_BIT = np.uint32(0x80000000)

MAIN_SB = 0
MAIN_DIL = MAIN_SB + H_SB
MAIN_MOBA = MAIN_DIL + H_DIL_PER_GROUP
MAIN_FOX = MAIN_MOBA + H_MOBA
MAIN_W = (MAIN_FOX + H_FOX) * HEAD_DIM
DIL_GROUP_W = H_DIL_PER_GROUP * HEAD_DIM
DIL_W = 128
DIL_TQ = 256
MOE_ROWS = 256
SB_T = 1024
SB_SEG = 256
SB_DEAD_LOG = -110.0
FOX_TQ = 1024
FOX_TK = 1024
MOBA_T = 1024
MOBA_NT = 8
AUG_PARTS = 3


def _params(sem, vmem_mib):
    return pltpu.CompilerParams(dimension_semantics=sem, vmem_limit_bytes=vmem_mib * MIB)


def _t5_bucket(n):
    exact = N_BUCKETS // 2
    nf = jnp.maximum(n, 1).astype(F32)
    large = exact + (jnp.log(nf / exact) / math.log(MAX_DIST / exact) * (N_BUCKETS - exact)).astype(jnp.int32)
    large = jnp.minimum(large, N_BUCKETS - 1)
    return jnp.where(n < exact, n, large)


def _bias_lookup(table, bucket):
    onehot = (bucket[..., None] == jnp.arange(N_BUCKETS)).astype(F32)
    return jnp.einsum('...k,kh->h...', onehot, table.astype(F32), precision=HIGHEST)


def _head_pair(q):
    lane = lax.broadcasted_iota(jnp.int32, q.shape, 1)
    zero = jnp.zeros_like(q)
    return [jnp.where(lane < HEAD_DIM, q, zero), jnp.where(lane >= HEAD_DIM, q, zero)], lane


def _attention_specs(seq, nq, tq, q_col, k_col, v_col):
    return [pl.BlockSpec((tq, LANES), lambda b, p, i: (b * nq + i, q_col + p)),
            pl.BlockSpec((seq, LANES), lambda b, p, i: (b, k_col + p)),
            pl.BlockSpec((seq, LANES), lambda b, p, i: (b, v_col + p))]


def _head_cols(off, width):
    q_col = off * HEAD_DIM // LANES
    return q_col, q_col + width // LANES, q_col + 2 * width // LANES


def _matmul_kernel(a_ref, w_ref, o_ref):
    o_ref[...] = jnp.dot(a_ref[...], w_ref[...], preferred_element_type=F32).astype(o_ref.dtype)


def _matmul(a, w, layer, col0, n, tm, tn, out_dtype, name):
    m, k = a.shape
    first = col0 // tn
    return pl.pallas_call(
        _matmul_kernel,
        grid=(n // tn, m // tm),
        in_specs=[pl.BlockSpec((tm, k), lambda j, i: (i, 0)),
                  pl.BlockSpec((None, k, tn), lambda j, i: (layer, 0, first + j))],
        out_specs=pl.BlockSpec((tm, tn), lambda j, i: (i, j)),
        out_shape=jax.ShapeDtypeStruct((m, n), out_dtype),
        compiler_params=_params(("parallel", "parallel"), 40),
        name=name,
    )(a, w)


QKV_TILE = 256
MAIN_HEADS = ((SB_OFF, H_SB), (DIL_OFF, H_DIL_PER_GROUP), (MOBA_OFF, H_MOBA), (FOX_OFF, H_FOX))


def _qkv_column_plan():
    groups = [MAIN_HEADS] + [((DIL_OFF + g * H_DIL_PER_GROUP, H_DIL_PER_GROUP),) for g in range(1, len(DIL_GROUPS))]
    src, is_q = [], []
    for heads in groups:
        for sec in range(3):
            for lo, cnt in heads:
                start = sec * W_MIX + lo * HEAD_DIM
                assert start % QKV_TILE == 0 and (cnt * HEAD_DIM) % QKV_TILE == 0
                for b in range(cnt * HEAD_DIM // QKV_TILE):
                    src.append(start // QKV_TILE + b)
                    is_q.append(int(sec == 0))
    assert sorted(src) == list(range(C_QKV // QKV_TILE))
    return np.asarray(src, np.int32), np.asarray(is_q, np.int32)


def _qkv_weight_kernel(src_ref, is_q_ref, w_ref, o_ref):
    del src_ref
    scale = jnp.where(is_q_ref[pl.program_id(1)] == 1, HEAD_DIM ** -0.5, 1.0)
    o_ref[...] = (w_ref[...].T * scale).astype(BF16)


def _qkv_weights(w_in_t):
    depth, _, d = w_in_t.shape
    src, is_q = _qkv_column_plan()
    grid_spec = pltpu.PrefetchScalarGridSpec(
        num_scalar_prefetch=2,
        grid=(depth, len(src)),
        in_specs=[pl.BlockSpec((None, QKV_TILE, d), lambda l, j, src, is_q: (l, src[j], 0))],
        out_specs=pl.BlockSpec((None, d, QKV_TILE), lambda l, j, src, is_q: (l, 0, j)),
    )
    return pl.pallas_call(
        _qkv_weight_kernel,
        grid_spec=grid_spec,
        out_shape=jax.ShapeDtypeStruct((depth, d, C_QKV), BF16),
        compiler_params=_params(("parallel", "parallel"), 24),
        name="qkv_weights",
    )(jnp.asarray(src), jnp.asarray(is_q), w_in_t)


GATE_COL0 = C_QKV + H_FOX
GATE_ALIGNED = GATE_COL0 // LANES * LANES
GATE_TILE = 256


def _gate_weight_kernel(a_ref, b_ref, o_ref, wf_ref):
    shift = GATE_COL0 - GATE_ALIGNED
    a = a_ref[...]
    both = jnp.concatenate([a, b_ref[...]], axis=0)
    o_ref[...] = both[shift:shift + GATE_TILE].T.astype(BF16)

    @pl.when(pl.program_id(1) == 0)
    def _():
        head = a[:LANES].T
        lane = lax.broadcasted_iota(jnp.int32, head.shape, 1)
        wf_ref[...] = jnp.where(lane < H_FOX, head, 0.0)


def _gate_weights(w_in_t):
    depth, _, d = w_in_t.shape
    width = N_BRANCH * d
    assert GATE_ALIGNED % GATE_TILE == 0 and width % GATE_TILE == 0 and GATE_ALIGNED == C_QKV
    a0 = GATE_ALIGNED // GATE_TILE
    per = GATE_TILE // LANES
    return pl.pallas_call(
        _gate_weight_kernel,
        grid=(depth, width // GATE_TILE),
        in_specs=[pl.BlockSpec((None, GATE_TILE, d), lambda l, j: (l, a0 + j, 0)),
                  pl.BlockSpec((None, LANES, d), lambda l, j: (l, (a0 + j + 1) * per, 0))],
        out_specs=[pl.BlockSpec((None, d, GATE_TILE), lambda l, j: (l, 0, j)),
                   pl.BlockSpec((None, d, LANES), lambda l, j: (l, 0, 0))],
        out_shape=[jax.ShapeDtypeStruct((depth, d, width), BF16), jax.ShapeDtypeStruct((depth, d, LANES), F32)],
        compiler_params=_params(("parallel", "arbitrary"), 24),
        name="gate_weights",
    )(w_in_t, w_in_t)


def _aug_constants():
    pairs = H_FOX // 2
    place_k = np.zeros((AUG_PARTS, LANES, pairs * LANES), np.float32)
    place_q = np.zeros((AUG_PARTS, LANES, pairs * LANES), np.float32)
    ones_k = np.zeros((1, pairs * LANES), np.float32)
    ones_q = np.zeros((1, pairs * LANES), np.float32)
    for g in range(H_FOX):
        p, h = divmod(g, 2)
        base = p * LANES + HEAD_DIM * (1 - h)
        for j in range(AUG_PARTS):
            place_k[j, g, base + j] = 1.0
            place_q[j, g, base + AUG_PARTS + j] = 1.0
            ones_k[0, base + AUG_PARTS + j] = 1.0
            ones_q[0, base + j] = 1.0
    return place_k, place_q, ones_k, ones_q


def _split_parts(x):
    parts = []
    for _ in range(AUG_PARTS):
        piece = x.astype(BF16)
        parts.append(piece)
        x = x - piece.astype(F32)
    return parts


def _decay_kernel(x_ref, whi_ref, wlo_ref, bf_ref, pk_ref, pq_ref, ok_ref, oq_ref, augq_ref, augk_ref, carry_ref):
    @pl.when(pl.program_id(1) == 0)
    def _():
        carry_ref[...] = jnp.zeros_like(carry_ref)

    tm = x_ref.shape[0]
    x = x_ref[...]
    f = (jnp.dot(x, whi_ref[...], preferred_element_type=F32)
         + jnp.dot(x, wlo_ref[...], preferred_element_type=F32) + bf_ref[...])
    logf = jnp.minimum(f, 0.0) - jnp.log1p(jnp.exp(-jnp.abs(f)))
    row = lax.broadcasted_iota(jnp.int32, (tm, tm), 0)
    col = lax.broadcasted_iota(jnp.int32, (tm, tm), 1)
    tri = jnp.where(col <= row, 1.0, 0.0).astype(F32)
    c = jnp.dot(tri, logf, preferred_element_type=F32, precision=HIGHEST) + carry_ref[...]
    carry_ref[...] = c[tm - 1:tm, :]
    aug_q = oq_ref[...]
    aug_k = ok_ref[...]
    for j, piece in enumerate(_split_parts(c)):
        aug_q = aug_q + jnp.dot(piece, pq_ref[j], preferred_element_type=F32)
        aug_k = aug_k - jnp.dot(piece, pk_ref[j], preferred_element_type=F32)
    augq_ref[...] = aug_q.astype(BF16)
    augk_ref[...] = aug_k.astype(BF16)


def _decay(x, w_f, b_f, batch, tm=256):
    n, d = x.shape
    nb = n // batch // tm
    width = H_FOX // 2 * LANES
    place_k, place_q, ones_k, ones_q = _aug_constants()
    w_hi = w_f.astype(BF16)
    w_lo = (w_f - w_hi.astype(F32)).astype(BF16)
    const = lambda a: pl.BlockSpec(a.shape, lambda b, i: (0,) * a.ndim)
    out = pl.BlockSpec((tm, width), lambda b, i: (b * nb + i, 0))
    return pl.pallas_call(
        _decay_kernel,
        grid=(batch, nb),
        in_specs=[pl.BlockSpec((tm, d), lambda b, i: (b * nb + i, 0)),
                  pl.BlockSpec((d, LANES), lambda b, i: (0, 0)),
                  pl.BlockSpec((d, LANES), lambda b, i: (0, 0)),
                  pl.BlockSpec((1, LANES), lambda b, i: (0, 0)),
                  const(place_k), const(place_q), const(ones_k), const(ones_q)],
        out_specs=[out, out],
        out_shape=[jax.ShapeDtypeStruct((n, width), BF16)] * 2,
        scratch_shapes=[pltpu.VMEM((1, LANES), F32)],
        compiler_params=_params(("arbitrary", "arbitrary"), 24),
        name="fox_decay",
    )(x, w_hi, w_lo, b_f, jnp.asarray(place_k, BF16), jnp.asarray(place_q, BF16), jnp.asarray(ones_k),
      jnp.asarray(ones_q))


def _sb_kernel(q_ref, k_ref, v_ref, o_ref):
    t, seg = SB_T, SB_SEG
    nseg = t // seg
    i = pl.program_id(2)
    qh, lane = _head_pair(q_ref[...])
    qneg = [-x for x in qh]
    r = lax.broadcasted_iota(jnp.int32, (seg, seg), 0)
    c = lax.broadcasted_iota(jnp.int32, (seg, seg), 1)
    later = jnp.where(r > c, 1.0, 0.0).astype(BF16)
    row = lax.broadcasted_iota(jnp.int32, (t, t), 0)
    col = lax.broadcasted_iota(jnp.int32, (t, t), 1)
    past = col < row

    def step(j, carry, diagonal):
        start = pl.multiple_of(j * t, t)
        k = k_ref[pl.ds(start, t), :]
        v = v_ref[pl.ds(start, t), :]
        out = []
        for h in range(2):
            tail, acc = carry[h]
            nz = lax.dot_general(qneg[h], k, NT, preferred_element_type=F32)
            minus_abs = lax.bitcast_convert_type(lax.bitcast_convert_type(nz, jnp.uint32) | SIGN_BIT, F32)
            log_1m = jnp.minimum(nz, 0.0) - jnp.log(1.0 + jnp.exp(minus_abs))
            base = log_1m - nz
            if diagonal:
                log_1m = jnp.where(past, log_1m, 0.0)
            sums = [jnp.dot(log_1m[:, b * seg:(b + 1) * seg].astype(BF16), later, preferred_element_type=F32)
                    for b in range(nseg)]
            for b in reversed(range(nseg)):
                sl = slice(b * seg, (b + 1) * seg)
                w = jnp.exp(base[:, sl] + (sums[b] + tail))
                if diagonal:
                    w = jnp.where(past[:, sl], w, 0.0)
                acc = acc + jnp.dot(w.astype(BF16), v[sl], preferred_element_type=F32)
                tail = tail + (sums[b][:, 0:1] + log_1m[:, b * seg:b * seg + 1])
            out.append((tail, acc))
        return tuple(out)

    init = tuple((jnp.zeros((t, 1), F32), jnp.zeros((t, LANES), F32)) for _ in range(2))
    carry = step(i, init, True)

    def live(state):
        jj, carry = state
        return (jj < i) & (jnp.maximum(jnp.max(carry[0][0]), jnp.max(carry[1][0])) > SB_DEAD_LOG)

    def walk(state):
        jj, carry = state
        return jj + 1, step(i - 1 - jj, carry, False)

    _, carry = lax.while_loop(live, walk, (jnp.int32(0), carry))
    o_ref[...] = jnp.where(lane < HEAD_DIM, carry[0][1], carry[1][1]).astype(o_ref.dtype)


def _sb_attention(qkv, batch):
    n = qkv.shape[0]
    seq = n // batch
    nq = seq // SB_T
    pairs = H_SB // 2
    return pl.pallas_call(
        _sb_kernel,
        grid=(batch, pairs, nq),
        in_specs=_attention_specs(seq, nq, SB_T, *_head_cols(MAIN_SB, MAIN_W)),
        out_specs=pl.BlockSpec((SB_T, LANES), lambda b, p, i: (b * nq + i, p)),
        out_shape=jax.ShapeDtypeStruct((n, pairs * LANES), BF16),
        compiler_params=_params(("parallel", "parallel", "arbitrary"), 56),
        name="sb_attention",
    )(qkv, qkv, qkv)


def _fox_kernel(q_ref, k_ref, v_ref, aq_ref, ak_ref, o_ref):
    tq, tk = FOX_TQ, FOX_TK
    i = pl.program_id(2)
    q = q_ref[...]
    lane = lax.broadcasted_iota(jnp.int32, q.shape, 1)
    own = [lane < HEAD_DIM, lane >= HEAD_DIM]
    qh = [jnp.where(own[h], q, aq_ref[...]) for h in range(2)]
    klane = lax.broadcasted_iota(jnp.int32, (tk, LANES), 1)
    kown = [klane < HEAD_DIM, klane >= HEAD_DIM]
    row = lax.broadcasted_iota(jnp.int32, (tq, tk), 0)
    col = lax.broadcasted_iota(jnp.int32, (tq, tk), 1)

    def step(j, carry, diagonal):
        start = pl.multiple_of(j * tk, tk)
        k = k_ref[pl.ds(start, tk), :]
        v = v_ref[pl.ds(start, tk), :]
        ak = ak_ref[pl.ds(start, tk), :]
        out = []
        for h in range(2):
            s = lax.dot_general(qh[h], jnp.where(kown[h], k, ak), NT, preferred_element_type=F32)
            if diagonal:
                s = jnp.where(col + j * tk <= row + i * tq, s, NEG)
            m, acc = carry[h]
            m_new = jnp.maximum(m, jnp.max(s, axis=1, keepdims=True))
            p = jnp.exp(s - m_new).astype(BF16)
            vh = jnp.where(kown[h], v, jnp.ones_like(v))
            acc = jnp.exp(m - m_new) * acc + jnp.dot(p, vh, preferred_element_type=F32)
            out.append((m_new, acc))
        return tuple(out)

    carry = tuple((jnp.full((tq, 1), NEG, F32), jnp.zeros((tq, LANES), F32)) for _ in range(2))
    n_full = (i * tq) // tk
    carry = lax.fori_loop(0, n_full, lambda j, c: step(j, c, False), carry)
    for d in range(max(tq // tk, 1)):
        carry = step(n_full + d, carry, True)
    o = []
    for h in range(2):
        acc = carry[h][1]
        o.append(acc / acc[:, HEAD_DIM * (1 - h):HEAD_DIM * (1 - h) + 1])
    o_ref[...] = jnp.where(own[0], o[0], o[1]).astype(o_ref.dtype)


def _fox_attention(qkv, aug_q, aug_k, batch):
    n = qkv.shape[0]
    seq = n // batch
    nq = seq // FOX_TQ
    pairs = H_FOX // 2
    specs = _attention_specs(seq, nq, FOX_TQ, *_head_cols(MAIN_FOX, MAIN_W))
    specs += [pl.BlockSpec((FOX_TQ, LANES), lambda b, p, i: (b * nq + i, p)),
              pl.BlockSpec((seq, LANES), lambda b, p, i: (b, p))]
    return pl.pallas_call(
        _fox_kernel,
        grid=(batch, pairs, nq),
        in_specs=specs,
        out_specs=pl.BlockSpec((FOX_TQ, LANES), lambda b, p, i: (b * nq + i, p)),
        out_shape=jax.ShapeDtypeStruct((n, pairs * LANES), BF16),
        compiler_params=_params(("parallel", "parallel", "arbitrary"), 48),
        name="fox_attention",
    )(qkv, qkv, qkv, aug_q, aug_k)


def _moba_kernel(q_ref, k_ref, v_ref, onehot_ref, bias_ref, o_ref, kmean_ref):
    t, mb = MOBA_T, MOBA_BLOCK
    nb = t // mb
    seq = k_ref.shape[0]
    i = pl.program_id(2)

    @pl.when(i == 0)
    def _():
        blk = lax.broadcasted_iota(jnp.int32, (LANES, seq), 0)
        pos = lax.broadcasted_iota(jnp.int32, (LANES, seq), 1)
        member = jnp.where((pos >= blk * mb) & (pos < (blk + 1) * mb), 1.0, 0.0).astype(BF16)
        mean = jnp.dot(member, k_ref[...], preferred_element_type=F32) * (1.0 / mb)
        hi = mean.astype(BF16)
        kmean_ref[0] = hi
        kmean_ref[1] = (mean - hi.astype(F32)).astype(BF16)

    q = q_ref[...]
    lane = lax.broadcasted_iota(jnp.int32, (t, LANES), 1)
    own_half = [lane < HEAD_DIM, lane >= HEAD_DIM]
    zero = jnp.zeros_like(q)

    nblk = seq // mb
    blk_row = lax.broadcasted_iota(jnp.int32, (nblk, t), 0).astype(F32)
    own_row = ((i * t + lax.broadcasted_iota(jnp.int32, (nblk, t), 1)) // mb).astype(F32)
    fully_past = blk_row < own_row
    place_r = lax.broadcasted_iota(jnp.int32, (nblk, LANES), 0)
    place_c = lax.broadcasted_iota(jnp.int32, (nblk, LANES), 1)
    qh = []
    for h in range(2):
        q_only = jnp.where(own_half[h], q, zero)
        gate = (lax.dot_general(kmean_ref[0][:nblk], q_only, NT, preferred_element_type=F32)
                + lax.dot_general(kmean_ref[1][:nblk], q_only, NT, preferred_element_type=F32))
        gate = jnp.where(fully_past, gate, -jnp.inf)
        picked = jnp.zeros((nblk, t), jnp.bool_)
        for _ in range(MOBA_TOPK):
            best = jnp.max(gate, axis=0, keepdims=True)
            first = jnp.min(jnp.where(gate == best, blk_row, float(LANES)), axis=0, keepdims=True)
            pick = blk_row == first
            picked = picked | pick
            gate = jnp.where(pick, -jnp.inf, gate)
        allowed = jnp.where((picked & fully_past) | (blk_row == own_row), 1.0, 0.0).astype(BF16)
        place = jnp.where(place_c == place_r + HEAD_DIM * (1 - h), 1.0, 0.0).astype(BF16)
        spread = lax.dot_general(allowed, place, (((0,), (0,)), ((), ())), preferred_element_type=F32)
        penalty = jnp.where(spread > 0.5, 0.0, NEG)
        qh.append(jnp.where(own_half[h], q, penalty.astype(BF16)))

    def step(j, carry):
        start = pl.multiple_of(j * t, t)
        k = k_ref[pl.ds(start, t), :]
        v = v_ref[pl.ds(start, t), :]
        onehot = onehot_ref[pl.ds(start, t), :]
        out = []
        for h in range(2):
            s = lax.dot_general(qh[h], jnp.where(own_half[h], k, onehot), NT, preferred_element_type=F32)
            rows = []
            for a in range(nb):
                tiles = [bias_ref[h, jnp.clip((i - j) * nb + a - b, 0, MOBA_NT)] for b in range(nb)]
                rows.append(jnp.concatenate(tiles, axis=1))
            s = s + jnp.concatenate(rows, axis=0)
            m, acc = carry[h]
            m_new = jnp.maximum(m, jnp.max(s, axis=1, keepdims=True))
            p = jnp.exp(s - m_new).astype(BF16)
            vh = jnp.where(own_half[h], v, jnp.ones_like(v))
            acc = jnp.exp(m - m_new) * acc + jnp.dot(p, vh, preferred_element_type=F32)
            out.append((m_new, acc))
        return tuple(out)

    carry = tuple((jnp.full((t, 1), NEG, F32), jnp.zeros((t, LANES), F32)) for _ in range(2))
    carry = lax.fori_loop(0, i + 1, step, carry)
    o = []
    for h in range(2):
        acc = carry[h][1]
        o.append(acc / acc[:, HEAD_DIM * (1 - h):HEAD_DIM * (1 - h) + 1])
    o_ref[...] = jnp.where(own_half[0], o[0], o[1]).astype(o_ref.dtype)


def _moba_attention(qkv, bias_tiles, batch):
    n = qkv.shape[0]
    seq = n // batch
    assert seq % MOBA_T == 0 and seq // MOBA_BLOCK <= HEAD_DIM
    nq = seq // MOBA_T
    pairs = H_MOBA // 2
    blk = np.arange(seq)[:, None] // MOBA_BLOCK
    onehot = jnp.asarray(blk == (np.arange(LANES)[None, :] % HEAD_DIM), BF16)
    specs = _attention_specs(seq, nq, MOBA_T, *_head_cols(MAIN_MOBA, MAIN_W))
    specs.append(pl.BlockSpec((seq, LANES), lambda b, p, i: (0, 0)))
    specs.append(pl.BlockSpec((None, 2, MOBA_NT + 1, MOBA_BLOCK, MOBA_BLOCK), lambda b, p, i: (p, 0, 0, 0, 0)))
    return pl.pallas_call(
        _moba_kernel,
        grid=(batch, pairs, nq),
        in_specs=specs,
        out_specs=pl.BlockSpec((MOBA_T, LANES), lambda b, p, i: (b * nq + i, p)),
        out_shape=jax.ShapeDtypeStruct((n, pairs * LANES), BF16),
        scratch_shapes=[pltpu.VMEM((2, LANES, LANES), BF16)],
        compiler_params=_params(("arbitrary", "arbitrary", "arbitrary"), 56),
        name="moba_attention",
    )(qkv, qkv, qkv, onehot, bias_tiles)


def _moba_bias_tiles(rel_bias):
    r = jnp.arange(MOBA_BLOCK)
    dist = (jnp.arange(MOBA_NT + 1)[:, None, None] * MOBA_BLOCK + r[None, :, None] - r[None, None, :])
    tiles = _bias_lookup(rel_bias[:, H_DIL:], _t5_bucket(jnp.maximum(dist, 0)))
    tiles = jnp.where(dist >= 0, tiles, NEG)
    return tiles.reshape(H_MOBA // 2, 2, MOBA_NT + 1, MOBA_BLOCK, MOBA_BLOCK).astype(F32)


def _check_moba_saturation():
    n = np.float32(MOBA_NT * MOBA_BLOCK - (MOBA_BLOCK - 1))
    exact = N_BUCKETS // 2
    large = exact + int(np.log(n / np.float32(exact)) / math.log(MAX_DIST / exact) * (N_BUCKETS - exact))
    assert large >= N_BUCKETS - 1, "MOBA_NT too small for the bias bucket table"


_check_moba_saturation()


def _dilated_kernel(q_ref, k_ref, v_ref, bias_ref, o_ref, lse_ref):
    tq, tk = DIL_TQ, DIL_TQ + DIL_W
    n_tiles = q_ref.shape[0] // tq

    def body(u, _):
        q0 = pl.multiple_of(u * tq, tq)
        k0 = pl.multiple_of(jnp.maximum(q0 - DIL_W, 0), DIL_W)
        variant = jnp.minimum(u, 1)
        qh, lane = _head_pair(q_ref[pl.ds(q0, tq), :])
        k = k_ref[pl.ds(k0, tk), :]
        v = v_ref[pl.ds(k0, tk), :]
        outs, lses = [], []
        for h in range(2):
            z = lax.dot_general(qh[h], k, NT, preferred_element_type=F32) + bias_ref[h, variant]
            m = jnp.max(z, axis=1, keepdims=True)
            p = jnp.exp(z - m)
            l = jnp.sum(p, axis=1, keepdims=True)
            outs.append(jnp.dot(p.astype(BF16), v, preferred_element_type=F32) / l)
            lses.append(m + jnp.log(l))
        o_ref[pl.ds(q0, tq), :] = jnp.where(lane < HEAD_DIM, outs[0], outs[1])
        lse_ref[pl.ds(q0, tq), :] = jnp.where(lane < HEAD_DIM, lses[0], lses[1])
        return 0

    lax.fori_loop(0, n_tiles, body, 0)


def _dilated_group(qkv, head_off, width, bias, batch, group):
    window, dil = DIL_GROUPS[group]
    assert window // dil == DIL_W
    n = qkv.shape[0]
    sub = n // batch // dil
    assert sub % DIL_TQ == 0 and sub >= DIL_TQ + DIL_W
    pairs = H_DIL_PER_GROUP // 2
    q_col, k_col, v_col = _head_cols(head_off, width)

    def spec(col0):
        return pl.BlockSpec((sub, LANES), lambda b, p, r: (b * dil + r, col0 + p))

    out_spec = pl.BlockSpec((sub, LANES), lambda b, p, r: (b * dil + r, p))
    out_shape = jax.ShapeDtypeStruct((n, pairs * LANES), F32)
    return pl.pallas_call(
        _dilated_kernel,
        grid=(batch, pairs, dil),
        in_specs=[spec(q_col), spec(k_col), spec(v_col),
                  pl.BlockSpec((None, 2, 2, DIL_TQ, DIL_TQ + DIL_W),
                               lambda b, p, r: (group * pairs + p, 0, 0, 0, 0))],
        out_specs=[out_spec, out_spec],
        out_shape=[out_shape, out_shape],
        compiler_params=_params(("parallel", "parallel", "parallel"), 48),
        name=f"dilated_attention_g{group}",
    )(qkv, qkv, qkv, bias)


def _by_residue(a, batch, dil):
    n, c = a.shape
    return a.reshape(batch, n // batch // dil, dil, c).transpose(0, 2, 1, 3).reshape(n, c)


def _by_position(a, batch, dil):
    n, c = a.shape
    return a.reshape(batch, dil, n // batch // dil, c).transpose(0, 2, 1, 3).reshape(n, c)


def _dilated_bias_tiles(rel_bias):
    r = jnp.arange(DIL_TQ)[:, None]
    c = jnp.arange(DIL_TQ + DIL_W)[None, :]
    steps = jnp.stack([r - c, r + DIL_W - c], 0)
    inside = (steps >= 0) & (steps <= DIL_W)
    tiles = []
    for g, (_, dil) in enumerate(DIL_GROUPS):
        tab = rel_bias[:, g * H_DIL_PER_GROUP:(g + 1) * H_DIL_PER_GROUP]
        bias = _bias_lookup(tab, _t5_bucket(jnp.maximum(steps, 0) * dil))
        tiles.append(jnp.where(inside, bias, NEG))
    tiles = jnp.stack(tiles, 0).astype(F32)
    return tiles.reshape(len(DIL_GROUPS) * H_DIL_PER_GROUP // 2, 2, 2, DIL_TQ, DIL_TQ + DIL_W)


def _dilated_combine_kernel(o0, o1, o2, l0, l1, l2, out_ref):
    lse = [l0[...], l1[...], l2[...]]
    m = jnp.maximum(jnp.maximum(lse[0], lse[1]), lse[2])
    e = [jnp.exp(x - m) for x in lse]
    den = e[0] + e[1] + e[2]
    out = (e[0] / den) * o0[...] + (e[1] / den) * o1[...] + (e[2] / den) * o2[...]
    out_ref[...] = out.astype(out_ref.dtype)


def _dilated_combine(os, lses, tm=1024):
    n, c = os[0].shape
    spec = pl.BlockSpec((tm, c), lambda i: (i, 0))
    return pl.pallas_call(
        _dilated_combine_kernel,
        grid=(n // tm,),
        in_specs=[spec] * 6,
        out_specs=spec,
        out_shape=jax.ShapeDtypeStruct((n, c), BF16),
        compiler_params=_params(("parallel",), 32),
        name="dilated_combine",
    )(*os, *lses)


def _merge_kernel(x_ref, o_sb, o_dil, o_moba, o_fox, wg0, wg1, wg2, wg3, bg_ref,
                  w_sb, w_dil, w_moba, w_fox, out_ref):
    x = x_ref[...]
    acc = None
    branches = ((o_sb, w_sb, wg0), (o_dil, w_dil, wg1), (o_moba, w_moba, wg2), (o_fox, w_fox, wg3))
    for b, (o_ref, w_ref, wg_ref) in enumerate(branches):
        gate = jax.nn.sigmoid(jnp.dot(x, wg_ref[...], preferred_element_type=F32) + bg_ref[b])
        term = gate * jnp.dot(o_ref[...], w_ref[...], preferred_element_type=F32)
        acc = term if acc is None else acc + term
    out_ref[...] = acc.astype(out_ref.dtype)


def _gated_merge(x_bf, outs, w_gate, b_gate, w_branch, layer, tm=512, tn=512):
    n, d = x_bf.shape
    nj = d // tn
    row = lambda width: pl.BlockSpec((tm, width), lambda j, i: (i, 0))
    in_specs = [row(d)] + [row(o.shape[1]) for o in outs]
    in_specs += [pl.BlockSpec((None, d, tn), functools.partial(lambda j, i, b: (layer, 0, b * nj + j), b=b))
                 for b in range(N_BRANCH)]
    in_specs.append(pl.BlockSpec((None, N_BRANCH, 1, tn), lambda j, i: (layer, 0, 0, j)))
    in_specs += [pl.BlockSpec((None, w.shape[1], tn), lambda j, i: (layer, 0, j)) for w in w_branch]
    return pl.pallas_call(
        _merge_kernel,
        grid=(nj, n // tm),
        in_specs=in_specs,
        out_specs=pl.BlockSpec((tm, tn), lambda j, i: (i, j)),
        out_shape=jax.ShapeDtypeStruct((n, d), BF16),
        compiler_params=_params(("parallel", "parallel"), 48),
        name="gated_merge",
    )(x_bf, *outs, w_gate, w_gate, w_gate, w_gate, b_gate, *w_branch)


def _layer_norm_store(z, g_ref, b_ref, xo_ref, xb_ref):
    mu = jnp.mean(z, axis=1, keepdims=True)
    zc = z - mu
    var = jnp.mean(zc * zc, axis=1, keepdims=True)
    out = zc * lax.rsqrt(var + LN_EPS) * g_ref[...] + b_ref[...]
    xo_ref[...] = out
    xb_ref[...] = out.astype(BF16)


def _out_ln_kernel(m_ref, w_ref, x_ref, g_ref, b_ref, xo_ref, xb_ref):
    y = jnp.dot(m_ref[...], w_ref[...], preferred_element_type=F32)
    _layer_norm_store(DN_ALPHA * x_ref[...] + y, g_ref, b_ref, xo_ref, xb_ref)


def _combine_ln_kernel(y_ref, gate_ref, x_ref, g_ref, b_ref, xo_ref, xb_ref):
    gate = gate_ref[...]
    z = DN_ALPHA * x_ref[...]
    for k in range(TOP_K):
        z = z + gate[:, k:k + 1] * y_ref[k].astype(F32)
    _layer_norm_store(z, g_ref, b_ref, xo_ref, xb_ref)


def _ln_call(kernel, lead_specs, lead_args, x, g, b, tm, name):
    n, d = x.shape
    row = pl.BlockSpec((tm, d), lambda i: (i, 0))
    vec = pl.BlockSpec((1, d), lambda i: (0, 0))
    return pl.pallas_call(
        kernel,
        grid=(n // tm,),
        in_specs=lead_specs + [row, vec, vec],
        out_specs=[row, row],
        out_shape=[jax.ShapeDtypeStruct((n, d), F32), jax.ShapeDtypeStruct((n, d), BF16)],
        compiler_params=_params(("parallel",), 48),
        name=name,
    )(*lead_args, x, g, b)


def _out_proj_ln(merged, w_out, layer, x, g, b, tm=256):
    d = x.shape[1]
    specs = [pl.BlockSpec((tm, d), lambda i: (i, 0)), pl.BlockSpec((None, d, d), lambda i: (layer, 0, 0))]
    return _ln_call(_out_ln_kernel, specs, (merged, w_out), x, g, b, tm, "out_proj_ln")


def _combine_ln(yk, gate, x, g, b, tm=256):
    d = x.shape[1]
    specs = [pl.BlockSpec((TOP_K, tm, d), lambda i: (0, i, 0)), pl.BlockSpec((tm, LANES), lambda i: (i, 0))]
    return _ln_call(_combine_ln_kernel, specs, (yk, gate), x, g, b, tm, "moe_combine_ln")


def _router_kernel(x_ref, w_ref, b_ref, idx_ref, gate_ref, rank_ref, count_ref, seen_ref):
    @pl.when(pl.program_id(0) == 0)
    def _():
        seen_ref[...] = jnp.zeros_like(seen_ref)

    tm = x_ref.shape[0]
    logits = jnp.dot(x_ref[...], w_ref[...], preferred_element_type=F32, precision=HIGHEST) + b_ref[...]
    col = lax.broadcasted_iota(jnp.int32, logits.shape, 1)
    colf = col.astype(F32)
    logits = jnp.where(col < N_EXPERTS, logits, -jnp.inf)
    vals, idxs, hits = [], [], []
    for _ in range(TOP_K):
        best = jnp.max(logits, axis=1, keepdims=True)
        first = jnp.min(jnp.where(logits == best, colf, float(LANES)), axis=1, keepdims=True)
        hit = colf == first
        logits = jnp.where(hit, -jnp.inf, logits)
        vals.append(best)
        idxs.append(first)
        hits.append(jnp.where(hit, 1.0, 0.0))
    es = [jnp.exp(v - vals[0]) for v in vals]
    den = es[0] + es[1] + es[2] + es[3]

    chosen = hits[0] + hits[1] + hits[2] + hits[3]
    row = lax.broadcasted_iota(jnp.int32, (tm, tm), 0)
    before = jnp.where(lax.broadcasted_iota(jnp.int32, (tm, tm), 1) < row, 1.0, 0.0).astype(BF16)
    earlier = jnp.dot(before, chosen.astype(BF16), preferred_element_type=F32) + seen_ref[...]
    seen = seen_ref[...] + jnp.sum(chosen, axis=0, keepdims=True)
    seen_ref[...] = seen
    count_ref[...] = seen

    idx_out = jnp.zeros(logits.shape, F32)
    gate_out = jnp.zeros(logits.shape, F32)
    rank_out = jnp.zeros(logits.shape, F32)
    for r in range(TOP_K):
        idx_out = jnp.where(col == r, idxs[r], idx_out)
        gate_out = jnp.where(col == r, es[r] / den, gate_out)
        rank_out = jnp.where(col == r, jnp.sum(hits[r] * earlier, axis=1, keepdims=True), rank_out)
    idx_ref[...] = idx_out.astype(jnp.int32)
    gate_ref[...] = gate_out
    rank_ref[...] = rank_out.astype(jnp.int32)


def _router(x, w_r, b_r, tm=512):
    n, d = x.shape
    out = pl.BlockSpec((tm, LANES), lambda i: (i, 0))
    return pl.pallas_call(
        _router_kernel,
        grid=(n // tm,),
        in_specs=[pl.BlockSpec((tm, d), lambda i: (i, 0)),
                  pl.BlockSpec((d, LANES), lambda i: (0, 0)),
                  pl.BlockSpec((1, LANES), lambda i: (0, 0))],
        out_specs=[out, out, out, pl.BlockSpec((1, LANES), lambda i: (0, 0))],
        out_shape=[jax.ShapeDtypeStruct((n, LANES), jnp.int32), jax.ShapeDtypeStruct((n, LANES), F32),
                   jax.ShapeDtypeStruct((n, LANES), jnp.int32), jax.ShapeDtypeStruct((1, LANES), F32)],
        scratch_shapes=[pltpu.VMEM((1, LANES), F32)],
        compiler_params=_params(("arbitrary",), 32),
        name="moe_router",
    )(x, w_r, b_r)


def _expert_kernel(blk_exp_ref, xs_ref, wu_ref, bu_ref, wd_ref, bd_ref, y_ref):
    del blk_exp_ref
    h = jnp.dot(xs_ref[...], wu_ref[...].astype(BF16), preferred_element_type=F32) + bu_ref[...]
    glu = jnp.minimum(h[:, :D_FF], SWIGLU_LIMIT)
    lin = jnp.clip(h[:, D_FF:], -SWIGLU_LIMIT, SWIGLU_LIMIT)
    act = glu * jax.nn.sigmoid(SWIGLU_ALPHA * glu) * (lin + 1.0)
    y = jnp.dot(act.astype(BF16), wd_ref[...].astype(BF16), preferred_element_type=F32) + bd_ref[...]
    y_ref[...] = y.astype(y_ref.dtype)


def _expert_ffn(xs, blk_exp, w_up, b_up, w_down, b_down, layer):
    p, d = xs.shape
    grid_spec = pltpu.PrefetchScalarGridSpec(
        num_scalar_prefetch=1,
        grid=(p // MOE_ROWS,),
        in_specs=[pl.BlockSpec((MOE_ROWS, d), lambda i, e: (i, 0)),
                  pl.BlockSpec((None, None, d, 2 * D_FF), lambda i, e: (layer, e[i], 0, 0)),
                  pl.BlockSpec((None, None, 1, 2 * D_FF), lambda i, e: (layer, e[i], 0, 0)),
                  pl.BlockSpec((None, None, D_FF, d), lambda i, e: (layer, e[i], 0, 0)),
                  pl.BlockSpec((None, None, 1, d), lambda i, e: (layer, e[i], 0, 0))],
        out_specs=pl.BlockSpec((MOE_ROWS, d), lambda i, e: (i, 0)),
    )
    return pl.pallas_call(
        _expert_kernel,
        grid_spec=grid_spec,
        out_shape=jax.ShapeDtypeStruct((p, d), BF16),
        compiler_params=_params(("arbitrary",), 48),
        name="expert_ffn",
    )(blk_exp, xs, w_up, b_up, w_down, b_down)


def _moe(x, x_bf, w_r, b_r, w_up, b_up, w_down, b_down, layer):
    n, d = x.shape
    nk = n * TOP_K
    idx, gate, rank, count = _router(x, w_r, b_r)
    counts = count[0, :N_EXPERTS].astype(jnp.int32)
    start = jnp.cumsum(counts) - counts
    padded = (counts + MOE_ROWS - 1) // MOE_ROWS * MOE_ROWS
    pend = jnp.cumsum(padded)
    pstart = pend - padded
    experts = jnp.arange(N_EXPERTS, dtype=jnp.int32)
    chosen = idx[:, :TOP_K, None] == experts
    slot = jnp.sum(jnp.where(chosen, pstart, 0), axis=2) + rank[:, :TOP_K]
    n_blocks = -(-(nk + N_EXPERTS * (MOE_ROWS - 1)) // MOE_ROWS)
    first_row = jnp.arange(n_blocks, dtype=jnp.int32) * MOE_ROWS
    blk_exp = jnp.minimum(jnp.sum(pend[None, :] <= first_row[:, None], axis=1), N_EXPERTS - 1).astype(jnp.int32)
    order = jnp.argsort(slot.reshape(nk)).astype(jnp.int32)
    blk_is = blk_exp[:, None] == experts
    blk_start = jnp.sum(jnp.where(blk_is, start, 0), axis=1)
    blk_pstart = jnp.sum(jnp.where(blk_is, pstart, 0), axis=1)
    blk_count = jnp.sum(jnp.where(blk_is, counts, 0), axis=1)
    within = first_row[:, None] + jnp.arange(MOE_ROWS, dtype=jnp.int32)[None, :] - blk_pstart[:, None]
    pair = jnp.clip(blk_start[:, None] + within, 0, nk - 1)
    spare = (first_row[:, None] + jnp.arange(MOE_ROWS, dtype=jnp.int32)[None, :]) % n
    slot_tok = jnp.where(within < blk_count[:, None], order[pair] // TOP_K, spare).reshape(n_blocks * MOE_ROWS)
    slot_tok, slot_km = lax.optimization_barrier((slot_tok, slot.T.reshape(nk)))
    xs = x_bf[slot_tok]
    y = _expert_ffn(xs, blk_exp, w_up, b_up, w_down, b_down, layer)
    return y[slot_km].reshape(TOP_K, n, d), gate


def kernel(x, rel_bias, w_in, b_forget, b_gate, w_br_sb, w_br_dil, w_br_moba, w_br_fox, w_out,
           ln1_g, ln1_b, w_router, b_router, w_up, b_up, w_down, b_down, ln2_g, ln2_b):
    batch, seq, d = x.shape
    n = batch * seq
    depth = w_in.shape[0]

    w_in_t = jnp.transpose(w_in, (0, 2, 1))
    w_qkv = _qkv_weights(w_in_t)
    w_gate, w_f = _gate_weights(w_in_t)
    b_f = jnp.pad(b_forget, ((0, 0), (0, LANES - H_FOX))).reshape(depth, 1, LANES)
    b_gate3 = b_gate.reshape(depth, N_BRANCH, 1, d)
    w_branch = [w.astype(BF16) for w in (w_br_sb, w_br_dil, w_br_moba, w_br_fox)]
    w_out_bf = w_out.astype(BF16)
    w_r = jnp.pad(w_router, ((0, 0), (0, 0), (0, LANES - N_EXPERTS)))
    b_r = jnp.pad(b_router, ((0, 0), (0, LANES - N_EXPERTS))).reshape(depth, 1, LANES)
    b_up3 = b_up.reshape(depth, N_EXPERTS, 1, 2 * D_FF)
    b_down3 = b_down.reshape(depth, N_EXPERTS, 1, d)
    moba_bias = _moba_bias_tiles(rel_bias)
    dil_bias = _dilated_bias_tiles(rel_bias)

    xf = x.reshape(n, d)
    xb = xf.astype(BF16)
    for l in range(depth):
        qkv = _matmul(xb, w_qkv, l, 0, 3 * MAIN_W, 1024, MAIN_W, BF16, "qkv_proj")
        aug_q, aug_k = _decay(xb, w_f[l], b_f[l], batch)
        o_sb = _sb_attention(qkv, batch)
        os, lses = [], []
        o, lse = _dilated_group(qkv, MAIN_DIL, MAIN_W, dil_bias, batch, 0)
        os.append(o)
        lses.append(lse)
        for g in range(1, len(DIL_GROUPS)):
            dil = DIL_GROUPS[g][1]
            qkv_g = _matmul(xb, w_qkv, l, 3 * (MAIN_W + (g - 1) * DIL_GROUP_W),
                            3 * DIL_GROUP_W, 1024, 3 * DIL_GROUP_W, BF16, f"qkv_proj_g{g}")
            o, lse = _dilated_group(_by_residue(qkv_g, batch, dil), 0, DIL_GROUP_W, dil_bias, batch, g)
            os.append(_by_position(o, batch, dil))
            lses.append(_by_position(lse, batch, dil))
        o_dil = _dilated_combine(os, lses)
        o_moba = _moba_attention(qkv, moba_bias, batch)
        o_fox = _fox_attention(qkv, aug_q, aug_k, batch)
        merged = _gated_merge(xb, (o_sb, o_dil, o_moba, o_fox), w_gate, b_gate3, w_branch, l)
        xf, xb = _out_proj_ln(merged, w_out_bf, l, xf, ln1_g[l].reshape(1, d), ln1_b[l].reshape(1, d))
        yk, gate = _moe(xf, xb, w_r[l], b_r[l], w_up, b_up3, w_down, b_down3, l)
        xf, xb = _combine_ln(yk, gate, xf, ln2_g[l].reshape(1, d), ln2_b[l].reshape(1, d))
    return xf.reshape(batch, seq, d)
```

```python
import functools
import math

import numpy as np
import jax
import jax.numpy as jnp
from jax import lax
from jax.experimental import pallas as pl
from jax.experimental.pallas import tpu as pltpu

D_MODEL = 2048
DEPTH = 4
HEAD_DIM = 64
H_SB = 8
H_DIL_PER_GROUP = 4
DIL_GROUPS = ((128, 1), (512, 4), (2048, 16))
H_DIL = H_DIL_PER_GROUP * len(DIL_GROUPS)
H_MOBA = 8
H_FOX = 8
H_ALL = H_SB + H_DIL + H_MOBA + H_FOX
SB_OFF = 0
DIL_OFF = SB_OFF + H_SB
MOBA_OFF = DIL_OFF + H_DIL
FOX_OFF = MOBA_OFF + H_MOBA
W_MIX = H_ALL * HEAD_DIM
N_BRANCH = 4
C_QKV = 3 * W_MIX
MOBA_BLOCK = 256
MOBA_TOPK = 3
N_BUCKETS = 32
MAX_DIST = 2048
N_EXPERTS = 32
TOP_K = 4
D_FF = 768
SWIGLU_ALPHA = 1.702
SWIGLU_LIMIT = 7.0
LN_EPS = 1e-5
DN_ALPHA = (2 * DEPTH) ** 0.25

F32 = jnp.float32
BF16 = jnp.bfloat16
LANES = 128
NEG = -1e30
HIGHEST = lax.Precision.HIGHEST
MIB = 1024 * 1024
NT = (((1,), (1,)), ((), ()))
SIGN_BIT = np.uint32(0x80000000)

MAIN_SB = 0
MAIN_DIL = MAIN_SB + H_SB
MAIN_MOBA = MAIN_DIL + H_DIL_PER_GROUP
MAIN_FOX = MAIN_MOBA + H_MOBA
MAIN_W = (MAIN_FOX + H_FOX) * HEAD_DIM
DIL_GROUP_W = H_DIL_PER_GROUP * HEAD_DIM
DIL_W = 128
DIL_TQ = 256
MOE_ROWS = 256
SB_T = 1024
SB_SEG = 256
SB_DEAD_LOG = -110.0
FOX_TQ = 1024
FOX_TK = 1024
FOX_WIDE = 2
MOBA_T = 1024
MOBA_NT = 8
AUG_PARTS = 3


def _params(sem, vmem_mib):
    return pltpu.CompilerParams(dimension_semantics=sem, vmem_limit_bytes=vmem_mib * MIB)


def _t5_bucket(n):
    exact = N_BUCKETS // 2
    nf = jnp.maximum(n, 1).astype(F32)
    large = exact + (jnp.log(nf / exact) / math.log(MAX_DIST / exact) * (N_BUCKETS - exact)).astype(jnp.int32)
    large = jnp.minimum(large, N_BUCKETS - 1)
    return jnp.where(n < exact, n, large)


def _bias_lookup(table, bucket):
    onehot = (bucket[..., None] == jnp.arange(N_BUCKETS)).astype(F32)
    return jnp.einsum('...k,kh->h...', onehot, table.astype(F32), precision=HIGHEST)


def _head_pair(q):
    lane = lax.broadcasted_iota(jnp.int32, q.shape, 1)
    zero = jnp.zeros_like(q)
    return [jnp.where(lane < HEAD_DIM, q, zero), jnp.where(lane >= HEAD_DIM, q, zero)], lane


def _attention_specs(seq, nq, tq, q_col, k_col, v_col):
    return [pl.BlockSpec((tq, LANES), lambda b, p, i: (b * nq + i, q_col + p)),
            pl.BlockSpec((seq, LANES), lambda b, p, i: (b, k_col + p)),
            pl.BlockSpec((seq, LANES), lambda b, p, i: (b, v_col + p))]


def _head_cols(off, width):
    q_col = off * HEAD_DIM // LANES
    return q_col, q_col + width // LANES, q_col + 2 * width // LANES


def _matmul_kernel(a_ref, w_ref, o_ref):
    o_ref[...] = jnp.dot(a_ref[...], w_ref[...], preferred_element_type=F32).astype(o_ref.dtype)


def _matmul(a, w, layer, col0, n, tm, tn, out_dtype, name):
    m, k = a.shape
    first = col0 // tn
    return pl.pallas_call(
        _matmul_kernel,
        grid=(n // tn, m // tm),
        in_specs=[pl.BlockSpec((tm, k), lambda j, i: (i, 0)),
                  pl.BlockSpec((None, k, tn), lambda j, i: (layer, 0, first + j))],
        out_specs=pl.BlockSpec((tm, tn), lambda j, i: (i, j)),
        out_shape=jax.ShapeDtypeStruct((m, n), out_dtype),
        compiler_params=_params(("parallel", "parallel"), 40),
        name=name,
    )(a, w)


QKV_TILE = 256
MAIN_HEADS = ((SB_OFF, H_SB), (DIL_OFF, H_DIL_PER_GROUP), (MOBA_OFF, H_MOBA), (FOX_OFF, H_FOX))


def _qkv_column_plan():
    groups = [MAIN_HEADS] + [((DIL_OFF + g * H_DIL_PER_GROUP, H_DIL_PER_GROUP),) for g in range(1, len(DIL_GROUPS))]
    src, is_q = [], []
    for heads in groups:
        for sec in range(3):
            for lo, cnt in heads:
                start = sec * W_MIX + lo * HEAD_DIM
                assert start % QKV_TILE == 0 and (cnt * HEAD_DIM) % QKV_TILE == 0
                for b in range(cnt * HEAD_DIM // QKV_TILE):
                    src.append(start // QKV_TILE + b)
                    is_q.append(int(sec == 0))
    assert sorted(src) == list(range(C_QKV // QKV_TILE))
    return np.asarray(src, np.int32), np.asarray(is_q, np.int32)


def _qkv_weight_kernel(src_ref, is_q_ref, w_ref, o_ref):
    del src_ref
    scale = jnp.where(is_q_ref[pl.program_id(1)] == 1, HEAD_DIM ** -0.5, 1.0)
    o_ref[...] = (w_ref[...].T * scale).astype(BF16)


def _qkv_weights(w_in_t):
    depth, _, d = w_in_t.shape
    src, is_q = _qkv_column_plan()
    grid_spec = pltpu.PrefetchScalarGridSpec(
        num_scalar_prefetch=2,
        grid=(depth, len(src)),
        in_specs=[pl.BlockSpec((None, QKV_TILE, d), lambda l, j, src, is_q: (l, src[j], 0))],
        out_specs=pl.BlockSpec((None, d, QKV_TILE), lambda l, j, src, is_q: (l, 0, j)),
    )
    return pl.pallas_call(
        _qkv_weight_kernel,
        grid_spec=grid_spec,
        out_shape=jax.ShapeDtypeStruct((depth, d, C_QKV), BF16),
        compiler_params=_params(("parallel", "parallel"), 24),
        name="qkv_weights",
    )(jnp.asarray(src), jnp.asarray(is_q), w_in_t)


GATE_COL0 = C_QKV + H_FOX
GATE_ALIGNED = GATE_COL0 // LANES * LANES
GATE_TILE = 256


def _gate_weight_kernel(a_ref, b_ref, o_ref, wf_ref):
    shift = GATE_COL0 - GATE_ALIGNED
    a = a_ref[...]
    both = jnp.concatenate([a, b_ref[...]], axis=0)
    o_ref[...] = both[shift:shift + GATE_TILE].T.astype(BF16)

    @pl.when(pl.program_id(1) == 0)
    def _():
        head = a[:LANES].T
        lane = lax.broadcasted_iota(jnp.int32, head.shape, 1)
        wf_ref[...] = jnp.where(lane < H_FOX, head, 0.0)


def _gate_weights(w_in_t):
    depth, _, d = w_in_t.shape
    width = N_BRANCH * d
    assert GATE_ALIGNED % GATE_TILE == 0 and width % GATE_TILE == 0 and GATE_ALIGNED == C_QKV
    a0 = GATE_ALIGNED // GATE_TILE
    per = GATE_TILE // LANES
    return pl.pallas_call(
        _gate_weight_kernel,
        grid=(depth, width // GATE_TILE),
        in_specs=[pl.BlockSpec((None, GATE_TILE, d), lambda l, j: (l, a0 + j, 0)),
                  pl.BlockSpec((None, LANES, d), lambda l, j: (l, (a0 + j + 1) * per, 0))],
        out_specs=[pl.BlockSpec((None, d, GATE_TILE), lambda l, j: (l, 0, j)),
                   pl.BlockSpec((None, d, LANES), lambda l, j: (l, 0, 0))],
        out_shape=[jax.ShapeDtypeStruct((depth, d, width), BF16), jax.ShapeDtypeStruct((depth, d, LANES), F32)],
        compiler_params=_params(("parallel", "arbitrary"), 24),
        name="gate_weights",
    )(w_in_t, w_in_t)


def _aug_constants():
    pairs = H_FOX // 2
    place_k = np.zeros((AUG_PARTS, LANES, pairs * LANES), np.float32)
    place_q = np.zeros((AUG_PARTS, LANES, pairs * LANES), np.float32)
    ones_k = np.zeros((1, pairs * LANES), np.float32)
    ones_q = np.zeros((1, pairs * LANES), np.float32)
    for g in range(H_FOX):
        p, h = divmod(g, 2)
        base = p * LANES + HEAD_DIM * (1 - h)
        for j in range(AUG_PARTS):
            place_k[j, g, base + j] = 1.0
            place_q[j, g, base + AUG_PARTS + j] = 1.0
            ones_k[0, base + AUG_PARTS + j] = 1.0
            ones_q[0, base + j] = 1.0
    return place_k, place_q, ones_k, ones_q


def _split_parts(x):
    parts = []
    for _ in range(AUG_PARTS):
        piece = x.astype(BF16)
        parts.append(piece)
        x = x - piece.astype(F32)
    return parts


def _decay_kernel(x_ref, whi_ref, wlo_ref, bf_ref, pk_ref, pq_ref, ok_ref, oq_ref, augq_ref, augk_ref, carry_ref):
    @pl.when(pl.program_id(1) == 0)
    def _():
        carry_ref[...] = jnp.zeros_like(carry_ref)

    tm = x_ref.shape[0]
    x = x_ref[...]
    f = (jnp.dot(x, whi_ref[...], preferred_element_type=F32)
         + jnp.dot(x, wlo_ref[...], preferred_element_type=F32) + bf_ref[...])
    logf = jnp.minimum(f, 0.0) - jnp.log1p(jnp.exp(-jnp.abs(f)))
    row = lax.broadcasted_iota(jnp.int32, (tm, tm), 0)
    col = lax.broadcasted_iota(jnp.int32, (tm, tm), 1)
    tri = jnp.where(col <= row, 1.0, 0.0).astype(F32)
    c = jnp.dot(tri, logf, preferred_element_type=F32, precision=HIGHEST) + carry_ref[...]
    carry_ref[...] = c[tm - 1:tm, :]
    aug_q = oq_ref[...]
    aug_k = ok_ref[...]
    for j, piece in enumerate(_split_parts(c)):
        aug_q = aug_q + jnp.dot(piece, pq_ref[j], preferred_element_type=F32)
        aug_k = aug_k - jnp.dot(piece, pk_ref[j], preferred_element_type=F32)
    augq_ref[...] = aug_q.astype(BF16)
    augk_ref[...] = aug_k.astype(BF16)


def _decay(x, w_f, b_f, batch, tm=256):
    n, d = x.shape
    nb = n // batch // tm
    width = H_FOX // 2 * LANES
    place_k, place_q, ones_k, ones_q = _aug_constants()
    w_hi = w_f.astype(BF16)
    w_lo = (w_f - w_hi.astype(F32)).astype(BF16)
    const = lambda a: pl.BlockSpec(a.shape, lambda b, i: (0,) * a.ndim)
    out = pl.BlockSpec((tm, width), lambda b, i: (b * nb + i, 0))
    return pl.pallas_call(
        _decay_kernel,
        grid=(batch, nb),
        in_specs=[pl.BlockSpec((tm, d), lambda b, i: (b * nb + i, 0)),
                  pl.BlockSpec((d, LANES), lambda b, i: (0, 0)),
                  pl.BlockSpec((d, LANES), lambda b, i: (0, 0)),
                  pl.BlockSpec((1, LANES), lambda b, i: (0, 0)),
                  const(place_k), const(place_q), const(ones_k), const(ones_q)],
        out_specs=[out, out],
        out_shape=[jax.ShapeDtypeStruct((n, width), BF16)] * 2,
        scratch_shapes=[pltpu.VMEM((1, LANES), F32)],
        compiler_params=_params(("arbitrary", "arbitrary"), 24),
        name="fox_decay",
    )(x, w_hi, w_lo, b_f, jnp.asarray(place_k, BF16), jnp.asarray(place_q, BF16), jnp.asarray(ones_k),
      jnp.asarray(ones_q))


def _sb_kernel(q_ref, k_ref, v_ref, o_ref):
    t, seg = SB_T, SB_SEG
    nseg = t // seg
    i = pl.program_id(2)
    qh, lane = _head_pair(q_ref[...])
    qneg = [-x for x in qh]
    r = lax.broadcasted_iota(jnp.int32, (seg, seg), 0)
    c = lax.broadcasted_iota(jnp.int32, (seg, seg), 1)
    later = jnp.where(r > c, 1.0, 0.0).astype(BF16)
    row = lax.broadcasted_iota(jnp.int32, (t, t), 0)
    col = lax.broadcasted_iota(jnp.int32, (t, t), 1)
    past = col < row

    def step(j, carry, diagonal):
        start = pl.multiple_of(j * t, t)
        k = k_ref[pl.ds(start, t), :]
        v = v_ref[pl.ds(start, t), :]
        out = []
        for h in range(2):
            tail, acc = carry[h]
            nz = lax.dot_general(qneg[h], k, NT, preferred_element_type=F32)
            minus_abs = lax.bitcast_convert_type(lax.bitcast_convert_type(nz, jnp.uint32) | SIGN_BIT, F32)
            log_1m = jnp.minimum(nz, 0.0) - jnp.log(1.0 + jnp.exp(minus_abs))
            base = log_1m - nz
            if diagonal:
                log_1m = jnp.where(past, log_1m, 0.0)
            sums = [jnp.dot(log_1m[:, b * seg:(b + 1) * seg].astype(BF16), later, preferred_element_type=F32)
                    for b in range(nseg)]
            for b in reversed(range(nseg)):
                sl = slice(b * seg, (b + 1) * seg)
                w = jnp.exp(base[:, sl] + (sums[b] + tail))
                if diagonal:
                    w = jnp.where(past[:, sl], w, 0.0)
                acc = acc + jnp.dot(w.astype(BF16), v[sl], preferred_element_type=F32)
                tail = tail + (sums[b][:, 0:1] + log_1m[:, b * seg:b * seg + 1])
            out.append((tail, acc))
        return tuple(out)

    init = tuple((jnp.zeros((t, 1), F32), jnp.zeros((t, LANES), F32)) for _ in range(2))
    carry = step(i, init, True)

    def live(state):
        jj, carry = state
        return (jj < i) & (jnp.maximum(jnp.max(carry[0][0]), jnp.max(carry[1][0])) > SB_DEAD_LOG)

    def walk(state):
        jj, carry = state
        return jj + 1, step(i - 1 - jj, carry, False)

    _, carry = lax.while_loop(live, walk, (jnp.int32(0), carry))
    o_ref[...] = jnp.where(lane < HEAD_DIM, carry[0][1], carry[1][1]).astype(o_ref.dtype)


def _sb_attention(qkv, batch):
    n = qkv.shape[0]
    seq = n // batch
    nq = seq // SB_T
    pairs = H_SB // 2
    return pl.pallas_call(
        _sb_kernel,
        grid=(batch, pairs, nq),
        in_specs=_attention_specs(seq, nq, SB_T, *_head_cols(MAIN_SB, MAIN_W)),
        out_specs=pl.BlockSpec((SB_T, LANES), lambda b, p, i: (b * nq + i, p)),
        out_shape=jax.ShapeDtypeStruct((n, pairs * LANES), BF16),
        compiler_params=_params(("parallel", "parallel", "arbitrary"), 56),
        name="sb_attention",
    )(qkv, qkv, qkv)


def _fox_kernel(q_ref, k_ref, v_ref, aq_ref, ak_ref, o_ref):
    tq, tk = FOX_TQ, FOX_TK
    i = pl.program_id(2)
    q = q_ref[...]
    lane = lax.broadcasted_iota(jnp.int32, q.shape, 1)
    own = [lane < HEAD_DIM, lane >= HEAD_DIM]
    qh = [jnp.where(own[h], q, aq_ref[...]) for h in range(2)]
    klane = lax.broadcasted_iota(jnp.int32, (tk, LANES), 1)
    kown = [klane < HEAD_DIM, klane >= HEAD_DIM]
    row = lax.broadcasted_iota(jnp.int32, (tq, tk), 0)
    col = lax.broadcasted_iota(jnp.int32, (tq, tk), 1)

    def step(start, width, carry, diagonal):
        k = k_ref[pl.ds(start, width), :]
        v = v_ref[pl.ds(start, width), :]
        ak = ak_ref[pl.ds(start, width), :]
        kown_w = [jnp.concatenate([x] * (width // tk), axis=0) if width > tk else x for x in kown]
        out = []
        for h in range(2):
            s = lax.dot_general(qh[h], jnp.where(kown_w[h], k, ak), NT, preferred_element_type=F32)
            if diagonal:
                s = jnp.where(col + start <= row + i * tq, s, NEG)
            m, acc = carry[h]
            m_new = jnp.maximum(m, jnp.max(s, axis=1, keepdims=True))
            p = jnp.exp(s - m_new).astype(BF16)
            vh = jnp.where(kown_w[h], v, jnp.ones_like(v))
            acc = jnp.exp(m - m_new) * acc + jnp.dot(p, vh, preferred_element_type=F32)
            out.append((m_new, acc))
        return tuple(out)

    assert tq == tk
    carry = tuple((jnp.full((tq, 1), NEG, F32), jnp.zeros((tq, LANES), F32)) for _ in range(2))
    wide = FOX_WIDE * tk
    n_wide = (i * tq) // wide
    carry = lax.fori_loop(0, n_wide, lambda j, c: step(pl.multiple_of(j * wide, wide), wide, c, False), carry)
    rest = (i * tq - n_wide * wide) // tk
    carry = lax.fori_loop(0, rest, lambda j, c: step(pl.multiple_of(n_wide * wide + j * tk, tk), tk, c, False),
                          carry)
    carry = step(pl.multiple_of(i * tq, tk), tk, carry, True)
    o = []
    for h in range(2):
        acc = carry[h][1]
        o.append(acc / acc[:, HEAD_DIM * (1 - h):HEAD_DIM * (1 - h) + 1])
    o_ref[...] = jnp.where(own[0], o[0], o[1]).astype(o_ref.dtype)


def _fox_attention(qkv, aug_q, aug_k, batch):
    n = qkv.shape[0]
    seq = n // batch
    nq = seq // FOX_TQ
    pairs = H_FOX // 2
    specs = _attention_specs(seq, nq, FOX_TQ, *_head_cols(MAIN_FOX, MAIN_W))
    specs += [pl.BlockSpec((FOX_TQ, LANES), lambda b, p, i: (b * nq + i, p)),
              pl.BlockSpec((seq, LANES), lambda b, p, i: (b, p))]
    return pl.pallas_call(
        _fox_kernel,
        grid=(batch, pairs, nq),
        in_specs=specs,
        out_specs=pl.BlockSpec((FOX_TQ, LANES), lambda b, p, i: (b * nq + i, p)),
        out_shape=jax.ShapeDtypeStruct((n, pairs * LANES), BF16),
        compiler_params=_params(("parallel", "parallel", "arbitrary"), 48),
        name="fox_attention",
    )(qkv, qkv, qkv, aug_q, aug_k)


def _moba_kernel(q_ref, k_ref, v_ref, onehot_ref, bias_ref, o_ref, kmean_ref):
    t, mb = MOBA_T, MOBA_BLOCK
    nb = t // mb
    seq = k_ref.shape[0]
    i = pl.program_id(2)

    @pl.when(i == 0)
    def _():
        blk = lax.broadcasted_iota(jnp.int32, (LANES, seq), 0)
        pos = lax.broadcasted_iota(jnp.int32, (LANES, seq), 1)
        member = jnp.where((pos >= blk * mb) & (pos < (blk + 1) * mb), 1.0, 0.0).astype(BF16)
        mean = jnp.dot(member, k_ref[...], preferred_element_type=F32) * (1.0 / mb)
        hi = mean.astype(BF16)
        kmean_ref[0] = hi
        kmean_ref[1] = (mean - hi.astype(F32)).astype(BF16)

    q = q_ref[...]
    lane = lax.broadcasted_iota(jnp.int32, (t, LANES), 1)
    own_half = [lane < HEAD_DIM, lane >= HEAD_DIM]
    zero = jnp.zeros_like(q)

    nblk = seq // mb
    blk_row = lax.broadcasted_iota(jnp.int32, (nblk, t), 0).astype(F32)
    own_row = ((i * t + lax.broadcasted_iota(jnp.int32, (nblk, t), 1)) // mb).astype(F32)
    fully_past = blk_row < own_row
    place_r = lax.broadcasted_iota(jnp.int32, (nblk, LANES), 0)
    place_c = lax.broadcasted_iota(jnp.int32, (nblk, LANES), 1)
    qh = []
    for h in range(2):
        q_only = jnp.where(own_half[h], q, zero)
        gate = (lax.dot_general(kmean_ref[0][:nblk], q_only, NT, preferred_element_type=F32)
                + lax.dot_general(kmean_ref[1][:nblk], q_only, NT, preferred_element_type=F32))
        gate = jnp.where(fully_past, gate, -jnp.inf)
        picked = jnp.zeros((nblk, t), jnp.bool_)
        for _ in range(MOBA_TOPK):
            best = jnp.max(gate, axis=0, keepdims=True)
            first = jnp.min(jnp.where(gate == best, blk_row, float(LANES)), axis=0, keepdims=True)
            pick = blk_row == first
            picked = picked | pick
            gate = jnp.where(pick, -jnp.inf, gate)
        allowed = jnp.where((picked & fully_past) | (blk_row == own_row), 1.0, 0.0).astype(BF16)
        place = jnp.where(place_c == place_r + HEAD_DIM * (1 - h), 1.0, 0.0).astype(BF16)
        spread = lax.dot_general(allowed, place, (((0,), (0,)), ((), ())), preferred_element_type=F32)
        penalty = jnp.where(spread > 0.5, 0.0, NEG)
        qh.append(jnp.where(own_half[h], q, penalty.astype(BF16)))

    def step(j, carry):
        start = pl.multiple_of(j * t, t)
        k = k_ref[pl.ds(start, t), :]
        v = v_ref[pl.ds(start, t), :]
        onehot = onehot_ref[pl.ds(start, t), :]
        out = []
        for h in range(2):
            s = lax.dot_general(qh[h], jnp.where(own_half[h], k, onehot), NT, preferred_element_type=F32)
            rows = []
            for a in range(nb):
                tiles = [bias_ref[h, jnp.clip((i - j) * nb + a - b, 0, MOBA_NT)] for b in range(nb)]
                rows.append(jnp.concatenate(tiles, axis=1))
            s = s + jnp.concatenate(rows, axis=0)
            m, acc = carry[h]
            m_new = jnp.maximum(m, jnp.max(s, axis=1, keepdims=True))
            p = jnp.exp(s - m_new).astype(BF16)
            vh = jnp.where(own_half[h], v, jnp.ones_like(v))
            acc = jnp.exp(m - m_new) * acc + jnp.dot(p, vh, preferred_element_type=F32)
            out.append((m_new, acc))
        return tuple(out)

    carry = tuple((jnp.full((t, 1), NEG, F32), jnp.zeros((t, LANES), F32)) for _ in range(2))
    carry = lax.fori_loop(0, i + 1, step, carry)
    o = []
    for h in range(2):
        acc = carry[h][1]
        o.append(acc / acc[:, HEAD_DIM * (1 - h):HEAD_DIM * (1 - h) + 1])
    o_ref[...] = jnp.where(own_half[0], o[0], o[1]).astype(o_ref.dtype)


def _moba_attention(qkv, bias_tiles, batch):
    n = qkv.shape[0]
    seq = n // batch
    assert seq % MOBA_T == 0 and seq // MOBA_BLOCK <= HEAD_DIM
    nq = seq // MOBA_T
    pairs = H_MOBA // 2
    blk = np.arange(seq)[:, None] // MOBA_BLOCK
    onehot = jnp.asarray(blk == (np.arange(LANES)[None, :] % HEAD_DIM), BF16)
    specs = _attention_specs(seq, nq, MOBA_T, *_head_cols(MAIN_MOBA, MAIN_W))
    specs.append(pl.BlockSpec((seq, LANES), lambda b, p, i: (0, 0)))
    specs.append(pl.BlockSpec((None, 2, MOBA_NT + 1, MOBA_BLOCK, MOBA_BLOCK), lambda b, p, i: (p, 0, 0, 0, 0)))
    return pl.pallas_call(
        _moba_kernel,
        grid=(batch, pairs, nq),
        in_specs=specs,
        out_specs=pl.BlockSpec((MOBA_T, LANES), lambda b, p, i: (b * nq + i, p)),
        out_shape=jax.ShapeDtypeStruct((n, pairs * LANES), BF16),
        scratch_shapes=[pltpu.VMEM((2, LANES, LANES), BF16)],
        compiler_params=_params(("arbitrary", "arbitrary", "arbitrary"), 56),
        name="moba_attention",
    )(qkv, qkv, qkv, onehot, bias_tiles)


def _moba_bias_tiles(rel_bias):
    r = jnp.arange(MOBA_BLOCK)
    dist = (jnp.arange(MOBA_NT + 1)[:, None, None] * MOBA_BLOCK + r[None, :, None] - r[None, None, :])
    tiles = _bias_lookup(rel_bias[:, H_DIL:], _t5_bucket(jnp.maximum(dist, 0)))
    tiles = jnp.where(dist >= 0, tiles, NEG)
    return tiles.reshape(H_MOBA // 2, 2, MOBA_NT + 1, MOBA_BLOCK, MOBA_BLOCK).astype(F32)


def _check_moba_saturation():
    n = np.float32(MOBA_NT * MOBA_BLOCK - (MOBA_BLOCK - 1))
    exact = N_BUCKETS // 2
    large = exact + int(np.log(n / np.float32(exact)) / math.log(MAX_DIST / exact) * (N_BUCKETS - exact))
    assert large >= N_BUCKETS - 1, "MOBA_NT too small for the bias bucket table"


_check_moba_saturation()


def _dilated_kernel(q_ref, k_ref, v_ref, bias_ref, o_ref, lse_ref):
    tq, tk = DIL_TQ, DIL_TQ + DIL_W
    n_tiles = q_ref.shape[0] // tq

    def body(u, _):
        q0 = pl.multiple_of(u * tq, tq)
        k0 = pl.multiple_of(jnp.maximum(q0 - DIL_W, 0), DIL_W)
        variant = jnp.minimum(u, 1)
        qh, lane = _head_pair(q_ref[pl.ds(q0, tq), :])
        k = k_ref[pl.ds(k0, tk), :]
        v = v_ref[pl.ds(k0, tk), :]
        outs, lses = [], []
        for h in range(2):
            z = lax.dot_general(qh[h], k, NT, preferred_element_type=F32) + bias_ref[h, variant]
            m = jnp.max(z, axis=1, keepdims=True)
            p = jnp.exp(z - m)
            l = jnp.sum(p, axis=1, keepdims=True)
            outs.append(jnp.dot(p.astype(BF16), v, preferred_element_type=F32) / l)
            lses.append(m + jnp.log(l))
        o_ref[pl.ds(q0, tq), :] = jnp.where(lane < HEAD_DIM, outs[0], outs[1])
        lse_ref[pl.ds(q0, tq), :] = jnp.where(lane < HEAD_DIM, lses[0], lses[1])
        return 0

    lax.fori_loop(0, n_tiles, body, 0)


def _dilated_group(qkv, head_off, width, bias, batch, group):
    window, dil = DIL_GROUPS[group]
    assert window // dil == DIL_W
    n = qkv.shape[0]
    sub = n // batch // dil
    assert sub % DIL_TQ == 0 and sub >= DIL_TQ + DIL_W
    pairs = H_DIL_PER_GROUP // 2
    q_col, k_col, v_col = _head_cols(head_off, width)

    def spec(col0):
        return pl.BlockSpec((sub, LANES), lambda b, p, r: (b * dil + r, col0 + p))

    out_spec = pl.BlockSpec((sub, LANES), lambda b, p, r: (b * dil + r, p))
    out_shape = jax.ShapeDtypeStruct((n, pairs * LANES), F32)
    return pl.pallas_call(
        _dilated_kernel,
        grid=(batch, pairs, dil),
        in_specs=[spec(q_col), spec(k_col), spec(v_col),
                  pl.BlockSpec((None, 2, 2, DIL_TQ, DIL_TQ + DIL_W),
                               lambda b, p, r: (group * pairs + p, 0, 0, 0, 0))],
        out_specs=[out_spec, out_spec],
        out_shape=[out_shape, out_shape],
        compiler_params=_params(("parallel", "parallel", "parallel"), 48),
        name=f"dilated_attention_g{group}",
    )(qkv, qkv, qkv, bias)


def _by_residue(a, batch, dil):
    n, c = a.shape
    return a.reshape(batch, n // batch // dil, dil, c).transpose(0, 2, 1, 3).reshape(n, c)


def _by_position(a, batch, dil):
    n, c = a.shape
    return a.reshape(batch, dil, n // batch // dil, c).transpose(0, 2, 1, 3).reshape(n, c)


def _dilated_bias_tiles(rel_bias):
    r = jnp.arange(DIL_TQ)[:, None]
    c = jnp.arange(DIL_TQ + DIL_W)[None, :]
    steps = jnp.stack([r - c, r + DIL_W - c], 0)
    inside = (steps >= 0) & (steps <= DIL_W)
    tiles = []
    for g, (_, dil) in enumerate(DIL_GROUPS):
        tab = rel_bias[:, g * H_DIL_PER_GROUP:(g + 1) * H_DIL_PER_GROUP]
        bias = _bias_lookup(tab, _t5_bucket(jnp.maximum(steps, 0) * dil))
        tiles.append(jnp.where(inside, bias, NEG))
    tiles = jnp.stack(tiles, 0).astype(F32)
    return tiles.reshape(len(DIL_GROUPS) * H_DIL_PER_GROUP // 2, 2, 2, DIL_TQ, DIL_TQ + DIL_W)


def _dilated_combine_kernel(o0, o1, o2, l0, l1, l2, out_ref):
    lse = [l0[...], l1[...], l2[...]]
    m = jnp.maximum(jnp.maximum(lse[0], lse[1]), lse[2])
    e = [jnp.exp(x - m) for x in lse]
    den = e[0] + e[1] + e[2]
    out = (e[0] / den) * o0[...] + (e[1] / den) * o1[...] + (e[2] / den) * o2[...]
    out_ref[...] = out.astype(out_ref.dtype)


def _dilated_combine(os, lses, tm=1024):
    n, c = os[0].shape
    spec = pl.BlockSpec((tm, c), lambda i: (i, 0))
    return pl.pallas_call(
        _dilated_combine_kernel,
        grid=(n // tm,),
        in_specs=[spec] * 6,
        out_specs=spec,
        out_shape=jax.ShapeDtypeStruct((n, c), BF16),
        compiler_params=_params(("parallel",), 32),
        name="dilated_combine",
    )(*os, *lses)


def _merge_kernel(x_ref, o_sb, o_dil, o_moba, o_fox, wg0, wg1, wg2, wg3, bg_ref,
                  w_sb, w_dil, w_moba, w_fox, out_ref):
    x = x_ref[...]
    acc = None
    branches = ((o_sb, w_sb, wg0), (o_dil, w_dil, wg1), (o_moba, w_moba, wg2), (o_fox, w_fox, wg3))
    for b, (o_ref, w_ref, wg_ref) in enumerate(branches):
        gate = jax.nn.sigmoid(jnp.dot(x, wg_ref[...], preferred_element_type=F32) + bg_ref[b])
        term = gate * jnp.dot(o_ref[...], w_ref[...], preferred_element_type=F32)
        acc = term if acc is None else acc + term
    out_ref[...] = acc.astype(out_ref.dtype)


def _gated_merge(x_bf, outs, w_gate, b_gate, w_branch, layer, tm=512, tn=512):
    n, d = x_bf.shape
    nj = d // tn
    row = lambda width: pl.BlockSpec((tm, width), lambda j, i: (i, 0))
    in_specs = [row(d)] + [row(o.shape[1]) for o in outs]
    in_specs += [pl.BlockSpec((None, d, tn), functools.partial(lambda j, i, b: (layer, 0, b * nj + j), b=b))
                 for b in range(N_BRANCH)]
    in_specs.append(pl.BlockSpec((None, N_BRANCH, 1, tn), lambda j, i: (layer, 0, 0, j)))
    in_specs += [pl.BlockSpec((None, w.shape[1], tn), lambda j, i: (layer, 0, j)) for w in w_branch]
    return pl.pallas_call(
        _merge_kernel,
        grid=(nj, n // tm),
        in_specs=in_specs,
        out_specs=pl.BlockSpec((tm, tn), lambda j, i: (i, j)),
        out_shape=jax.ShapeDtypeStruct((n, d), BF16),
        compiler_params=_params(("parallel", "parallel"), 48),
        name="gated_merge",
    )(x_bf, *outs, w_gate, w_gate, w_gate, w_gate, b_gate, *w_branch)


def _layer_norm_store(z, g_ref, b_ref, xo_ref, xb_ref):
    mu = jnp.mean(z, axis=1, keepdims=True)
    zc = z - mu
    var = jnp.mean(zc * zc, axis=1, keepdims=True)
    out = zc * lax.rsqrt(var + LN_EPS) * g_ref[...] + b_ref[...]
    xo_ref[...] = out
    xb_ref[...] = out.astype(BF16)


def _out_ln_kernel(m_ref, w_ref, x_ref, g_ref, b_ref, xo_ref, xb_ref):
    y = jnp.dot(m_ref[...], w_ref[...], preferred_element_type=F32)
    _layer_norm_store(DN_ALPHA * x_ref[...] + y, g_ref, b_ref, xo_ref, xb_ref)


def _combine_ln_kernel(y_ref, gate_ref, x_ref, g_ref, b_ref, xo_ref, xb_ref):
    gate = gate_ref[...]
    z = DN_ALPHA * x_ref[...]
    for k in range(TOP_K):
        z = z + gate[:, k:k + 1] * y_ref[k].astype(F32)
    _layer_norm_store(z, g_ref, b_ref, xo_ref, xb_ref)


def _ln_call(kernel, lead_specs, lead_args, x, g, b, tm, name):
    n, d = x.shape
    row = pl.BlockSpec((tm, d), lambda i: (i, 0))
    vec = pl.BlockSpec((1, d), lambda i: (0, 0))
    return pl.pallas_call(
        kernel,
        grid=(n // tm,),
        in_specs=lead_specs + [row, vec, vec],
        out_specs=[row, row],
        out_shape=[jax.ShapeDtypeStruct((n, d), F32), jax.ShapeDtypeStruct((n, d), BF16)],
        compiler_params=_params(("parallel",), 48),
        name=name,
    )(*lead_args, x, g, b)


def _out_proj_ln(merged, w_out, layer, x, g, b, tm=256):
    d = x.shape[1]
    specs = [pl.BlockSpec((tm, d), lambda i: (i, 0)), pl.BlockSpec((None, d, d), lambda i: (layer, 0, 0))]
    return _ln_call(_out_ln_kernel, specs, (merged, w_out), x, g, b, tm, "out_proj_ln")


def _combine_ln(yk, gate, x, g, b, tm=256):
    d = x.shape[1]
    specs = [pl.BlockSpec((TOP_K, tm, d), lambda i: (0, i, 0)), pl.BlockSpec((tm, LANES), lambda i: (i, 0))]
    return _ln_call(_combine_ln_kernel, specs, (yk, gate), x, g, b, tm, "moe_combine_ln")


def _router_kernel(x_ref, w_ref, b_ref, idx_ref, gate_ref, rank_ref, count_ref, seen_ref):
    @pl.when(pl.program_id(0) == 0)
    def _():
        seen_ref[...] = jnp.zeros_like(seen_ref)

    tm = x_ref.shape[0]
    logits = jnp.dot(x_ref[...], w_ref[...], preferred_element_type=F32, precision=HIGHEST) + b_ref[...]
    col = lax.broadcasted_iota(jnp.int32, logits.shape, 1)
    colf = col.astype(F32)
    logits = jnp.where(col < N_EXPERTS, logits, -jnp.inf)
    vals, idxs, hits = [], [], []
    for _ in range(TOP_K):
        best = jnp.max(logits, axis=1, keepdims=True)
        first = jnp.min(jnp.where(logits == best, colf, float(LANES)), axis=1, keepdims=True)
        hit = colf == first
        logits = jnp.where(hit, -jnp.inf, logits)
        vals.append(best)
        idxs.append(first)
        hits.append(jnp.where(hit, 1.0, 0.0))
    es = [jnp.exp(v - vals[0]) for v in vals]
    den = es[0] + es[1] + es[2] + es[3]

    chosen = hits[0] + hits[1] + hits[2] + hits[3]
    row = lax.broadcasted_iota(jnp.int32, (tm, tm), 0)
    before = jnp.where(lax.broadcasted_iota(jnp.int32, (tm, tm), 1) < row, 1.0, 0.0).astype(BF16)
    earlier = jnp.dot(before, chosen.astype(BF16), preferred_element_type=F32) + seen_ref[...]
    seen = seen_ref[...] + jnp.sum(chosen, axis=0, keepdims=True)
    seen_ref[...] = seen
    count_ref[...] = seen

    idx_out = jnp.zeros(logits.shape, F32)
    gate_out = jnp.zeros(logits.shape, F32)
    rank_out = jnp.zeros(logits.shape, F32)
    for r in range(TOP_K):
        idx_out = jnp.where(col == r, idxs[r], idx_out)
        gate_out = jnp.where(col == r, es[r] / den, gate_out)
        rank_out = jnp.where(col == r, jnp.sum(hits[r] * earlier, axis=1, keepdims=True), rank_out)
    idx_ref[...] = idx_out.astype(jnp.int32)
    gate_ref[...] = gate_out
    rank_ref[...] = rank_out.astype(jnp.int32)


def _router(x, w_r, b_r, tm=512):
    n, d = x.shape
    out = pl.BlockSpec((tm, LANES), lambda i: (i, 0))
    return pl.pallas_call(
        _router_kernel,
        grid=(n // tm,),
        in_specs=[pl.BlockSpec((tm, d), lambda i: (i, 0)),
                  pl.BlockSpec((d, LANES), lambda i: (0, 0)),
                  pl.BlockSpec((1, LANES), lambda i: (0, 0))],
        out_specs=[out, out, out, pl.BlockSpec((1, LANES), lambda i: (0, 0))],
        out_shape=[jax.ShapeDtypeStruct((n, LANES), jnp.int32), jax.ShapeDtypeStruct((n, LANES), F32),
                   jax.ShapeDtypeStruct((n, LANES), jnp.int32), jax.ShapeDtypeStruct((1, LANES), F32)],
        scratch_shapes=[pltpu.VMEM((1, LANES), F32)],
        compiler_params=_params(("arbitrary",), 32),
        name="moe_router",
    )(x, w_r, b_r)


def _expert_kernel(blk_exp_ref, xs_ref, wu_ref, bu_ref, wd_ref, bd_ref, y_ref):
    del blk_exp_ref
    h = jnp.dot(xs_ref[...], wu_ref[...].astype(BF16), preferred_element_type=F32) + bu_ref[...]
    glu = jnp.minimum(h[:, :D_FF], SWIGLU_LIMIT)
    lin = jnp.clip(h[:, D_FF:], -SWIGLU_LIMIT, SWIGLU_LIMIT)
    act = glu * jax.nn.sigmoid(SWIGLU_ALPHA * glu) * (lin + 1.0)
    y = jnp.dot(act.astype(BF16), wd_ref[...].astype(BF16), preferred_element_type=F32) + bd_ref[...]
    y_ref[...] = y.astype(y_ref.dtype)


def _expert_ffn(xs, blk_exp, w_up, b_up, w_down, b_down, layer):
    p, d = xs.shape
    grid_spec = pltpu.PrefetchScalarGridSpec(
        num_scalar_prefetch=1,
        grid=(p // MOE_ROWS,),
        in_specs=[pl.BlockSpec((MOE_ROWS, d), lambda i, e: (i, 0)),
                  pl.BlockSpec((None, None, d, 2 * D_FF), lambda i, e: (layer, e[i], 0, 0)),
                  pl.BlockSpec((None, None, 1, 2 * D_FF), lambda i, e: (layer, e[i], 0, 0)),
                  pl.BlockSpec((None, None, D_FF, d), lambda i, e: (layer, e[i], 0, 0)),
                  pl.BlockSpec((None, None, 1, d), lambda i, e: (layer, e[i], 0, 0))],
        out_specs=pl.BlockSpec((MOE_ROWS, d), lambda i, e: (i, 0)),
    )
    return pl.pallas_call(
        _expert_kernel,
        grid_spec=grid_spec,
        out_shape=jax.ShapeDtypeStruct((p, d), BF16),
        compiler_params=_params(("arbitrary",), 48),
        name="expert_ffn",
    )(blk_exp, xs, w_up, b_up, w_down, b_down)


def _moe(x, x_bf, w_r, b_r, w_up, b_up, w_down, b_down, layer):
    n, d = x.shape
    nk = n * TOP_K
    idx, gate, rank, count = _router(x, w_r, b_r)
    counts = count[0, :N_EXPERTS].astype(jnp.int32)
    start = jnp.cumsum(counts) - counts
    padded = (counts + MOE_ROWS - 1) // MOE_ROWS * MOE_ROWS
    pend = jnp.cumsum(padded)
    pstart = pend - padded
    experts = jnp.arange(N_EXPERTS, dtype=jnp.int32)
    chosen = idx[:, :TOP_K, None] == experts
    slot = jnp.sum(jnp.where(chosen, pstart, 0), axis=2) + rank[:, :TOP_K]
    n_blocks = -(-(nk + N_EXPERTS * (MOE_ROWS - 1)) // MOE_ROWS)
    first_row = jnp.arange(n_blocks, dtype=jnp.int32) * MOE_ROWS
    blk_exp = jnp.minimum(jnp.sum(pend[None, :] <= first_row[:, None], axis=1), N_EXPERTS - 1).astype(jnp.int32)
    order = jnp.argsort(slot.reshape(nk)).astype(jnp.int32)
    blk_is = blk_exp[:, None] == experts
    blk_start = jnp.sum(jnp.where(blk_is, start, 0), axis=1)
    blk_pstart = jnp.sum(jnp.where(blk_is, pstart, 0), axis=1)
    blk_count = jnp.sum(jnp.where(blk_is, counts, 0), axis=1)
    within = first_row[:, None] + jnp.arange(MOE_ROWS, dtype=jnp.int32)[None, :] - blk_pstart[:, None]
    pair = jnp.clip(blk_start[:, None] + within, 0, nk - 1)
    spare = (first_row[:, None] + jnp.arange(MOE_ROWS, dtype=jnp.int32)[None, :]) % n
    slot_tok = jnp.where(within < blk_count[:, None], order[pair] // TOP_K, spare).reshape(n_blocks * MOE_ROWS)
    slot_tok, slot_km = lax.optimization_barrier((slot_tok, slot.T.reshape(nk)))
    xs = x_bf[slot_tok]
    y = _expert_ffn(xs, blk_exp, w_up, b_up, w_down, b_down, layer)
    return y[slot_km].reshape(TOP_K, n, d), gate


def kernel(x, rel_bias, w_in, b_forget, b_gate, w_br_sb, w_br_dil, w_br_moba, w_br_fox, w_out,
           ln1_g, ln1_b, w_router, b_router, w_up, b_up, w_down, b_down, ln2_g, ln2_b):
    batch, seq, d = x.shape
    n = batch * seq
    depth = w_in.shape[0]

    w_in_t = jnp.transpose(w_in, (0, 2, 1))
    w_qkv = _qkv_weights(w_in_t)
    w_gate, w_f = _gate_weights(w_in_t)
    b_f = jnp.pad(b_forget, ((0, 0), (0, LANES - H_FOX))).reshape(depth, 1, LANES)
    b_gate3 = b_gate.reshape(depth, N_BRANCH, 1, d)
    w_branch = [w.astype(BF16) for w in (w_br_sb, w_br_dil, w_br_moba, w_br_fox)]
    w_out_bf = w_out.astype(BF16)
    w_r = jnp.pad(w_router, ((0, 0), (0, 0), (0, LANES - N_EXPERTS)))
    b_r = jnp.pad(b_router, ((0, 0), (0, LANES - N_EXPERTS))).reshape(depth, 1, LANES)
    b_up3 = b_up.reshape(depth, N_EXPERTS, 1, 2 * D_FF)
    b_down3 = b_down.reshape(depth, N_EXPERTS, 1, d)
    moba_bias = _moba_bias_tiles(rel_bias)
    dil_bias = _dilated_bias_tiles(rel_bias)

    xf = x.reshape(n, d)
    xb = xf.astype(BF16)
    for l in range(depth):
        qkv = _matmul(xb, w_qkv, l, 0, 3 * MAIN_W, 1024, MAIN_W, BF16, "qkv_proj")
        aug_q, aug_k = _decay(xb, w_f[l], b_f[l], batch)
        o_sb = _sb_attention(qkv, batch)
        os, lses = [], []
        o, lse = _dilated_group(qkv, MAIN_DIL, MAIN_W, dil_bias, batch, 0)
        os.append(o)
        lses.append(lse)
        for g in range(1, len(DIL_GROUPS)):
            dil = DIL_GROUPS[g][1]
            qkv_g = _matmul(xb, w_qkv, l, 3 * (MAIN_W + (g - 1) * DIL_GROUP_W),
                            3 * DIL_GROUP_W, 1024, 3 * DIL_GROUP_W, BF16, f"qkv_proj_g{g}")
            o, lse = _dilated_group(_by_residue(qkv_g, batch, dil), 0, DIL_GROUP_W, dil_bias, batch, g)
            os.append(_by_position(o, batch, dil))
            lses.append(_by_position(lse, batch, dil))
        o_dil = _dilated_combine(os, lses)
        o_moba = _moba_attention(qkv, moba_bias, batch)
        o_fox = _fox_attention(qkv, aug_q, aug_k, batch)
        merged = _gated_merge(xb, (o_sb, o_dil, o_moba, o_fox), w_gate, b_gate3, w_branch, l)
        xf, xb = _out_proj_ln(merged, w_out_bf, l, xf, ln1_g[l].reshape(1, d), ln1_b[l].reshape(1, d))
        yk, gate = _moe(xf, xb, w_r[l], b_r[l], w_up, b_up3, w_down, b_down3, l)
        xf, xb = _combine_ln(yk, gate, xf, ln2_g[l].reshape(1, d), ln2_b[l].reshape(1, d))
    return xf.reshape(batch, seq, d)
```

```python
import functools
import math

import numpy as np
import jax
import jax.numpy as jnp
from jax import lax
from jax.experimental import pallas as pl
from jax.experimental.pallas import tpu as pltpu

D_MODEL = 2048
DEPTH = 4
HEAD_DIM = 64
H_SB = 8
H_DIL_PER_GROUP = 4
DIL_GROUPS = ((128, 1), (512, 4), (2048, 16))
H_DIL = H_DIL_PER_GROUP * len(DIL_GROUPS)
H_MOBA = 8
H_FOX = 8
H_ALL = H_SB + H_DIL + H_MOBA + H_FOX
SB_OFF = 0
DIL_OFF = SB_OFF + H_SB
MOBA_OFF = DIL_OFF + H_DIL
FOX_OFF = MOBA_OFF + H_MOBA
W_MIX = H_ALL * HEAD_DIM
N_BRANCH = 4
C_QKV = 3 * W_MIX
MOBA_BLOCK = 256
MOBA_TOPK = 3
N_BUCKETS = 32
MAX_DIST = 2048
N_EXPERTS = 32
TOP_K = 4
D_FF = 768
SWIGLU_ALPHA = 1.702
SWIGLU_LIMIT = 7.0
LN_EPS = 1e-5
DN_ALPHA = (2 * DEPTH) ** 0.25

F32 = jnp.float32
BF16 = jnp.bfloat16
LANES = 128
NEG = -1e30
HIGHEST = lax.Precision.HIGHEST
MIB = 1024 * 1024
NT = (((1,), (1,)), ((), ()))
SIGN_BIT = np.uint32(0x80000000)

MAIN_SB = 0
MAIN_DIL = MAIN_SB + H_SB
MAIN_MOBA = MAIN_DIL + H_DIL_PER_GROUP
MAIN_FOX = MAIN_MOBA + H_MOBA
MAIN_W = (MAIN_FOX + H_FOX) * HEAD_DIM
DIL_GROUP_W = H_DIL_PER_GROUP * HEAD_DIM
DIL_W = 128
DIL_TQ = 256
MOE_ROWS = 256
SB_T = 1024
SB_SEG = 256
SB_DEAD_LOG = -110.0
FOX_TQ = 1024
FOX_TK = 1024
FOX_WIDE = 2
MOBA_T = 1024
MOBA_NT = 8
AUG_PARTS = 3


def _params(sem, vmem_mib):
    return pltpu.CompilerParams(dimension_semantics=sem, vmem_limit_bytes=vmem_mib * MIB)


def _t5_bucket(n):
    exact = N_BUCKETS // 2
    nf = jnp.maximum(n, 1).astype(F32)
    large = exact + (jnp.log(nf / exact) / math.log(MAX_DIST / exact) * (N_BUCKETS - exact)).astype(jnp.int32)
    large = jnp.minimum(large, N_BUCKETS - 1)
    return jnp.where(n < exact, n, large)


def _bias_lookup(table, bucket):
    onehot = (bucket[..., None] == jnp.arange(N_BUCKETS)).astype(F32)
    return jnp.einsum('...k,kh->h...', onehot, table.astype(F32), precision=HIGHEST)


def _head_pair(q):
    lane = lax.broadcasted_iota(jnp.int32, q.shape, 1)
    zero = jnp.zeros_like(q)
    return [jnp.where(lane < HEAD_DIM, q, zero), jnp.where(lane >= HEAD_DIM, q, zero)], lane


def _attention_specs(seq, nq, tq, q_col, k_col, v_col):
    return [pl.BlockSpec((tq, LANES), lambda b, p, i: (b * nq + i, q_col + p)),
            pl.BlockSpec((seq, LANES), lambda b, p, i: (b, k_col + p)),
            pl.BlockSpec((seq, LANES), lambda b, p, i: (b, v_col + p))]


def _head_cols(off, width):
    q_col = off * HEAD_DIM // LANES
    return q_col, q_col + width // LANES, q_col + 2 * width // LANES


def _matmul_kernel(a_ref, w_ref, o_ref):
    o_ref[...] = jnp.dot(a_ref[...], w_ref[...], preferred_element_type=F32).astype(o_ref.dtype)


def _matmul(a, w, layer, col0, n, tm, tn, out_dtype, name):
    m, k = a.shape
    first = col0 // tn
    return pl.pallas_call(
        _matmul_kernel,
        grid=(n // tn, m // tm),
        in_specs=[pl.BlockSpec((tm, k), lambda j, i: (i, 0)),
                  pl.BlockSpec((None, k, tn), lambda j, i: (layer, 0, first + j))],
        out_specs=pl.BlockSpec((tm, tn), lambda j, i: (i, j)),
        out_shape=jax.ShapeDtypeStruct((m, n), out_dtype),
        compiler_params=_params(("parallel", "parallel"), 40),
        name=name,
    )(a, w)


QKV_TILE = 256
MAIN_HEADS = ((SB_OFF, H_SB), (DIL_OFF, H_DIL_PER_GROUP), (MOBA_OFF, H_MOBA), (FOX_OFF, H_FOX))


def _qkv_column_plan():
    groups = [MAIN_HEADS] + [((DIL_OFF + g * H_DIL_PER_GROUP, H_DIL_PER_GROUP),) for g in range(1, len(DIL_GROUPS))]
    src, is_q = [], []
    for heads in groups:
        for sec in range(3):
            for lo, cnt in heads:
                start = sec * W_MIX + lo * HEAD_DIM
                assert start % QKV_TILE == 0 and (cnt * HEAD_DIM) % QKV_TILE == 0
                for b in range(cnt * HEAD_DIM // QKV_TILE):
                    src.append(start // QKV_TILE + b)
                    is_q.append(int(sec == 0))
    assert sorted(src) == list(range(C_QKV // QKV_TILE))
    return np.asarray(src, np.int32), np.asarray(is_q, np.int32)


def _qkv_weight_kernel(src_ref, is_q_ref, w_ref, o_ref):
    del src_ref
    scale = jnp.where(is_q_ref[pl.program_id(1)] == 1, HEAD_DIM ** -0.5, 1.0)
    o_ref[...] = (w_ref[...].T * scale).astype(BF16)


def _qkv_weights(w_in_t):
    depth, _, d = w_in_t.shape
    src, is_q = _qkv_column_plan()
    grid_spec = pltpu.PrefetchScalarGridSpec(
        num_scalar_prefetch=2,
        grid=(depth, len(src)),
        in_specs=[pl.BlockSpec((None, QKV_TILE, d), lambda l, j, src, is_q: (l, src[j], 0))],
        out_specs=pl.BlockSpec((None, d, QKV_TILE), lambda l, j, src, is_q: (l, 0, j)),
    )
    return pl.pallas_call(
        _qkv_weight_kernel,
        grid_spec=grid_spec,
        out_shape=jax.ShapeDtypeStruct((depth, d, C_QKV), BF16),
        compiler_params=_params(("parallel", "parallel"), 24),
        name="qkv_weights",
    )(jnp.asarray(src), jnp.asarray(is_q), w_in_t)


GATE_COL0 = C_QKV + H_FOX
GATE_ALIGNED = GATE_COL0 // LANES * LANES
GATE_TILE = 256


def _gate_weight_kernel(a_ref, b_ref, o_ref, wf_ref):
    shift = GATE_COL0 - GATE_ALIGNED
    a = a_ref[...]
    both = jnp.concatenate([a, b_ref[...]], axis=0)
    o_ref[...] = both[shift:shift + GATE_TILE].T.astype(BF16)

    @pl.when(pl.program_id(1) == 0)
    def _():
        head = a[:LANES].T
        lane = lax.broadcasted_iota(jnp.int32, head.shape, 1)
        wf_ref[...] = jnp.where(lane < H_FOX, head, 0.0)


def _gate_weights(w_in_t):
    depth, _, d = w_in_t.shape
    width = N_BRANCH * d
    assert GATE_ALIGNED % GATE_TILE == 0 and width % GATE_TILE == 0 and GATE_ALIGNED == C_QKV
    a0 = GATE_ALIGNED // GATE_TILE
    per = GATE_TILE // LANES
    return pl.pallas_call(
        _gate_weight_kernel,
        grid=(depth, width // GATE_TILE),
        in_specs=[pl.BlockSpec((None, GATE_TILE, d), lambda l, j: (l, a0 + j, 0)),
                  pl.BlockSpec((None, LANES, d), lambda l, j: (l, (a0 + j + 1) * per, 0))],
        out_specs=[pl.BlockSpec((None, d, GATE_TILE), lambda l, j: (l, 0, j)),
                   pl.BlockSpec((None, d, LANES), lambda l, j: (l, 0, 0))],
        out_shape=[jax.ShapeDtypeStruct((depth, d, width), BF16), jax.ShapeDtypeStruct((depth, d, LANES), F32)],
        compiler_params=_params(("parallel", "arbitrary"), 24),
        name="gate_weights",
    )(w_in_t, w_in_t)


def _aug_constants():
    pairs = H_FOX // 2
    place_k = np.zeros((AUG_PARTS, LANES, pairs * LANES), np.float32)
    place_q = np.zeros((AUG_PARTS, LANES, pairs * LANES), np.float32)
    ones_k = np.zeros((1, pairs * LANES), np.float32)
    ones_q = np.zeros((1, pairs * LANES), np.float32)
    for g in range(H_FOX):
        p, h = divmod(g, 2)
        base = p * LANES + HEAD_DIM * (1 - h)
        for j in range(AUG_PARTS):
            place_k[j, g, base + j] = 1.0
            place_q[j, g, base + AUG_PARTS + j] = 1.0
            ones_k[0, base + AUG_PARTS + j] = 1.0
            ones_q[0, base + j] = 1.0
    return place_k, place_q, ones_k, ones_q


def _split_parts(x):
    parts = []
    for _ in range(AUG_PARTS):
        piece = x.astype(BF16)
        parts.append(piece)
        x = x - piece.astype(F32)
    return parts


def _decay_kernel(x_ref, whi_ref, wlo_ref, bf_ref, pk_ref, pq_ref, ok_ref, oq_ref, augq_ref, augk_ref, carry_ref):
    @pl.when(pl.program_id(1) == 0)
    def _():
        carry_ref[...] = jnp.zeros_like(carry_ref)

    tm = x_ref.shape[0]
    x = x_ref[...]
    f = (jnp.dot(x, whi_ref[...], preferred_element_type=F32)
         + jnp.dot(x, wlo_ref[...], preferred_element_type=F32) + bf_ref[...])
    logf = jnp.minimum(f, 0.0) - jnp.log1p(jnp.exp(-jnp.abs(f)))
    row = lax.broadcasted_iota(jnp.int32, (tm, tm), 0)
    col = lax.broadcasted_iota(jnp.int32, (tm, tm), 1)
    tri = jnp.where(col <= row, 1.0, 0.0).astype(F32)
    c = jnp.dot(tri, logf, preferred_element_type=F32, precision=HIGHEST) + carry_ref[...]
    carry_ref[...] = c[tm - 1:tm, :]
    aug_q = oq_ref[...]
    aug_k = ok_ref[...]
    for j, piece in enumerate(_split_parts(c)):
        aug_q = aug_q + jnp.dot(piece, pq_ref[j], preferred_element_type=F32)
        aug_k = aug_k - jnp.dot(piece, pk_ref[j], preferred_element_type=F32)
    augq_ref[...] = aug_q.astype(BF16)
    augk_ref[...] = aug_k.astype(BF16)


def _decay(x, w_f, b_f, batch, tm=256):
    n, d = x.shape
    nb = n // batch // tm
    width = H_FOX // 2 * LANES
    place_k, place_q, ones_k, ones_q = _aug_constants()
    w_hi = w_f.astype(BF16)
    w_lo = (w_f - w_hi.astype(F32)).astype(BF16)
    const = lambda a: pl.BlockSpec(a.shape, lambda b, i: (0,) * a.ndim)
    out = pl.BlockSpec((tm, width), lambda b, i: (b * nb + i, 0))
    return pl.pallas_call(
        _decay_kernel,
        grid=(batch, nb),
        in_specs=[pl.BlockSpec((tm, d), lambda b, i: (b * nb + i, 0)),
                  pl.BlockSpec((d, LANES), lambda b, i: (0, 0)),
                  pl.BlockSpec((d, LANES), lambda b, i: (0, 0)),
                  pl.BlockSpec((1, LANES), lambda b, i: (0, 0)),
                  const(place_k), const(place_q), const(ones_k), const(ones_q)],
        out_specs=[out, out],
        out_shape=[jax.ShapeDtypeStruct((n, width), BF16)] * 2,
        scratch_shapes=[pltpu.VMEM((1, LANES), F32)],
        compiler_params=_params(("arbitrary", "arbitrary"), 24),
        name="fox_decay",
    )(x, w_hi, w_lo, b_f, jnp.asarray(place_k, BF16), jnp.asarray(place_q, BF16), jnp.asarray(ones_k),
      jnp.asarray(ones_q))


def _sb_kernel(q_ref, k_ref, v_ref, o_ref):
    t, seg = SB_T, SB_SEG
    nseg = t // seg
    i = pl.program_id(2)
    qh, lane = _head_pair(q_ref[...])
    qneg = [-x for x in qh]
    r = lax.broadcasted_iota(jnp.int32, (seg, seg), 0)
    c = lax.broadcasted_iota(jnp.int32, (seg, seg), 1)
    later = jnp.where(r > c, 1.0, 0.0).astype(BF16)
    row = lax.broadcasted_iota(jnp.int32, (t, t), 0)
    col = lax.broadcasted_iota(jnp.int32, (t, t), 1)
    past = col < row

    def step(j, carry, diagonal):
        start = pl.multiple_of(j * t, t)
        k = k_ref[pl.ds(start, t), :]
        v = v_ref[pl.ds(start, t), :]
        out = []
        for h in range(2):
            tail, acc = carry[h]
            nz = lax.dot_general(qneg[h], k, NT, preferred_element_type=F32)
            minus_abs = lax.bitcast_convert_type(lax.bitcast_convert_type(nz, jnp.uint32) | SIGN_BIT, F32)
            log_1m = jnp.minimum(nz, 0.0) - jnp.log(1.0 + jnp.exp(minus_abs))
            base = log_1m - nz
            if diagonal:
                log_1m = jnp.where(past, log_1m, 0.0)
            sums = [jnp.dot(log_1m[:, b * seg:(b + 1) * seg].astype(BF16), later, preferred_element_type=F32)
                    for b in range(nseg)]
            for b in reversed(range(nseg)):
                sl = slice(b * seg, (b + 1) * seg)
                w = jnp.exp(base[:, sl] + (sums[b] + tail))
                if diagonal:
                    w = jnp.where(past[:, sl], w, 0.0)
                acc = acc + jnp.dot(w.astype(BF16), v[sl], preferred_element_type=F32)
                tail = tail + (sums[b][:, 0:1] + log_1m[:, b * seg:b * seg + 1])
            out.append((tail, acc))
        return tuple(out)

    init = tuple((jnp.zeros((t, 1), F32), jnp.zeros((t, LANES), F32)) for _ in range(2))
    carry = step(i, init, True)

    def live(state):
        jj, carry = state
        return (jj < i) & (jnp.maximum(jnp.max(carry[0][0]), jnp.max(carry[1][0])) > SB_DEAD_LOG)

    def walk(state):
        jj, carry = state
        return jj + 1, step(i - 1 - jj, carry, False)

    _, carry = lax.while_loop(live, walk, (jnp.int32(0), carry))
    o_ref[...] = jnp.where(lane < HEAD_DIM, carry[0][1], carry[1][1]).astype(o_ref.dtype)


def _sb_attention(qkv, batch):
    n = qkv.shape[0]
    seq = n // batch
    nq = seq // SB_T
    pairs = H_SB // 2
    return pl.pallas_call(
        _sb_kernel,
        grid=(batch, pairs, nq),
        in_specs=_attention_specs(seq, nq, SB_T, *_head_cols(MAIN_SB, MAIN_W)),
        out_specs=pl.BlockSpec((SB_T, LANES), lambda b, p, i: (b * nq + i, p)),
        out_shape=jax.ShapeDtypeStruct((n, pairs * LANES), BF16),
        compiler_params=_params(("parallel", "parallel", "arbitrary"), 56),
        name="sb_attention",
    )(qkv, qkv, qkv)


def _fox_kernel(q_ref, k_ref, v_ref, aq_ref, ak_ref, o_ref):
    tq, tk = FOX_TQ, FOX_TK
    i = pl.program_id(2)
    q = q_ref[...]
    lane = lax.broadcasted_iota(jnp.int32, q.shape, 1)
    own = [lane < HEAD_DIM, lane >= HEAD_DIM]
    qh = [jnp.where(own[h], q, aq_ref[...]) for h in range(2)]
    klane = lax.broadcasted_iota(jnp.int32, (tk, LANES), 1)
    kown = [klane < HEAD_DIM, klane >= HEAD_DIM]
    row = lax.broadcasted_iota(jnp.int32, (tq, tk), 0)
    col = lax.broadcasted_iota(jnp.int32, (tq, tk), 1)

    def step(start, width, carry, diagonal):
        k = k_ref[pl.ds(start, width), :]
        v = v_ref[pl.ds(start, width), :]
        ak = ak_ref[pl.ds(start, width), :]
        kown_w = [jnp.concatenate([x] * (width // tk), axis=0) if width > tk else x for x in kown]
        out = []
        for h in range(2):
            s = lax.dot_general(qh[h], jnp.where(kown_w[h], k, ak), NT, preferred_element_type=F32)
            if diagonal:
                s = jnp.where(col + start <= row + i * tq, s, NEG)
            m, acc = carry[h]
            m_new = jnp.maximum(m, jnp.max(s, axis=1, keepdims=True))
            p = jnp.exp(s - m_new).astype(BF16)
            vh = jnp.where(kown_w[h], v, jnp.ones_like(v))
            acc = jnp.exp(m - m_new) * acc + jnp.dot(p, vh, preferred_element_type=F32)
            out.append((m_new, acc))
        return tuple(out)

    assert tq == tk
    carry = tuple((jnp.full((tq, 1), NEG, F32), jnp.zeros((tq, LANES), F32)) for _ in range(2))
    wide = FOX_WIDE * tk
    n_wide = (i * tq) // wide
    carry = lax.fori_loop(0, n_wide, lambda j, c: step(pl.multiple_of(j * wide, wide), wide, c, False), carry)
    rest = (i * tq - n_wide * wide) // tk
    carry = lax.fori_loop(0, rest, lambda j, c: step(pl.multiple_of(n_wide * wide + j * tk, tk), tk, c, False),
                          carry)
    carry = step(pl.multiple_of(i * tq, tk), tk, carry, True)
    o = []
    for h in range(2):
        acc = carry[h][1]
        o.append(acc / acc[:, HEAD_DIM * (1 - h):HEAD_DIM * (1 - h) + 1])
    o_ref[...] = jnp.where(own[0], o[0], o[1]).astype(o_ref.dtype)


def _fox_attention(qkv, aug_q, aug_k, batch):
    n = qkv.shape[0]
    seq = n // batch
    nq = seq // FOX_TQ
    pairs = H_FOX // 2
    specs = _attention_specs(seq, nq, FOX_TQ, *_head_cols(MAIN_FOX, MAIN_W))
    specs += [pl.BlockSpec((FOX_TQ, LANES), lambda b, p, i: (b * nq + i, p)),
              pl.BlockSpec((seq, LANES), lambda b, p, i: (b, p))]
    return pl.pallas_call(
        _fox_kernel,
        grid=(batch, pairs, nq),
        in_specs=specs,
        out_specs=pl.BlockSpec((FOX_TQ, LANES), lambda b, p, i: (b * nq + i, p)),
        out_shape=jax.ShapeDtypeStruct((n, pairs * LANES), BF16),
        compiler_params=_params(("parallel", "parallel", "arbitrary"), 48),
        name="fox_attention",
    )(qkv, qkv, qkv, aug_q, aug_k)


def _moba_kernel(q_ref, k_ref, v_ref, onehot_ref, bias_ref, o_ref, kmean_ref):
    t, mb = MOBA_T, MOBA_BLOCK
    nb = t // mb
    seq = k_ref.shape[0]
    i = pl.program_id(2)

    @pl.when(i == 0)
    def _():
        blk = lax.broadcasted_iota(jnp.int32, (LANES, seq), 0)
        pos = lax.broadcasted_iota(jnp.int32, (LANES, seq), 1)
        member = jnp.where((pos >= blk * mb) & (pos < (blk + 1) * mb), 1.0, 0.0).astype(BF16)
        mean = jnp.dot(member, k_ref[...], preferred_element_type=F32) * (1.0 / mb)
        hi = mean.astype(BF16)
        kmean_ref[0] = hi
        kmean_ref[1] = (mean - hi.astype(F32)).astype(BF16)

    q = q_ref[...]
    lane = lax.broadcasted_iota(jnp.int32, (t, LANES), 1)
    own_half = [lane < HEAD_DIM, lane >= HEAD_DIM]
    zero = jnp.zeros_like(q)

    nblk = seq // mb
    blk_row = lax.broadcasted_iota(jnp.int32, (nblk, t), 0).astype(F32)
    own_row = ((i * t + lax.broadcasted_iota(jnp.int32, (nblk, t), 1)) // mb).astype(F32)
    fully_past = blk_row < own_row
    place_r = lax.broadcasted_iota(jnp.int32, (nblk, LANES), 0)
    place_c = lax.broadcasted_iota(jnp.int32, (nblk, LANES), 1)
    qh = []
    for h in range(2):
        q_only = jnp.where(own_half[h], q, zero)
        gate = (lax.dot_general(kmean_ref[0][:nblk], q_only, NT, preferred_element_type=F32)
                + lax.dot_general(kmean_ref[1][:nblk], q_only, NT, preferred_element_type=F32))
        gate = jnp.where(fully_past, gate, -jnp.inf)
        picked = jnp.zeros((nblk, t), jnp.bool_)
        for _ in range(MOBA_TOPK):
            best = jnp.max(gate, axis=0, keepdims=True)
            first = jnp.min(jnp.where(gate == best, blk_row, float(LANES)), axis=0, keepdims=True)
            pick = blk_row == first
            picked = picked | pick
            gate = jnp.where(pick, -jnp.inf, gate)
        allowed = jnp.where((picked & fully_past) | (blk_row == own_row), 1.0, 0.0).astype(BF16)
        place = jnp.where(place_c == place_r + HEAD_DIM * (1 - h), 1.0, 0.0).astype(BF16)
        spread = lax.dot_general(allowed, place, (((0,), (0,)), ((), ())), preferred_element_type=F32)
        penalty = jnp.where(spread > 0.5, 0.0, NEG)
        qh.append(jnp.where(own_half[h], q, penalty.astype(BF16)))

    def step(j, carry):
        start = pl.multiple_of(j * t, t)
        k = k_ref[pl.ds(start, t), :]
        v = v_ref[pl.ds(start, t), :]
        onehot = onehot_ref[pl.ds(start, t), :]
        out = []
        for h in range(2):
            s = lax.dot_general(qh[h], jnp.where(own_half[h], k, onehot), NT, preferred_element_type=F32)
            rows = []
            for a in range(nb):
                tiles = [bias_ref[h, jnp.clip((i - j) * nb + a - b, 0, MOBA_NT)] for b in range(nb)]
                rows.append(jnp.concatenate(tiles, axis=1))
            s = s + jnp.concatenate(rows, axis=0)
            m, acc = carry[h]
            m_new = jnp.maximum(m, jnp.max(s, axis=1, keepdims=True))
            p = jnp.exp(s - m_new).astype(BF16)
            vh = jnp.where(own_half[h], v, jnp.ones_like(v))
            acc = jnp.exp(m - m_new) * acc + jnp.dot(p, vh, preferred_element_type=F32)
            out.append((m_new, acc))
        return tuple(out)

    carry = tuple((jnp.full((t, 1), NEG, F32), jnp.zeros((t, LANES), F32)) for _ in range(2))
    carry = lax.fori_loop(0, i + 1, step, carry)
    o = []
    for h in range(2):
        acc = carry[h][1]
        o.append(acc / acc[:, HEAD_DIM * (1 - h):HEAD_DIM * (1 - h) + 1])
    o_ref[...] = jnp.where(own_half[0], o[0], o[1]).astype(o_ref.dtype)


def _moba_attention(qkv, bias_tiles, batch):
    n = qkv.shape[0]
    seq = n // batch
    assert seq % MOBA_T == 0 and seq // MOBA_BLOCK <= HEAD_DIM
    nq = seq // MOBA_T
    pairs = H_MOBA // 2
    blk = np.arange(seq)[:, None] // MOBA_BLOCK
    onehot = jnp.asarray(blk == (np.arange(LANES)[None, :] % HEAD_DIM), BF16)
    specs = _attention_specs(seq, nq, MOBA_T, *_head_cols(MAIN_MOBA, MAIN_W))
    specs.append(pl.BlockSpec((seq, LANES), lambda b, p, i: (0, 0)))
    specs.append(pl.BlockSpec((None, 2, MOBA_NT + 1, MOBA_BLOCK, MOBA_BLOCK), lambda b, p, i: (p, 0, 0, 0, 0)))
    return pl.pallas_call(
        _moba_kernel,
        grid=(batch, pairs, nq),
        in_specs=specs,
        out_specs=pl.BlockSpec((MOBA_T, LANES), lambda b, p, i: (b * nq + i, p)),
        out_shape=jax.ShapeDtypeStruct((n, pairs * LANES), BF16),
        scratch_shapes=[pltpu.VMEM((2, LANES, LANES), BF16)],
        compiler_params=_params(("arbitrary", "arbitrary", "arbitrary"), 56),
        name="moba_attention",
    )(qkv, qkv, qkv, onehot, bias_tiles)


def _moba_bias_tiles(rel_bias):
    r = jnp.arange(MOBA_BLOCK)
    dist = (jnp.arange(MOBA_NT + 1)[:, None, None] * MOBA_BLOCK + r[None, :, None] - r[None, None, :])
    tiles = _bias_lookup(rel_bias[:, H_DIL:], _t5_bucket(jnp.maximum(dist, 0)))
    tiles = jnp.where(dist >= 0, tiles, NEG)
    return tiles.reshape(H_MOBA // 2, 2, MOBA_NT + 1, MOBA_BLOCK, MOBA_BLOCK).astype(F32)


def _check_moba_saturation():
    n = np.float32(MOBA_NT * MOBA_BLOCK - (MOBA_BLOCK - 1))
    exact = N_BUCKETS // 2
    large = exact + int(np.log(n / np.float32(exact)) / math.log(MAX_DIST / exact) * (N_BUCKETS - exact))
    assert large >= N_BUCKETS - 1, "MOBA_NT too small for the bias bucket table"


_check_moba_saturation()


def _dilated_kernel(q_ref, k_ref, v_ref, bias_ref, o_ref, lse_ref):
    tq, tk = DIL_TQ, DIL_TQ + DIL_W
    n_tiles = q_ref.shape[0] // tq

    def body(u, _):
        q0 = pl.multiple_of(u * tq, tq)
        k0 = pl.multiple_of(jnp.maximum(q0 - DIL_W, 0), DIL_W)
        variant = jnp.minimum(u, 1)
        qh, lane = _head_pair(q_ref[pl.ds(q0, tq), :])
        k = k_ref[pl.ds(k0, tk), :]
        v = v_ref[pl.ds(k0, tk), :]
        outs, lses = [], []
        for h in range(2):
            z = lax.dot_general(qh[h], k, NT, preferred_element_type=F32) + bias_ref[h, variant]
            m = jnp.max(z, axis=1, keepdims=True)
            p = jnp.exp(z - m)
            l = jnp.sum(p, axis=1, keepdims=True)
            outs.append(jnp.dot(p.astype(BF16), v, preferred_element_type=F32) / l)
            lses.append(m + jnp.log(l))
        o_ref[pl.ds(q0, tq), :] = jnp.where(lane < HEAD_DIM, outs[0], outs[1])
        lse_ref[pl.ds(q0, tq), :] = jnp.where(lane < HEAD_DIM, lses[0], lses[1])
        return 0

    lax.fori_loop(0, n_tiles, body, 0)


def _dilated_group(qkv, head_off, width, bias, batch, group):
    window, dil = DIL_GROUPS[group]
    assert window // dil == DIL_W
    n = qkv.shape[0]
    sub = n // batch // dil
    assert sub % DIL_TQ == 0 and sub >= DIL_TQ + DIL_W
    pairs = H_DIL_PER_GROUP // 2
    q_col, k_col, v_col = _head_cols(head_off, width)

    def spec(col0):
        return pl.BlockSpec((sub, LANES), lambda b, p, r: (b * dil + r, col0 + p))

    out_spec = pl.BlockSpec((sub, LANES), lambda b, p, r: (b * dil + r, p))
    out_shape = jax.ShapeDtypeStruct((n, pairs * LANES), F32)
    return pl.pallas_call(
        _dilated_kernel,
        grid=(batch, pairs, dil),
        in_specs=[spec(q_col), spec(k_col), spec(v_col),
                  pl.BlockSpec((None, 2, 2, DIL_TQ, DIL_TQ + DIL_W),
                               lambda b, p, r: (group * pairs + p, 0, 0, 0, 0))],
        out_specs=[out_spec, out_spec],
        out_shape=[out_shape, out_shape],
        compiler_params=_params(("parallel", "parallel", "parallel"), 48),
        name=f"dilated_attention_g{group}",
    )(qkv, qkv, qkv, bias)


def _by_residue(a, batch, dil):
    n, c = a.shape
    return a.reshape(batch, n // batch // dil, dil, c).transpose(0, 2, 1, 3).reshape(n, c)


def _by_position(a, batch, dil):
    n, c = a.shape
    return a.reshape(batch, dil, n // batch // dil, c).transpose(0, 2, 1, 3).reshape(n, c)


def _dilated_bias_tiles(rel_bias):
    r = jnp.arange(DIL_TQ)[:, None]
    c = jnp.arange(DIL_TQ + DIL_W)[None, :]
    steps = jnp.stack([r - c, r + DIL_W - c], 0)
    inside = (steps >= 0) & (steps <= DIL_W)
    tiles = []
    for g, (_, dil) in enumerate(DIL_GROUPS):
        tab = rel_bias[:, g * H_DIL_PER_GROUP:(g + 1) * H_DIL_PER_GROUP]
        bias = _bias_lookup(tab, _t5_bucket(jnp.maximum(steps, 0) * dil))
        tiles.append(jnp.where(inside, bias, NEG))
    tiles = jnp.stack(tiles, 0).astype(F32)
    return tiles.reshape(len(DIL_GROUPS) * H_DIL_PER_GROUP // 2, 2, 2, DIL_TQ, DIL_TQ + DIL_W)


def _dilated_combine_kernel(o0, o1, o2, l0, l1, l2, out_ref):
    lse = [l0[...], l1[...], l2[...]]
    m = jnp.maximum(jnp.maximum(lse[0], lse[1]), lse[2])
    e = [jnp.exp(x - m) for x in lse]
    den = e[0] + e[1] + e[2]
    out = (e[0] / den) * o0[...] + (e[1] / den) * o1[...] + (e[2] / den) * o2[...]
    out_ref[...] = out.astype(out_ref.dtype)


def _dilated_combine(os, lses, tm=1024):
    n, c = os[0].shape
    spec = pl.BlockSpec((tm, c), lambda i: (i, 0))
    return pl.pallas_call(
        _dilated_combine_kernel,
        grid=(n // tm,),
        in_specs=[spec] * 6,
        out_specs=spec,
        out_shape=jax.ShapeDtypeStruct((n, c), BF16),
        compiler_params=_params(("parallel",), 32),
        name="dilated_combine",
    )(*os, *lses)


def _merge_kernel(x_ref, o_sb, o_dil, o_moba, o_fox, wg0, wg1, wg2, wg3, bg_ref,
                  w_sb, w_dil, w_moba, w_fox, out_ref):
    x = x_ref[...]
    acc = None
    branches = ((o_sb, w_sb, wg0), (o_dil, w_dil, wg1), (o_moba, w_moba, wg2), (o_fox, w_fox, wg3))
    for b, (o_ref, w_ref, wg_ref) in enumerate(branches):
        gate = jax.nn.sigmoid(jnp.dot(x, wg_ref[...], preferred_element_type=F32) + bg_ref[b])
        term = gate * jnp.dot(o_ref[...], w_ref[...], preferred_element_type=F32)
        acc = term if acc is None else acc + term
    out_ref[...] = acc.astype(out_ref.dtype)


def _gated_merge(x_bf, outs, w_gate, b_gate, w_branch, layer, tm=512, tn=512):
    n, d = x_bf.shape
    nj = d // tn
    row = lambda width: pl.BlockSpec((tm, width), lambda j, i: (i, 0))
    in_specs = [row(d)] + [row(o.shape[1]) for o in outs]
    in_specs += [pl.BlockSpec((None, d, tn), functools.partial(lambda j, i, b: (layer, 0, b * nj + j), b=b))
                 for b in range(N_BRANCH)]
    in_specs.append(pl.BlockSpec((None, N_BRANCH, 1, tn), lambda j, i: (layer, 0, 0, j)))
    in_specs += [pl.BlockSpec((None, w.shape[1], tn), lambda j, i: (layer, 0, j)) for w in w_branch]
    return pl.pallas_call(
        _merge_kernel,
        grid=(nj, n // tm),
        in_specs=in_specs,
        out_specs=pl.BlockSpec((tm, tn), lambda j, i: (i, j)),
        out_shape=jax.ShapeDtypeStruct((n, d), BF16),
        compiler_params=_params(("parallel", "parallel"), 48),
        name="gated_merge",
    )(x_bf, *outs, w_gate, w_gate, w_gate, w_gate, b_gate, *w_branch)


def _layer_norm_store(z, g_ref, b_ref, xo_ref, xb_ref):
    mu = jnp.mean(z, axis=1, keepdims=True)
    zc = z - mu
    var = jnp.mean(zc * zc, axis=1, keepdims=True)
    out = zc * lax.rsqrt(var + LN_EPS) * g_ref[...] + b_ref[...]
    xo_ref[...] = out
    xb_ref[...] = out.astype(BF16)


def _out_ln_kernel(m_ref, w_ref, x_ref, g_ref, b_ref, xo_ref, xb_ref):
    y = jnp.dot(m_ref[...], w_ref[...], preferred_element_type=F32)
    _layer_norm_store(DN_ALPHA * x_ref[...] + y, g_ref, b_ref, xo_ref, xb_ref)


def _combine_ln_kernel(y_ref, gate_ref, x_ref, g_ref, b_ref, xo_ref, xb_ref):
    gate = gate_ref[...]
    z = DN_ALPHA * x_ref[...]
    for k in range(TOP_K):
        z = z + gate[:, k:k + 1] * y_ref[k].astype(F32)
    _layer_norm_store(z, g_ref, b_ref, xo_ref, xb_ref)


def _ln_call(kernel, lead_specs, lead_args, x, g, b, tm, name):
    n, d = x.shape
    row = pl.BlockSpec((tm, d), lambda i: (i, 0))
    vec = pl.BlockSpec((1, d), lambda i: (0, 0))
    return pl.pallas_call(
        kernel,
        grid=(n // tm,),
        in_specs=lead_specs + [row, vec, vec],
        out_specs=[row, row],
        out_shape=[jax.ShapeDtypeStruct((n, d), F32), jax.ShapeDtypeStruct((n, d), BF16)],
        compiler_params=_params(("parallel",), 48),
        name=name,
    )(*lead_args, x, g, b)


def _out_proj_ln(merged, w_out, layer, x, g, b, tm=256):
    d = x.shape[1]
    specs = [pl.BlockSpec((tm, d), lambda i: (i, 0)), pl.BlockSpec((None, d, d), lambda i: (layer, 0, 0))]
    return _ln_call(_out_ln_kernel, specs, (merged, w_out), x, g, b, tm, "out_proj_ln")


def _combine_ln(yk, gate, x, g, b, tm=256):
    d = x.shape[1]
    specs = [pl.BlockSpec((TOP_K, tm, d), lambda i: (0, i, 0)), pl.BlockSpec((tm, LANES), lambda i: (i, 0))]
    return _ln_call(_combine_ln_kernel, specs, (yk, gate), x, g, b, tm, "moe_combine_ln")


def _router_kernel(x_ref, w_ref, b_ref, idx_ref, gate_ref, rank_ref, count_ref, seen_ref):
    @pl.when(pl.program_id(0) == 0)
    def _():
        seen_ref[...] = jnp.zeros_like(seen_ref)

    tm = x_ref.shape[0]
    logits = jnp.dot(x_ref[...], w_ref[...], preferred_element_type=F32, precision=HIGHEST) + b_ref[...]
    col = lax.broadcasted_iota(jnp.int32, logits.shape, 1)
    colf = col.astype(F32)
    logits = jnp.where(col < N_EXPERTS, logits, -jnp.inf)
    vals, idxs, hits = [], [], []
    for _ in range(TOP_K):
        best = jnp.max(logits, axis=1, keepdims=True)
        first = jnp.min(jnp.where(logits == best, colf, float(LANES)), axis=1, keepdims=True)
        hit = colf == first
        logits = jnp.where(hit, -jnp.inf, logits)
        vals.append(best)
        idxs.append(first)
        hits.append(jnp.where(hit, 1.0, 0.0))
    es = [jnp.exp(v - vals[0]) for v in vals]
    den = es[0] + es[1] + es[2] + es[3]

    chosen = hits[0] + hits[1] + hits[2] + hits[3]
    row = lax.broadcasted_iota(jnp.int32, (tm, tm), 0)
    before = jnp.where(lax.broadcasted_iota(jnp.int32, (tm, tm), 1) < row, 1.0, 0.0).astype(BF16)
    earlier = jnp.dot(before, chosen.astype(BF16), preferred_element_type=F32) + seen_ref[...]
    seen = seen_ref[...] + jnp.sum(chosen, axis=0, keepdims=True)
    seen_ref[...] = seen
    count_ref[...] = seen

    idx_out = jnp.zeros(logits.shape, F32)
    gate_out = jnp.zeros(logits.shape, F32)
    rank_out = jnp.zeros(logits.shape, F32)
    for r in range(TOP_K):
        idx_out = jnp.where(col == r, idxs[r], idx_out)
        gate_out = jnp.where(col == r, es[r] / den, gate_out)
        rank_out = jnp.where(col == r, jnp.sum(hits[r] * earlier, axis=1, keepdims=True), rank_out)
    idx_ref[...] = idx_out.astype(jnp.int32)
    gate_ref[...] = gate_out
    rank_ref[...] = rank_out.astype(jnp.int32)


def _router(x, w_r, b_r, tm=512):
    n, d = x.shape
    out = pl.BlockSpec((tm, LANES), lambda i: (i, 0))
    return pl.pallas_call(
        _router_kernel,
        grid=(n // tm,),
        in_specs=[pl.BlockSpec((tm, d), lambda i: (i, 0)),
                  pl.BlockSpec((d, LANES), lambda i: (0, 0)),
                  pl.BlockSpec((1, LANES), lambda i: (0, 0))],
        out_specs=[out, out, out, pl.BlockSpec((1, LANES), lambda i: (0, 0))],
        out_shape=[jax.ShapeDtypeStruct((n, LANES), jnp.int32), jax.ShapeDtypeStruct((n, LANES), F32),
                   jax.ShapeDtypeStruct((n, LANES), jnp.int32), jax.ShapeDtypeStruct((1, LANES), F32)],
        scratch_shapes=[pltpu.VMEM((1, LANES), F32)],
        compiler_params=_params(("arbitrary",), 32),
        name="moe_router",
    )(x, w_r, b_r)


def _expert_kernel(blk_exp_ref, xs_ref, wu_ref, bu_ref, wd_ref, bd_ref, *rest):
    y_ref = rest[-1]
    del blk_exp_ref
    h = jnp.dot(xs_ref[...], wu_ref[...].astype(BF16), preferred_element_type=F32) + bu_ref[...]
    glu = jnp.minimum(h[:, :D_FF], SWIGLU_LIMIT)
    lin = jnp.clip(h[:, D_FF:], -SWIGLU_LIMIT, SWIGLU_LIMIT)
    act = glu * jax.nn.sigmoid(SWIGLU_ALPHA * glu) * (lin + 1.0)
    y = jnp.dot(act.astype(BF16), wd_ref[...].astype(BF16), preferred_element_type=F32) + bd_ref[...]
    y_ref[...] = y.astype(y_ref.dtype)


def _expert_ffn(xs, blk_exp, w_up, b_up, w_down, b_down, layer, first_block, total_blocks, y_prev=None):
    p, d = xs.shape
    in_specs = [pl.BlockSpec((MOE_ROWS, d), lambda i, e: (i, 0)),
                pl.BlockSpec((None, None, d, 2 * D_FF), lambda i, e: (layer, e[i], 0, 0)),
                pl.BlockSpec((None, None, 1, 2 * D_FF), lambda i, e: (layer, e[i], 0, 0)),
                pl.BlockSpec((None, None, D_FF, d), lambda i, e: (layer, e[i], 0, 0)),
                pl.BlockSpec((None, None, 1, d), lambda i, e: (layer, e[i], 0, 0))]
    args = [blk_exp, xs, w_up, b_up, w_down, b_down]
    aliases = {}
    if y_prev is not None:
        in_specs.append(pl.BlockSpec(memory_space=pl.ANY))
        args.append(y_prev)
        aliases = {len(args) - 1: 0}
    grid_spec = pltpu.PrefetchScalarGridSpec(
        num_scalar_prefetch=1,
        grid=(p // MOE_ROWS,),
        in_specs=in_specs,
        out_specs=pl.BlockSpec((MOE_ROWS, d), lambda i, e: (first_block + i, 0)),
    )
    return pl.pallas_call(
        _expert_kernel,
        grid_spec=grid_spec,
        out_shape=jax.ShapeDtypeStruct((total_blocks * MOE_ROWS, d), BF16),
        input_output_aliases=aliases,
        compiler_params=_params(("arbitrary",), 48),
        name="expert_ffn",
    )(*args)


def _moe(x, x_bf, w_r, b_r, w_up, b_up, w_down, b_down, layer):
    n, d = x.shape
    nk = n * TOP_K
    idx, gate, rank, count = _router(x, w_r, b_r)
    counts = count[0, :N_EXPERTS].astype(jnp.int32)
    start = jnp.cumsum(counts) - counts
    padded = (counts + MOE_ROWS - 1) // MOE_ROWS * MOE_ROWS
    pend = jnp.cumsum(padded)
    pstart = pend - padded
    experts = jnp.arange(N_EXPERTS, dtype=jnp.int32)
    chosen = idx[:, :TOP_K, None] == experts
    slot = jnp.sum(jnp.where(chosen, pstart, 0), axis=2) + rank[:, :TOP_K]
    n_blocks = -(-(nk + N_EXPERTS * (MOE_ROWS - 1)) // MOE_ROWS)
    first_row = jnp.arange(n_blocks, dtype=jnp.int32) * MOE_ROWS
    blk_exp = jnp.minimum(jnp.sum(pend[None, :] <= first_row[:, None], axis=1), N_EXPERTS - 1).astype(jnp.int32)
    order = jnp.argsort(slot.reshape(nk)).astype(jnp.int32)
    blk_is = blk_exp[:, None] == experts
    blk_start = jnp.sum(jnp.where(blk_is, start, 0), axis=1)
    blk_pstart = jnp.sum(jnp.where(blk_is, pstart, 0), axis=1)
    blk_count = jnp.sum(jnp.where(blk_is, counts, 0), axis=1)
    within = first_row[:, None] + jnp.arange(MOE_ROWS, dtype=jnp.int32)[None, :] - blk_pstart[:, None]
    pair = jnp.clip(blk_start[:, None] + within, 0, nk - 1)
    spare = (first_row[:, None] + jnp.arange(MOE_ROWS, dtype=jnp.int32)[None, :]) % n
    slot_tok = jnp.where(within < blk_count[:, None], order[pair] // TOP_K, spare).reshape(n_blocks * MOE_ROWS)
    slot_tok, slot_km = lax.optimization_barrier((slot_tok, slot.T.reshape(nk)))
    half = n_blocks // 2
    y = None
    for first, stop in ((0, half), (half, n_blocks)):
        xs = x_bf[slot_tok[first * MOE_ROWS:stop * MOE_ROWS]]
        y = _expert_ffn(xs, blk_exp[first:stop], w_up, b_up, w_down, b_down, layer, first, n_blocks, y)
    return y[slot_km].reshape(TOP_K, n, d), gate


def kernel(x, rel_bias, w_in, b_forget, b_gate, w_br_sb, w_br_dil, w_br_moba, w_br_fox, w_out,
           ln1_g, ln1_b, w_router, b_router, w_up, b_up, w_down, b_down, ln2_g, ln2_b):
    batch, seq, d = x.shape
    n = batch * seq
    depth = w_in.shape[0]

    w_in_t = jnp.transpose(w_in, (0, 2, 1))
    w_qkv = _qkv_weights(w_in_t)
    w_gate, w_f = _gate_weights(w_in_t)
    b_f = jnp.pad(b_forget, ((0, 0), (0, LANES - H_FOX))).reshape(depth, 1, LANES)
    b_gate3 = b_gate.reshape(depth, N_BRANCH, 1, d)
    w_branch = [w.astype(BF16) for w in (w_br_sb, w_br_dil, w_br_moba, w_br_fox)]
    w_out_bf = w_out.astype(BF16)
    w_r = jnp.pad(w_router, ((0, 0), (0, 0), (0, LANES - N_EXPERTS)))
    b_r = jnp.pad(b_router, ((0, 0), (0, LANES - N_EXPERTS))).reshape(depth, 1, LANES)
    b_up3 = b_up.reshape(depth, N_EXPERTS, 1, 2 * D_FF)
    b_down3 = b_down.reshape(depth, N_EXPERTS, 1, d)
    moba_bias = _moba_bias_tiles(rel_bias)
    dil_bias = _dilated_bias_tiles(rel_bias)

    xf = x.reshape(n, d)
    xb = xf.astype(BF16)
    for l in range(depth):
        qkv = _matmul(xb, w_qkv, l, 0, 3 * MAIN_W, 1024, MAIN_W, BF16, "qkv_proj")
        aug_q, aug_k = _decay(xb, w_f[l], b_f[l], batch)
        o_sb = _sb_attention(qkv, batch)
        os, lses = [], []
        o, lse = _dilated_group(qkv, MAIN_DIL, MAIN_W, dil_bias, batch, 0)
        os.append(o)
        lses.append(lse)
        for g in range(1, len(DIL_GROUPS)):
            dil = DIL_GROUPS[g][1]
            qkv_g = _matmul(xb, w_qkv, l, 3 * (MAIN_W + (g - 1) * DIL_GROUP_W),
                            3 * DIL_GROUP_W, 1024, 3 * DIL_GROUP_W, BF16, f"qkv_proj_g{g}")
            o, lse = _dilated_group(_by_residue(qkv_g, batch, dil), 0, DIL_GROUP_W, dil_bias, batch, g)
            os.append(_by_position(o, batch, dil))
            lses.append(_by_position(lse, batch, dil))
        o_dil = _dilated_combine(os, lses)
        o_moba = _moba_attention(qkv, moba_bias, batch)
        o_fox = _fox_attention(qkv, aug_q, aug_k, batch)
        merged = _gated_merge(xb, (o_sb, o_dil, o_moba, o_fox), w_gate, b_gate3, w_branch, l)
        xf, xb = _out_proj_ln(merged, w_out_bf, l, xf, ln1_g[l].reshape(1, d), ln1_b[l].reshape(1, d))
        yk, gate = _moe(xf, xb, w_r[l], b_r[l], w_up, b_up3, w_down, b_down3, l)
        xf, xb = _combine_ln(yk, gate, xf, ln2_g[l].reshape(1, d), ln2_b[l].reshape(1, d))
    return xf.reshape(batch, seq, d)
```
